```python
import math
import jax, jax.numpy as jnp
from jax import lax
import numpy as np

D_MODEL = 2048
BATCH = 2
SEQ = 8192
DEPTH = 2

S5_WIDTH = D_MODEL // 2
S5_GROUP_CH = 16
S5_GROUPS = S5_WIDTH // S5_GROUP_CH
S5_STATE = 64
MLA_HEADS = 8
MLA_NOPE = 128
MLA_ROPE = 64
MLA_V = 128
MLA_Q_RANK = 512
MLA_KV_RANK = 512
ROPE_THETA = 10000.0
Q_BLOCK = 128
IN_COLS = S5_WIDTH + MLA_Q_RANK + MLA_KV_RANK + MLA_ROPE
MIX_OUT = S5_WIDTH + MLA_HEADS * MLA_V
CONV_WIDTH = D_MODEL
CONV_KERNEL = 31
N_EXPERTS = 32
TOP_K = 4
EXPERT_FF = D_MODEL
SWIGLU_ALPHA = 1.702
SWIGLU_LIMIT = 7.0
EXPERT_BLOCK = 128
EPS = 1e-6

kernel_name = 'hybrid_s5_mla_conformer_moe'


def rmsnorm(x, g):
    xf = x.astype(jnp.float32)
    y = xf * lax.rsqrt(jnp.mean(xf * xf, axis=-1, keepdims=True) + EPS)
    return (y * g.astype(jnp.float32)).astype(x.dtype)


def layernorm(x, g, b):
    xf = x.astype(jnp.float32)
    mu = jnp.mean(xf, axis=-1, keepdims=True)
    var = jnp.mean(jnp.square(xf - mu), axis=-1, keepdims=True)
    y = (xf - mu) * lax.rsqrt(var + 1e-5)
    return (y * g.astype(jnp.float32) + b.astype(jnp.float32)).astype(x.dtype)


def ada_modulation(c, w, b):
    m = jax.nn.silu(c) @ w + b
    shift, scale, gate = jnp.split(m, 3, axis=-1)
    return shift[:, None, :], scale[:, None, :], gate[:, None, :]


def rope_tables(positions):
    inv_freq = 1.0 / (ROPE_THETA ** (jnp.arange(0, MLA_ROPE, 2, dtype=jnp.float32) / MLA_ROPE))
    ang = positions.astype(jnp.float32)[..., None] * inv_freq
    return jnp.cos(ang), jnp.sin(ang)


def apply_rope(x, cos, sin):
    x1, x2 = jnp.split(x, 2, axis=-1)
    return jnp.concatenate([x1 * cos - x2 * sin, x1 * sin + x2 * cos], axis=-1).astype(x.dtype)


def _complex_scan_op(e1, e2):
    a1r, a1i, b1r, b1i = e1
    a2r, a2i, b2r, b2i = e2
    return (a2r * a1r - a2i * a1i,
            a2r * a1i + a2i * a1r,
            a2r * b1r - a2i * b1i + b2r,
            a2r * b1i + a2i * b1r + b2i)


def s5_mixer(u, lam_re, lam_im, log_step, b_re, b_im, c_re, c_im, d, w_glu, b_glu):
    bsz, seq, _ = u.shape
    f32 = jnp.float32
    uf = u.astype(f32).reshape(bsz, seq, S5_GROUPS, S5_GROUP_CH)
    lr = jnp.minimum(lam_re.astype(f32), -1e-4)
    li = lam_im.astype(f32)
    step = jnp.exp(log_step.astype(f32))[:, None]
    mag = jnp.exp(lr * step)
    ab_re = mag * jnp.cos(li * step)
    ab_im = mag * jnp.sin(li * step)
    denom = lr * lr + li * li
    nr, ni = ab_re - 1.0, ab_im
    ratio_re = (nr * lr + ni * li) / denom
    ratio_im = (ni * lr - nr * li) / denom
    br, bi = b_re.astype(f32), b_im.astype(f32)
    bb_re = ratio_re[..., None] * br - ratio_im[..., None] * bi
    bb_im = ratio_re[..., None] * bi + ratio_im[..., None] * br
    bu_re = jnp.einsum('blgh,gph->lbgp', uf, bb_re)
    bu_im = jnp.einsum('blgh,gph->lbgp', uf, bb_im)
    a_re = jnp.broadcast_to(ab_re[None, None], (seq, 1, S5_GROUPS, S5_STATE))
    a_im = jnp.broadcast_to(ab_im[None, None], (seq, 1, S5_GROUPS, S5_STATE))
    _, _, x_re, x_im = lax.associative_scan(_complex_scan_op, (a_re, a_im, bu_re, bu_im), axis=0)
    y = (jnp.einsum('lbgp,ghp->blgh', x_re, c_re.astype(f32))
         - jnp.einsum('lbgp,ghp->blgh', x_im, c_im.astype(f32))
         + d.astype(f32) * uf)
    y = jax.nn.gelu(y.reshape(bsz, seq, S5_WIDTH)).astype(u.dtype)
    return y * jax.nn.sigmoid(y @ w_glu + b_glu)


def causal_block_attention(q, k, v, scale):
    bsz, seq, nh, _ = q.shape
    dv = v.shape[-1]
    n_blocks = seq // Q_BLOCK
    key_pos = jnp.arange(seq)

    def one_block(i):
        qb = lax.dynamic_slice_in_dim(q, i * Q_BLOCK, Q_BLOCK, axis=1)
        s = jnp.einsum('bqhd,bkhd->bhqk', qb, k, preferred_element_type=jnp.float32) * scale
        q_pos = i * Q_BLOCK + jnp.arange(Q_BLOCK)
        s = jnp.where(key_pos[None, :] <= q_pos[:, None], s, -1e30)
        p = jax.nn.softmax(s, axis=-1).astype(v.dtype)
        return jnp.einsum('bhqk,bkhd->bqhd', p, v)

    o = lax.map(one_block, jnp.arange(n_blocks))
    return o.transpose(1, 0, 2, 3, 4).reshape(bsz, seq, nh * dv)


def mla_mixer(c_q, c_kv, k_rope, cos, sin, q_norm_g, w_uq, kv_norm_g, w_ukv):
    bsz, seq, _ = c_q.shape
    q = (rmsnorm(c_q, q_norm_g) @ w_uq).reshape(bsz, seq, MLA_HEADS, MLA_NOPE + MLA_ROPE)
    q_nope, q_pe = q[..., :MLA_NOPE], q[..., MLA_NOPE:]
    q_pe = apply_rope(q_pe, cos[:, :, None, :], sin[:, :, None, :])
    kv = (rmsnorm(c_kv, kv_norm_g) @ w_ukv).reshape(bsz, seq, MLA_HEADS, MLA_NOPE + MLA_V)
    k_nope, v = kv[..., :MLA_NOPE], kv[..., MLA_NOPE:]
    k_pe = apply_rope(k_rope, cos, sin)[:, :, None, :]
    k_pe = jnp.broadcast_to(k_pe, (bsz, seq, MLA_HEADS, MLA_ROPE))
    q = jnp.concatenate([q_nope, q_pe], axis=-1)
    k = jnp.concatenate([k_nope, k_pe], axis=-1)
    return causal_block_attention(q, k, v, 1.0 / math.sqrt(MLA_NOPE + MLA_ROPE))


def s5_mla_sublayer(x, c, cos, sin, norm_g, ada_w, ada_b, w_in,
                    lam_re, lam_im, log_step, b_re, b_im, c_re, c_im, d, w_glu, b_glu,
                    q_norm_g, w_uq, kv_norm_g, w_ukv, w_out):
    shift, scale, gate = ada_modulation(c, ada_w, ada_b)
    h = rmsnorm(x, norm_g) * (1.0 + scale) + shift
    proj = h @ w_in
    cut = [S5_WIDTH, S5_WIDTH + MLA_Q_RANK, S5_WIDTH + MLA_Q_RANK + MLA_KV_RANK]
    u, c_q, c_kv, k_rope = jnp.split(proj, cut, axis=-1)
    y_s5 = s5_mixer(u, lam_re, lam_im, log_step, b_re, b_im, c_re, c_im, d, w_glu, b_glu)
    y_mla = mla_mixer(c_q, c_kv, k_rope, cos, sin, q_norm_g, w_uq, kv_norm_g, w_ukv)
    y = jnp.concatenate([y_s5, y_mla], axis=-1) @ w_out
    return x + gate * y


def conv_sublayer(x, c, norm_g, ada_w, ada_b, w_pw1, b_pw1, w_dw, b_dw, ln_g, ln_b, w_pw2, b_pw2):
    shift, scale, gate = ada_modulation(c, ada_w, ada_b)
    h = rmsnorm(x, norm_g) * (1.0 + scale) + shift
    y = jax.nn.glu(h @ w_pw1 + b_pw1, axis=-1)
    y = lax.conv_general_dilated(
        y, w_dw[:, None, :].astype(y.dtype), window_strides=(1,),
        padding=[(CONV_KERNEL - 1, 0)], dimension_numbers=('NWC', 'WIO', 'NWC'),
        feature_group_count=CONV_WIDTH) + b_dw
    y = jax.nn.silu(layernorm(y, ln_g, ln_b))
    return x + gate * (y @ w_pw2 + b_pw2)


def moe_ffn(h, router_w, router_b, w1, b1, w2, b2):
    bsz, seq, dm = h.shape
    xf = h.reshape(-1, dm)
    n_tok = xf.shape[0]
    logits = (xf @ router_w + router_b).astype(jnp.float32)
    top_v, top_i = lax.top_k(logits, TOP_K)
    gates = jax.nn.softmax(top_v, axis=-1).astype(h.dtype)
    n_assign = n_tok * TOP_K
    flat_e = top_i.reshape(-1).astype(jnp.int32)
    flat_tok = jnp.arange(n_assign, dtype=jnp.int32) // TOP_K
    flat_g = gates.reshape(-1)
    order = jnp.argsort(flat_e)
    sorted_e = flat_e[order]
    counts = jnp.bincount(flat_e, length=N_EXPERTS)
    padded = ((counts + EXPERT_BLOCK - 1) // EXPERT_BLOCK) * EXPERT_BLOCK
    pad_end = jnp.cumsum(padded)
    pad_start = pad_end - padded
    start = jnp.cumsum(counts) - counts
    rank = jnp.arange(n_assign, dtype=jnp.int32) - start[sorted_e]
    dest = pad_start[sorted_e] + rank
    n_blocks = (n_assign + EXPERT_BLOCK - 1) // EXPERT_BLOCK + N_EXPERTS
    n_slots = n_blocks * EXPERT_BLOCK
    slot_tok = jnp.zeros((n_slots,), jnp.int32).at[dest].set(flat_tok[order])
    slot_w = jnp.zeros((n_slots,), h.dtype).at[dest].set(flat_g[order])
    block_start = jnp.arange(n_blocks, dtype=jnp.int32) * EXPERT_BLOCK
    block_expert = jnp.clip(jnp.searchsorted(pad_end, block_start, side='right'), 0, N_EXPERTS - 1)

    def expert_block(args):
        tok, e = args
        a = xf[tok] @ w1[e] + b1[e]
        a_glu = jnp.minimum(a[:, ::2], SWIGLU_LIMIT)
        a_lin = jnp.clip(a[:, 1::2], -SWIGLU_LIMIT, SWIGLU_LIMIT)
        act = a_glu * jax.nn.sigmoid(SWIGLU_ALPHA * a_glu) * (a_lin + 1.0)
        return act @ w2[e] + b2[e]

    y = lax.map(expert_block, (slot_tok.reshape(n_blocks, EXPERT_BLOCK), block_expert))
    y = y.reshape(n_slots, dm) * slot_w[:, None]
    out = jnp.zeros_like(xf).at[slot_tok].add(y)
    return out.reshape(bsz, seq, dm)


def moe_sublayer(x, c, norm_g, ada_w, ada_b, router_w, router_b, w1, b1, w2, b2):
    shift, scale, gate = ada_modulation(c, ada_w, ada_b)
    h = rmsnorm(x, norm_g) * (1.0 + scale) + shift
    return x + gate * moe_ffn(h, router_w, router_b, w1, b1, w2, b2)


def setup_inputs(seed: int = 0) -> dict:
    key = jax.random.key(seed)
    it = iter(list(jax.random.split(key, 80)))
    f32 = jnp.float32

    def nrm(shape, scale):
        return jax.random.normal(next(it), shape, f32) * scale

    def gain(n):
        return 1.0 + nrm((n,), 0.02)

    def ada():
        return nrm((D_MODEL, 3 * D_MODEL), 0.5 * D_MODEL ** -0.5), nrm((3 * D_MODEL,), 0.02)

    def moe_params(prefix, p):
        p[prefix + 'moe_norm_g'] = gain(D_MODEL)
        p[prefix + 'moe_ada_w'], p[prefix + 'moe_ada_b'] = ada()
        p[prefix + 'router_w'] = nrm((D_MODEL, N_EXPERTS), D_MODEL ** -0.5)
        p[prefix + 'router_b'] = nrm((N_EXPERTS,), 0.01)
        p[prefix + 'exp_w1'] = nrm((N_EXPERTS, D_MODEL, 2 * EXPERT_FF), D_MODEL ** -0.5)
        p[prefix + 'exp_b1'] = nrm((N_EXPERTS, 2 * EXPERT_FF), 0.01)
        p[prefix + 'exp_w2'] = nrm((N_EXPERTS, EXPERT_FF, D_MODEL), EXPERT_FF ** -0.5)
        p[prefix + 'exp_b2'] = nrm((N_EXPERTS, D_MODEL), 0.01)

    p = {}
    p['x'] = nrm((BATCH, SEQ, D_MODEL), 1.0)
    p['c'] = nrm((BATCH, D_MODEL), 1.0)
    p['positions'] = (jnp.arange(SEQ, dtype=jnp.int32)[None, :]
                      + jax.random.randint(next(it), (BATCH, 1), 0, 4096, dtype=jnp.int32))
    p['l0_mix_norm_g'] = gain(D_MODEL)
    p['l0_mix_ada_w'], p['l0_mix_ada_b'] = ada()
    p['l0_w_in'] = nrm((D_MODEL, IN_COLS), D_MODEL ** -0.5)
    p['l0_s5_lambda_re'] = -0.5 + nrm((S5_GROUPS, S5_STATE), 0.01)
    p['l0_s5_lambda_im'] = jnp.tile(math.pi * jnp.arange(S5_STATE, dtype=f32)[None, :], (S5_GROUPS, 1))
    p['l0_s5_log_step'] = jax.random.uniform(next(it), (S5_GROUPS,), f32, math.log(0.001), math.log(0.1))
    p['l0_s5_b_re'] = nrm((S5_GROUPS, S5_STATE, S5_GROUP_CH), (2.0 * S5_GROUP_CH) ** -0.5)
    p['l0_s5_b_im'] = nrm((S5_GROUPS, S5_STATE, S5_GROUP_CH), (2.0 * S5_GROUP_CH) ** -0.5)
    p['l0_s5_c_re'] = nrm((S5_GROUPS, S5_GROUP_CH, S5_STATE), (2.0 * S5_STATE) ** -0.5)
    p['l0_s5_c_im'] = nrm((S5_GROUPS, S5_GROUP_CH, S5_STATE), (2.0 * S5_STATE) ** -0.5)
    p['l0_s5_d'] = nrm((S5_GROUPS, S5_GROUP_CH), 1.0)
    p['l0_s5_w_glu'] = nrm((S5_WIDTH, S5_WIDTH), S5_WIDTH ** -0.5)
    p['l0_s5_b_glu'] = nrm((S5_WIDTH,), 0.01)
    p['l0_mla_q_norm_g'] = gain(MLA_Q_RANK)
    p['l0_mla_w_uq'] = nrm((MLA_Q_RANK, MLA_HEADS * (MLA_NOPE + MLA_ROPE)), MLA_Q_RANK ** -0.5)
    p['l0_mla_kv_norm_g'] = gain(MLA_KV_RANK)
    p['l0_mla_w_ukv'] = nrm((MLA_KV_RANK, MLA_HEADS * (MLA_NOPE + MLA_V)), MLA_KV_RANK ** -0.5)
    p['l0_w_out'] = nrm((MIX_OUT, D_MODEL), MIX_OUT ** -0.5)
    moe_params('l0_', p)
    p['l1_mix_norm_g'] = gain(D_MODEL)
    p['l1_mix_ada_w'], p['l1_mix_ada_b'] = ada()
    p['l1_conv_w_pw1'] = nrm((D_MODEL, 2 * CONV_WIDTH), D_MODEL ** -0.5)
    p['l1_conv_b_pw1'] = nrm((2 * CONV_WIDTH,), 0.01)
    p['l1_conv_w_dw'] = nrm((CONV_KERNEL, CONV_WIDTH), CONV_KERNEL ** -0.5)
    p['l1_conv_b_dw'] = nrm((CONV_WIDTH,), 0.01)
    p['l1_conv_ln_g'] = gain(CONV_WIDTH)
    p['l1_conv_ln_b'] = nrm((CONV_WIDTH,), 0.01)
    p['l1_conv_w_pw2'] = nrm((CONV_WIDTH, D_MODEL), CONV_WIDTH ** -0.5)
    p['l1_conv_b_pw2'] = nrm((D_MODEL,), 0.01)
    moe_params('l1_', p)
    p['final_norm_g'] = gain(D_MODEL)
    return p


def reference(x, c, positions,
              l0_mix_norm_g, l0_mix_ada_w, l0_mix_ada_b, l0_w_in,
              l0_s5_lambda_re, l0_s5_lambda_im, l0_s5_log_step, l0_s5_b_re, l0_s5_b_im,
              l0_s5_c_re, l0_s5_c_im, l0_s5_d, l0_s5_w_glu, l0_s5_b_glu,
              l0_mla_q_norm_g, l0_mla_w_uq, l0_mla_kv_norm_g, l0_mla_w_ukv, l0_w_out,
              l0_moe_norm_g, l0_moe_ada_w, l0_moe_ada_b, l0_router_w, l0_router_b,
              l0_exp_w1, l0_exp_b1, l0_exp_w2, l0_exp_b2,
              l1_mix_norm_g, l1_mix_ada_w, l1_mix_ada_b, l1_conv_w_pw1, l1_conv_b_pw1,
              l1_conv_w_dw, l1_conv_b_dw, l1_conv_ln_g, l1_conv_ln_b, l1_conv_w_pw2, l1_conv_b_pw2,
              l1_moe_norm_g, l1_moe_ada_w, l1_moe_ada_b, l1_router_w, l1_router_b,
              l1_exp_w1, l1_exp_b1, l1_exp_w2, l1_exp_b2,
              final_norm_g):
    cos, sin = rope_tables(positions)
    mixer_params = (
        (l0_mix_norm_g, l0_mix_ada_w, l0_mix_ada_b, l0_w_in,
         l0_s5_lambda_re, l0_s5_lambda_im, l0_s5_log_step, l0_s5_b_re, l0_s5_b_im,
         l0_s5_c_re, l0_s5_c_im, l0_s5_d, l0_s5_w_glu, l0_s5_b_glu,
         l0_mla_q_norm_g, l0_mla_w_uq, l0_mla_kv_norm_g, l0_mla_w_ukv, l0_w_out),
        (l1_mix_norm_g, l1_mix_ada_w, l1_mix_ada_b, l1_conv_w_pw1, l1_conv_b_pw1,
         l1_conv_w_dw, l1_conv_b_dw, l1_conv_ln_g, l1_conv_ln_b, l1_conv_w_pw2, l1_conv_b_pw2),
    )
    moe_params = (
        (l0_moe_norm_g, l0_moe_ada_w, l0_moe_ada_b, l0_router_w, l0_router_b,
         l0_exp_w1, l0_exp_b1, l0_exp_w2, l0_exp_b2),
        (l1_moe_norm_g, l1_moe_ada_w, l1_moe_ada_b, l1_router_w, l1_router_b,
         l1_exp_w1, l1_exp_b1, l1_exp_w2, l1_exp_b2),
    )
    for layer in range(DEPTH):
        if layer % 2 == 0:
            x = s5_mla_sublayer(x, c, cos, sin, *mixer_params[layer])
        else:
            x = conv_sublayer(x, c, *mixer_params[layer])
        x = moe_sublayer(x, c, *moe_params[layer])
    return rmsnorm(x, final_norm_g)
```

```python
import functools
import math

import numpy as np
import jax
import jax.numpy as jnp
from jax import lax
from jax.experimental import pallas as pl
from jax.experimental.pallas import tpu as pltpu

F32 = jnp.float32
BF16 = jnp.bfloat16

D_MODEL = 2048
S5_WIDTH = 1024
S5_GROUP_CH = 16
S5_GROUPS = 64
S5_STATE = 64
MLA_HEADS = 8
MLA_NOPE = 128
MLA_ROPE = 64
MLA_V = 128
MLA_Q_RANK = 512
MLA_KV_RANK = 512
ROPE_THETA = 10000.0
CONV_KERNEL = 31
N_EXPERTS = 32
TOP_K = 4
EXPERT_FF = 2048
SWIGLU_ALPHA = 1.702
SWIGLU_LIMIT = 7.0
EPS = 1e-6
LN_EPS = 1e-5
NEG_BIG = -1e30

LANES = 128
SUBLANES = 8
VMEM_LIMIT_BYTES = 56 * 1024 * 1024

S5_CHUNK = 16
S5_ROW = S5_CHUNK * S5_GROUP_CH
PROJ_TM = 512
MLA_TM = 256
ATT_T = 512
OUT_TM = 256
ROUTE_TM = 512
CONV_TM = 256
CONV_HALO = 32
FFN_TM = 256
FFN_R = 1024
FFN_FC = 256
GATHER_T = 256
COMBINE_T = 128


def _cparams(sem):
    return pltpu.CompilerParams(dimension_semantics=sem, vmem_limit_bytes=VMEM_LIMIT_BYTES)


def _norm_mod(x, g, scale, shift):
    ms = jnp.mean(x * x, axis=-1, keepdims=True)
    return (x * lax.rsqrt(ms + EPS)) * g * (1.0 + scale) + shift


def _rms(x, g):
    ms = jnp.mean(x * x, axis=-1, keepdims=True)
    return (x * lax.rsqrt(ms + EPS)) * g


def _dot(a, b):
    return jnp.dot(a, b, preferred_element_type=F32)


def _dot_nt(a, b):
    return lax.dot_general(a, b, (((1,), (1,)), ((), ())), preferred_element_type=F32)


SLAB = D_MODEL // LANES


def _store_slabs(ref, row0, val, lead=()):
    n = val.shape[0]
    for s in range(SLAB):
        ref[lead + (pl.ds(row0 * SLAB + s, n, stride=SLAB), slice(None))] = val[:, s * LANES:(s + 1) * LANES]


def _load_slabs(ref, row0, n, lead=()):
    parts = [ref[lead + (pl.ds(row0 * SLAB + s, n, stride=SLAB), slice(None))] for s in range(SLAB)]
    return jnp.concatenate(parts, axis=1)


ADA_TN = 768
ADA_KC = 256


def _ada_kernel(ct_ref, w_ref, b_ref, o_ref):
    nb = ct_ref.shape[0]
    kdim = w_ref.shape[0]
    rows = []
    for b in range(nb):
        acc = jnp.zeros((1, w_ref.shape[1]), F32)
        for k0 in range(0, kdim, ADA_KC):
            c = ct_ref[b, k0:k0 + ADA_KC, :]
            cs = c * jax.nn.sigmoid(c)
            acc = acc + jnp.sum(w_ref[k0:k0 + ADA_KC, :] * cs, axis=0, keepdims=True)
        rows.append(acc)
    o_ref[...] = jnp.concatenate(rows, axis=0) + b_ref[...]


def _ada_mod(c, w, b):
    nb, d = c.shape
    n = w.shape[1]
    ct = c.reshape(nb, d, 1)
    m = pl.pallas_call(
        _ada_kernel,
        grid=(n // ADA_TN,),
        in_specs=[pl.BlockSpec((nb, d, 1), lambda j: (0, 0, 0)),
                  pl.BlockSpec((d, ADA_TN), lambda j: (0, j)),
                  pl.BlockSpec((1, ADA_TN), lambda j: (0, j))],
        out_specs=pl.BlockSpec((nb, ADA_TN), lambda j: (0, j)),
        out_shape=jax.ShapeDtypeStruct((nb, n), F32),
        compiler_params=_cparams(("arbitrary",)),
        name="ada_mod",
    )(ct, w, b.reshape(1, n))
    return m.reshape(nb, 1, n)


def _mod_specs(tiles_per_batch, which):
    return pl.BlockSpec((1, 1, D_MODEL), lambda i, *_: (i // tiles_per_batch, 0, which))


def _proj_in_kernel(x_ref, g_ref, shift_ref, scale_ref, w_ref, u_ref, cq_ref, ckv_ref, kr_ref):
    h = _norm_mod(x_ref[...], g_ref[...], scale_ref[0], shift_ref[0]).astype(BF16)
    acc = _dot(h, w_ref[...])
    c0, c1, c2 = S5_WIDTH, S5_WIDTH + MLA_Q_RANK, S5_WIDTH + MLA_Q_RANK + MLA_KV_RANK
    u_ref[...] = acc[:, :c0].astype(BF16)
    cq_ref[...] = acc[:, c0:c1]
    ckv_ref[...] = acc[:, c1:c2]
    kr_ref[...] = acc[:, c2:]


def _proj_in(x2, g, mod, w_ext, seq):
    t = x2.shape[0]
    tm = min(PROJ_TM, seq)
    tpb = seq // tm
    n = w_ext.shape[1]
    row = lambda w: pl.BlockSpec((tm, w), lambda i: (i, 0))
    return pl.pallas_call(
        _proj_in_kernel,
        grid=(t // tm,),
        in_specs=[row(D_MODEL),
                  pl.BlockSpec((1, D_MODEL), lambda i: (0, 0)),
                  _mod_specs(tpb, 0), _mod_specs(tpb, 1),
                  pl.BlockSpec((D_MODEL, n), lambda i: (0, 0))],
        out_specs=[row(S5_WIDTH), row(MLA_Q_RANK), row(MLA_KV_RANK), row(2 * MLA_ROPE)],
        out_shape=[jax.ShapeDtypeStruct((t, S5_WIDTH), BF16),
                   jax.ShapeDtypeStruct((t, MLA_Q_RANK), F32),
                   jax.ShapeDtypeStruct((t, MLA_KV_RANK), F32),
                   jax.ShapeDtypeStruct((t, 2 * MLA_ROPE), F32)],
        compiler_params=_cparams(("arbitrary",)),
        name="proj_in",
    )(x2, g.reshape(1, -1), mod, mod, w_ext)


def _s5_prep_kernel(lamc_ref, lamr_ref, step_ref, bt_ref, btt_ref, ct_ref, d_ref,
                    kt_ref, wt_ref, v_ref, a16_ref):
    P, H, C = S5_STATE, S5_GROUP_CH, S5_CHUNK
    step = step_ref[0]
    step = jnp.exp(step)
    lr_c = jnp.minimum(lamc_ref[0, 0], -1e-4)
    li_c = lamc_ref[0, 1]
    lr_r = jnp.minimum(lamr_ref[0, 0:1, :], -1e-4)
    li_r = lamr_ref[0, 1:2, :]

    def ratio(lr, li):
        mag = jnp.exp(lr * step)
        ab_re = mag * jnp.cos(li * step)
        ab_im = mag * jnp.sin(li * step)
        denom = lr * lr + li * li
        nr, ni = ab_re - 1.0, ab_im
        return (nr * lr + ni * li) / denom, (ni * lr - nr * li) / denom

    rr_c, ri_c = ratio(lr_c, li_c)
    rr_r, ri_r = ratio(lr_r, li_r)

    lane = lax.broadcasted_iota(jnp.int32, (1, S5_ROW), 1)
    kk = (lane // H).astype(F32)

    def powers(k):
        mag = jnp.exp(lr_c * step * k)
        return mag * jnp.cos(li_c * step * k), mag * jnp.sin(li_c * step * k)

    bre_t, bim_t = bt_ref[0, 0], bt_ref[0, 1]
    bbt_re = rr_c * bre_t - ri_c * bim_t
    bbt_im = rr_c * bim_t + ri_c * bre_t
    cre_t, cim_t = ct_ref[0, 0], ct_ref[0, 1]

    er, ei = powers(float(C - 1) - kk)
    wt_ref[0, 0:P, :] = (er * bbt_re - ei * bbt_im).astype(BF16)
    wt_ref[0, P:2 * P, :] = (er * bbt_im + ei * bbt_re).astype(BF16)

    er, ei = powers(kk + 1.0)
    v_ref[0, 0:P, :] = (cre_t * er - cim_t * ei).astype(BF16)
    v_ref[0, P:2 * P, :] = (-cre_t * ei - cim_t * er).astype(BF16)

    er, ei = powers(kk)
    q_re = er * cre_t - ei * cim_t
    q_im = er * cim_t + ei * cre_t
    brt, bit = btt_ref[0, 0], btt_ref[0, 1]
    bbr = rr_r * brt - ri_r * bit
    bbi = rr_r * bit + ri_r * brt
    hi = lax.Precision.HIGHEST
    mall = (jnp.dot(bbr, q_re, precision=hi, preferred_element_type=F32)
            - jnp.dot(bbi, q_im, precision=hi, preferred_element_type=F32))
    rowh = lax.broadcasted_iota(jnp.int32, (H, S5_ROW), 0)
    laneh = lax.broadcasted_iota(jnp.int32, (H, S5_ROW), 1)
    mall = mall + jnp.where(laneh == rowh, d_ref[0], 0.0)
    for s in range(C):
        piece = mall if s == 0 else pltpu.roll(mall, H * s, 1)
        piece = jnp.where(laneh >= H * s, piece, 0.0)
        kt_ref[0, s * H:(s + 1) * H, :] = piece.astype(BF16)

    mag = jnp.exp(lr_r * step * float(C))
    a16_ref[0, 0:1, :] = mag * jnp.cos(li_r * step * float(C))
    a16_ref[0, 1:2, :] = mag * jnp.sin(li_r * step * float(C))


def _s5_prep(lam_re, lam_im, log_step, b_re, b_im, c_re, c_im, d):
    G, P, H, C = S5_GROUPS, S5_STATE, S5_GROUP_CH, S5_CHUNK
    lam = jnp.stack([lam_re, lam_im], axis=1).astype(F32)
    lamc = lam.reshape(G, 2, P, 1)
    b = jnp.stack([b_re, b_im], axis=1).astype(F32)
    bt = jnp.tile(b, (1, 1, 1, C))
    btt = jnp.swapaxes(b, 2, 3)
    c = jnp.stack([c_re, c_im], axis=1).astype(F32)
    ct = jnp.tile(jnp.swapaxes(c, 2, 3), (1, 1, 1, C))
    dt = jnp.tile(d.astype(F32), (1, C)).reshape(G, 1, S5_ROW)
    step = log_step.astype(F32).reshape(G, 1, 1)
    g4 = lambda *shape: pl.BlockSpec((1,) + shape, lambda g: (g,) + (0,) * len(shape))
    return pl.pallas_call(
        _s5_prep_kernel,
        grid=(G,),
        in_specs=[g4(2, P, 1), g4(2, P), g4(1, 1), g4(2, P, S5_ROW), g4(2, H, P),
                  g4(2, P, S5_ROW), g4(1, S5_ROW)],
        out_specs=[g4(S5_ROW, S5_ROW), g4(2 * P, S5_ROW), g4(2 * P, S5_ROW), g4(2, P)],
        out_shape=[jax.ShapeDtypeStruct((G, S5_ROW, S5_ROW), BF16),
                   jax.ShapeDtypeStruct((G, 2 * P, S5_ROW), BF16),
                   jax.ShapeDtypeStruct((G, 2 * P, S5_ROW), BF16),
                   jax.ShapeDtypeStruct((G, 2, P), F32)],
        compiler_params=_cparams(("arbitrary",)),
        name="s5_prep",
    )(lamc, lam, step, bt, btt, ct, dt)


def _s5_main_kernel(u_ref, kt_ref, wt_ref, v_ref, a16_ref, y_ref, sr_ref, si_ref, xr_ref, xi_ref,
                    *, nbatch, nchunk):
    P = S5_STATE
    u = u_ref[0]
    sr_ref[...] = _dot_nt(u, wt_ref[0, 0:P, :])
    si_ref[...] = _dot_nt(u, wt_ref[0, P:2 * P, :])
    ar = a16_ref[0, 0:1, :]
    ai = a16_ref[0, 1:2, :]

    def step(c, carry):
        out = []
        for b in range(nbatch):
            xr, xi = carry[2 * b], carry[2 * b + 1]
            r = b * nchunk + c
            xr_ref[pl.ds(r, 1), :] = xr
            xi_ref[pl.ds(r, 1), :] = xi
            nxr = ar * xr - ai * xi + sr_ref[pl.ds(r, 1), :]
            nxi = ar * xi + ai * xr + si_ref[pl.ds(r, 1), :]
            out += [nxr, nxi]
        return tuple(out)

    zero = jnp.zeros((1, P), F32)
    lax.fori_loop(0, nchunk, step, (zero,) * (2 * nbatch))
    y = _dot(u, kt_ref[0])
    y = y + _dot(xr_ref[...].astype(BF16), v_ref[0, 0:P, :])
    y = y + _dot(xi_ref[...].astype(BF16), v_ref[0, P:2 * P, :])
    y_ref[0] = y


def _s5_main(u_r, kt, wt, v, a16, nbatch, nchunk):
    G, rows, _ = u_r.shape
    P = S5_STATE
    g3 = lambda a, b: pl.BlockSpec((1, a, b), lambda g: (g, 0, 0))
    return pl.pallas_call(
        functools.partial(_s5_main_kernel, nbatch=nbatch, nchunk=nchunk),
        grid=(G,),
        in_specs=[g3(rows, S5_ROW), g3(S5_ROW, S5_ROW), g3(2 * P, S5_ROW), g3(2 * P, S5_ROW), g3(2, P)],
        out_specs=g3(rows, S5_ROW),
        out_shape=jax.ShapeDtypeStruct((G, rows, S5_ROW), F32),
        scratch_shapes=[pltpu.VMEM((rows, P), F32) for _ in range(4)],
        compiler_params=_cparams(("arbitrary",)),
        name="s5_main",
    )(u_r, kt, wt, v, a16)


def _mla_proj_kernel(cq_ref, ckv_ref, kr_ref, pos_ref, qg_ref, kvg_ref, wq_ref, wkv_ref,
                     invf_ref, sgn_ref, q_ref, k_ref, v_ref):
    qscale = 1.0 / math.sqrt(MLA_NOPE + MLA_ROPE)
    qa = _dot(_rms(cq_ref[...], qg_ref[...]).astype(BF16), wq_ref[...])
    kva = _dot(_rms(ckv_ref[...], kvg_ref[...]).astype(BF16), wkv_ref[...])
    ang = pos_ref[...].astype(F32) * invf_ref[...]
    cc = jnp.cos(ang)
    ss = jnp.sin(ang) * sgn_ref[...]

    def rope(slab):
        return slab * cc + pltpu.roll(slab, MLA_ROPE, 1) * ss

    kpe = rope(kr_ref[...])[:, :MLA_ROPE].astype(BF16)
    hw = MLA_NOPE + 2 * MLA_ROPE
    for h in range(MLA_HEADS):
        blk = qa[:, h * hw:(h + 1) * hw]
        q_ref[0, h, :, 0:MLA_NOPE] = (blk[:, :MLA_NOPE] * qscale).astype(BF16)
        qpe = rope(blk[:, MLA_NOPE:]) * qscale
        q_ref[0, h, :, MLA_NOPE:MLA_NOPE + MLA_ROPE] = qpe[:, :MLA_ROPE].astype(BF16)
        kvb = kva[:, h * hw:(h + 1) * hw]
        k_ref[0, h, :, 0:MLA_NOPE] = kvb[:, :MLA_NOPE].astype(BF16)
        k_ref[0, h, :, MLA_NOPE:MLA_NOPE + MLA_ROPE] = kpe
        v_ref[0, h] = kvb[:, MLA_NOPE:].astype(BF16)


def _mla_proj(cq, ckv, kr, pos, qg, kvg, wq_ext, wkv, nbatch, seq):
    tm = min(MLA_TM, seq)
    nl = seq // tm
    dqk = MLA_NOPE + MLA_ROPE
    half = MLA_ROPE // 2
    inv_freq = 1.0 / (ROPE_THETA ** (jnp.arange(0, MLA_ROPE, 2, dtype=F32) / MLA_ROPE))
    invf = jnp.tile(inv_freq, 4).reshape(1, 2 * MLA_ROPE)
    sgn = jnp.tile(jnp.concatenate([-jnp.ones((half,), F32), jnp.ones((half,), F32)]), 2).reshape(1, 2 * MLA_ROPE)
    row = lambda w: pl.BlockSpec((tm, w), lambda b, i: (b * nl + i, 0))
    full = lambda a, b_: pl.BlockSpec((a, b_), lambda b, i: (0, 0))
    head = lambda w: pl.BlockSpec((1, MLA_HEADS, tm, w), lambda b, i: (b, 0, i, 0))
    return pl.pallas_call(
        _mla_proj_kernel,
        grid=(nbatch, nl),
        in_specs=[row(MLA_Q_RANK), row(MLA_KV_RANK), row(2 * MLA_ROPE), row(1),
                  full(1, MLA_Q_RANK), full(1, MLA_KV_RANK),
                  full(MLA_Q_RANK, wq_ext.shape[1]), full(MLA_KV_RANK, wkv.shape[1]),
                  full(1, 2 * MLA_ROPE), full(1, 2 * MLA_ROPE)],
        out_specs=[head(dqk), head(dqk), head(MLA_V)],
        out_shape=[jax.ShapeDtypeStruct((nbatch, MLA_HEADS, seq, dqk), BF16),
                   jax.ShapeDtypeStruct((nbatch, MLA_HEADS, seq, dqk), BF16),
                   jax.ShapeDtypeStruct((nbatch, MLA_HEADS, seq, MLA_V), BF16)],
        compiler_params=_cparams(("arbitrary", "arbitrary")),
        name="mla_proj",
    )(cq, ckv, kr, pos, qg.reshape(1, -1), kvg.reshape(1, -1), wq_ext, wkv, invf, sgn)


def _flash_kernel(qi_ref, kj_ref, q_ref, k_ref, v_ref, o_ref, m_ref, l_ref, acc_ref, *, tq):
    p_id = pl.program_id(2)
    qi = qi_ref[p_id]
    kj = kj_ref[p_id]

    @pl.when(kj == 0)
    def _():
        m_ref[...] = jnp.full(m_ref.shape, NEG_BIG, F32)
        l_ref[...] = jnp.zeros(l_ref.shape, F32)
        acc_ref[...] = jnp.zeros(acc_ref.shape, F32)

    s = _dot_nt(q_ref[0, 0], k_ref[0, 0])
    row = qi * tq + lax.broadcasted_iota(jnp.int32, s.shape, 0)
    col = kj * tq + lax.broadcasted_iota(jnp.int32, s.shape, 1)
    s = jnp.where(col <= row, s, NEG_BIG)
    m_old = m_ref[...]
    m_new = jnp.maximum(m_old, jnp.max(s, axis=-1, keepdims=True))
    alpha = jnp.exp(m_old - m_new)
    p = jnp.exp(s - m_new)
    l_ref[...] = alpha * l_ref[...] + jnp.sum(p, axis=-1, keepdims=True)
    acc_ref[...] = alpha * acc_ref[...] + _dot(p.astype(BF16), v_ref[0, 0])
    m_ref[...] = m_new

    @pl.when(kj == qi)
    def _():
        o_ref[0] = (acc_ref[...] / l_ref[...]).astype(o_ref.dtype)


def _flash_attention(q, k, v):
    nbatch, nh, seq, dqk = q.shape
    tq = min(ATT_T, seq)
    nq = seq // tq
    pairs = [(i, j) for i in range(nq) for j in range(i + 1)]
    qi_tab = jnp.asarray(np.array([p[0] for p in pairs], np.int32))
    kj_tab = jnp.asarray(np.array([p[1] for p in pairs], np.int32))
    grid_spec = pltpu.PrefetchScalarGridSpec(
        num_scalar_prefetch=2,
        grid=(nbatch, nh, len(pairs)),
        in_specs=[pl.BlockSpec((1, 1, tq, dqk), lambda b, h, p, qi, kj: (b, h, qi[p], 0)),
                  pl.BlockSpec((1, 1, tq, dqk), lambda b, h, p, qi, kj: (b, h, kj[p], 0)),
                  pl.BlockSpec((1, 1, tq, MLA_V), lambda b, h, p, qi, kj: (b, h, kj[p], 0))],
        out_specs=pl.BlockSpec((1, tq, MLA_V), lambda b, h, p, qi, kj: (b, qi[p], h)),
        scratch_shapes=[pltpu.VMEM((tq, 1), F32), pltpu.VMEM((tq, 1), F32), pltpu.VMEM((tq, MLA_V), F32)],
    )
    return pl.pallas_call(
        functools.partial(_flash_kernel, tq=tq),
        grid_spec=grid_spec,
        out_shape=jax.ShapeDtypeStruct((nbatch, seq, nh * MLA_V), BF16),
        compiler_params=_cparams(("arbitrary", "arbitrary", "arbitrary")),
        name="flash_attn",
    )(qi_tab, kj_tab, q, k, v)


def _gelu_tanh(x):
    c = math.sqrt(2.0 / math.pi)
    return 0.5 * x * (1.0 + jnp.tanh(c * (x + 0.044715 * (x * x * x))))


def _mixer_out_kernel(ys_ref, ym_ref, x_ref, gate_ref, wglu_ref, bglu_ref, wo_ref, o_ref):
    y = _gelu_tanh(ys_ref[...])
    g = _dot(y.astype(BF16), wglu_ref[...]) + bglu_ref[...]
    s5o = (y * jax.nn.sigmoid(g)).astype(BF16)
    acc = _dot(s5o, wo_ref[0:S5_WIDTH, :]) + _dot(ym_ref[...], wo_ref[S5_WIDTH:, :])
    o_ref[...] = x_ref[...] + gate_ref[0] * acc


def _mixer_out(ys, ym, x2, mod, w_glu, b_glu, w_out, seq):
    t = x2.shape[0]
    tm = min(OUT_TM, seq)
    tpb = seq // tm
    row = lambda w: pl.BlockSpec((tm, w), lambda i: (i, 0))
    full = lambda a, b: pl.BlockSpec((a, b), lambda i: (0, 0))
    return pl.pallas_call(
        _mixer_out_kernel,
        grid=(t // tm,),
        in_specs=[row(S5_WIDTH), row(MLA_HEADS * MLA_V), row(D_MODEL), _mod_specs(tpb, 2),
                  full(S5_WIDTH, S5_WIDTH), full(1, S5_WIDTH), full(2 * S5_WIDTH, D_MODEL)],
        out_specs=row(D_MODEL),
        out_shape=jax.ShapeDtypeStruct((t, D_MODEL), F32),
        compiler_params=_cparams(("arbitrary",)),
        name="mixer_out",
    )(ys, ym, x2, mod, w_glu, b_glu.reshape(1, -1), w_out)


def _route_kernel(x_ref, g_ref, shift_ref, scale_ref, rw_ref, rb_ref, tri_ref,
                  h_ref, route_ref, gates_ref, cnt_ref, carry_ref):
    i = pl.program_id(0)

    @pl.when(i == 0)
    def _():
        carry_ref[...] = jnp.zeros(carry_ref.shape, F32)

    h = _norm_mod(x_ref[...], g_ref[...], scale_ref[0], shift_ref[0])
    _store_slabs(h_ref, 0, h)
    logits = jnp.dot(h, rw_ref[...], precision=lax.Precision.HIGHEST,
                     preferred_element_type=F32) + rb_ref[...]
    lane = lax.broadcasted_iota(jnp.int32, logits.shape, 1)
    lanef = lane.astype(F32)
    lg = jnp.where(lane < N_EXPERTS, logits, -jnp.inf)
    vals, hots, idxs = [], [], []
    sel = jnp.zeros(logits.shape, F32)
    for _ in range(TOP_K):
        m = jnp.max(lg, axis=-1, keepdims=True)
        idx = jnp.min(jnp.where(lg == m, lanef, float(LANES)), axis=-1, keepdims=True)
        hot = lanef == idx
        vals.append(m)
        idxs.append(idx)
        hots.append(hot)
        sel = jnp.where(hot, 1.0, sel)
        lg = jnp.where(hot, -jnp.inf, lg)
    es = [jnp.exp(v - vals[0]) for v in vals]
    denom = es[0] + es[1] + es[2] + es[3]
    before = _dot(tri_ref[...], sel.astype(BF16)) + carry_ref[...]
    route = jnp.zeros(logits.shape, F32)
    gates = jnp.zeros(logits.shape, F32)
    for k in range(TOP_K):
        rank = jnp.sum(jnp.where(hots[k], before, 0.0), axis=-1, keepdims=True)
        route = jnp.where(lane == k, idxs[k], route)
        route = jnp.where(lane == TOP_K + k, rank, route)
        gates = jnp.where(lane == k, es[k] / denom, gates)
    route_ref[...] = route.astype(jnp.int32)
    gates_ref[...] = gates
    carry_ref[...] = carry_ref[...] + jnp.sum(sel, axis=0, keepdims=True)
    cnt_ref[...] = carry_ref[...]


def _route(x2, g, mod, router_w, router_b, seq):
    t = x2.shape[0]
    tm = min(ROUTE_TM, seq)
    tpb = seq // tm
    rw = jnp.zeros((D_MODEL, LANES), F32).at[:, :N_EXPERTS].set(router_w)
    rb = jnp.zeros((1, LANES), F32).at[0, :N_EXPERTS].set(router_b)
    tri = jnp.asarray(np.tril(np.ones((tm, tm), np.float32), -1), BF16)
    row = lambda w: pl.BlockSpec((tm, w), lambda i: (i, 0))
    full = lambda a, b: pl.BlockSpec((a, b), lambda i: (0, 0))
    return pl.pallas_call(
        _route_kernel,
        grid=(t // tm,),
        in_specs=[row(D_MODEL), full(1, D_MODEL), _mod_specs(tpb, 0), _mod_specs(tpb, 1),
                  full(D_MODEL, LANES), full(1, LANES), full(tm, tm)],
        out_specs=[pl.BlockSpec((tm * SLAB, LANES), lambda i: (i, 0)), row(LANES), row(LANES), full(1, LANES)],
        out_shape=[jax.ShapeDtypeStruct((t * SLAB, LANES), F32),
                   jax.ShapeDtypeStruct((t, LANES), jnp.int32),
                   jax.ShapeDtypeStruct((t, LANES), F32),
                   jax.ShapeDtypeStruct((1, LANES), F32)],
        scratch_shapes=[pltpu.VMEM((1, LANES), F32)],
        compiler_params=_cparams(("arbitrary",)),
        name="moe_route",
    )(x2, g.reshape(1, -1), mod, mod, rw, rb, tri)


def _slab_rows(r, n=1):
    return pl.ds(pl.multiple_of(r * SLAB, SLAB), n * SLAB)


def _gather_kernel(idx_ref, h_hbm, o_ref, buf_ref, sem):
    n = buf_ref.shape[0] // SLAB

    def row_copy(r):
        tok = idx_ref[0, 0, r]
        return pltpu.make_async_copy(h_hbm.at[_slab_rows(tok), :], buf_ref.at[_slab_rows(r), :], sem)

    def start(r, c):
        row_copy(r).start()
        return c

    def wait(r, c):
        row_copy(r).wait()
        return c

    lax.fori_loop(0, n, start, 0)
    lax.fori_loop(0, n, wait, 0)
    o_ref[...] = buf_ref[...]


def _gather_rows(h_slabs, src_tok):
    ns = src_tok.shape[0]
    nt = ns // GATHER_T
    return pl.pallas_call(
        _gather_kernel,
        grid=(nt,),
        in_specs=[pl.BlockSpec((1, 1, GATHER_T), lambda i: (i, 0, 0), memory_space=pltpu.SMEM),
                  pl.BlockSpec(memory_space=pl.ANY)],
        out_specs=pl.BlockSpec((GATHER_T * SLAB, LANES), lambda i: (i, 0)),
        out_shape=jax.ShapeDtypeStruct((ns * SLAB, LANES), h_slabs.dtype),
        scratch_shapes=[pltpu.VMEM((GATHER_T * SLAB, LANES), h_slabs.dtype), pltpu.SemaphoreType.DMA(())],
        compiler_params=_cparams(("arbitrary",)),
        name="moe_gather",
    )(src_tok.reshape(nt, 1, GATHER_T), h_slabs)


def _ffn_kernel(we_ref, row0_ref, nt_ref, nv_ref, xs_hbm, w1_ref, b1_ref, w2_ref, b2_ref, perm_ref,
                ys_hbm, xin_ref, x_ref, acc_ref, w1p_ref, w2b_ref, stage_ref, sem_in, sem_out):
    w = pl.program_id(0)
    f = pl.program_id(1)
    nf = pl.num_programs(1)
    nt = nt_ref[w]
    row0 = row0_ref[w]
    tm, fc = FFN_TM, FFN_FC

    @pl.when(jnp.logical_and(f == 0, nt > 0))
    def _():
        def tile_copy(t):
            return pltpu.make_async_copy(xs_hbm.at[_slab_rows(row0 + t * tm, tm), :],
                                         xin_ref.at[_slab_rows(t * tm, tm), :], sem_in)

        def start(t, c):
            tile_copy(t).start()
            return c

        def wait(t, c):
            tile_copy(t).wait()
            return c

        def unpack(t, c):
            rows = pl.ds(pl.multiple_of(t * tm, tm), tm)
            x_ref[rows, :] = _load_slabs(xin_ref, t * tm, tm).astype(BF16)
            return c

        lax.fori_loop(0, nt, start, 0)
        lax.fori_loop(0, nt, wait, 0)
        lax.fori_loop(0, nt, unpack, 0)

    @pl.when(nt > 0)
    def _():
        w1p_ref[...] = _dot(w1_ref[0].astype(BF16), perm_ref[...]).astype(BF16)
        w2b_ref[...] = w2_ref[0].astype(BF16)

        def tile(t, c):
            rows = pl.ds(pl.multiple_of(t * tm, tm), tm)
            a = _dot(x_ref[rows, :], w1p_ref[...]) + b1_ref[0]
            glu = jnp.minimum(a[:, :fc], SWIGLU_LIMIT)
            lin = jnp.clip(a[:, fc:], -SWIGLU_LIMIT, SWIGLU_LIMIT)
            act = glu * jax.nn.sigmoid(SWIGLU_ALPHA * glu) * (lin + 1.0)
            contrib = _dot(act.astype(BF16), w2b_ref[...])

            @pl.when(f == 0)
            def _():
                acc_ref[rows, :] = contrib + b2_ref[0]

            @pl.when(f > 0)
            def _():
                acc_ref[rows, :] = acc_ref[rows, :] + contrib

            @pl.when(f == nf - 1)
            def _():
                _store_slabs(stage_ref, 0, acc_ref[rows, :])
                cp = pltpu.make_async_copy(stage_ref, ys_hbm.at[_slab_rows(row0 + t * tm, tm), :], sem_out)
                cp.start()
                cp.wait()

            return c

        lax.fori_loop(0, nt, tile, 0)

    @pl.when(jnp.logical_and(w == pl.num_programs(0) - 1, f == nf - 1))
    def _():
        used = nv_ref[1]
        ntail = (ys_hbm.shape[0] // SLAB - used) // tm
        stage_ref[...] = jnp.zeros(stage_ref.shape, F32)

        def fill(t, c):
            cp = pltpu.make_async_copy(stage_ref, ys_hbm.at[_slab_rows(used + t * tm, tm), :], sem_out)
            cp.start()
            cp.wait()
            return c

        lax.fori_loop(0, ntail, fill, 0)


def _ffn(xs, w1, b1p, w2, b2, perm, we, row0, ntile, nvalid):
    d = D_MODEL
    nf = EXPERT_FF // FFN_FC
    wmax = we.shape[0]

    def wmap(axis):
        def index(w, f, we_r, row0_r, nt_r, nv_r):
            fe = jnp.where(w < nv_r[0], f, nf - 1)
            return (we_r[w], 0, fe) if axis == 2 else (we_r[w], fe, 0)
        return index

    grid_spec = pltpu.PrefetchScalarGridSpec(
        num_scalar_prefetch=4,
        grid=(wmax, nf),
        in_specs=[pl.BlockSpec(memory_space=pl.ANY),
                  pl.BlockSpec((1, d, 2 * FFN_FC), wmap(2)),
                  pl.BlockSpec((1, 1, 2 * FFN_FC), wmap(2)),
                  pl.BlockSpec((1, FFN_FC, d), wmap(1)),
                  pl.BlockSpec((1, 1, d), lambda w, f, we_r, *_: (we_r[w], 0, 0)),
                  pl.BlockSpec((2 * FFN_FC, 2 * FFN_FC), lambda w, f, *_: (0, 0))],
        out_specs=pl.BlockSpec(memory_space=pl.ANY),
        scratch_shapes=[pltpu.VMEM((FFN_R * SLAB, LANES), xs.dtype),
                        pltpu.VMEM((FFN_R, d), BF16),
                        pltpu.VMEM((FFN_R, d), F32),
                        pltpu.VMEM((d, 2 * FFN_FC), BF16),
                        pltpu.VMEM((FFN_FC, d), BF16),
                        pltpu.VMEM((FFN_TM * SLAB, LANES), F32),
                        pltpu.SemaphoreType.DMA(()),
                        pltpu.SemaphoreType.DMA(())],
    )
    return pl.pallas_call(
        _ffn_kernel,
        grid_spec=grid_spec,
        out_shape=jax.ShapeDtypeStruct(xs.shape, F32),
        compiler_params=_cparams(("arbitrary", "arbitrary")),
        name="moe_ffn",
    )(we, row0, ntile, nvalid, xs, w1, b1p, w2, b2, perm)


def _combine_kernel(dest_ref, ys_hbm, gates_ref, x_ref, gmod_ref, fg_ref, o_ref, buf_ref, sem, *, final):
    n = x_ref.shape[0]

    def row_copy(j):
        k = j // n
        r = j - k * n
        slot = dest_ref[0, 0, r * TOP_K + k]
        return pltpu.make_async_copy(ys_hbm.at[_slab_rows(slot), :], buf_ref.at[k, _slab_rows(r), :], sem)

    def start(j, c):
        row_copy(j).start()
        return c

    def wait(j, c):
        row_copy(j).wait()
        return c

    lax.fori_loop(0, n * TOP_K, start, 0)
    lax.fori_loop(0, n * TOP_K, wait, 0)
    gates = gates_ref[...]
    acc = gates[:, 0:1] * _load_slabs(buf_ref, 0, n, lead=(0,))
    for k in range(1, TOP_K):
        acc = acc + gates[:, k:k + 1] * _load_slabs(buf_ref, 0, n, lead=(k,))
    out = x_ref[...] + gmod_ref[0] * acc
    if final:
        out = _rms(out, fg_ref[...])
    o_ref[...] = out


def _combine(ys, dest, gates, x2, mod, final_g, seq, final):
    t, d = x2.shape
    tm = min(COMBINE_T, seq)
    tpb = seq // tm
    nt = t // tm
    row = lambda w: pl.BlockSpec((tm, w), lambda i: (i, 0))
    return pl.pallas_call(
        functools.partial(_combine_kernel, final=final),
        grid=(nt,),
        in_specs=[pl.BlockSpec((1, 1, tm * TOP_K), lambda i: (i, 0, 0), memory_space=pltpu.SMEM),
                  pl.BlockSpec(memory_space=pl.ANY),
                  row(LANES), row(d), _mod_specs(tpb, 2),
                  pl.BlockSpec((1, d), lambda i: (0, 0))],
        out_specs=row(d),
        out_shape=jax.ShapeDtypeStruct((t, d), F32),
        scratch_shapes=[pltpu.VMEM((TOP_K, tm * SLAB, LANES), ys.dtype), pltpu.SemaphoreType.DMA(())],
        compiler_params=_cparams(("arbitrary",)),
        name="moe_combine",
    )(dest.reshape(nt, 1, tm * TOP_K), ys, gates, x2, mod, final_g.reshape(1, -1))


def _moe_sublayer(x2, c_mod, norm_g, router_w, router_b, w1, b1, w2, b2, final_g, seq, final):
    t = x2.shape[0]
    h, route, gates, cnt = _route(x2, norm_g, c_mod, router_w, router_b, seq)
    top_e = route[:, :TOP_K]
    rank = route[:, TOP_K:2 * TOP_K]
    counts = cnt[0, :N_EXPERTS].astype(jnp.int32)
    padded = ((counts + FFN_TM - 1) // FFN_TM) * FFN_TM
    starts = jnp.cumsum(padded) - padded
    dest = starts[top_e] + rank
    ns = t * TOP_K + N_EXPERTS * FFN_TM
    ns = ((ns + GATHER_T - 1) // GATHER_T) * GATHER_T
    tok = jnp.broadcast_to(jnp.arange(t, dtype=jnp.int32)[:, None], (t, TOP_K))
    src_tok = jnp.zeros((ns,), jnp.int32).at[dest.reshape(-1)].set(tok.reshape(-1))
    n_items = (padded + FFN_R - 1) // FFN_R
    item_end = jnp.cumsum(n_items)
    wmax = (t * TOP_K) // FFN_R + N_EXPERTS
    wid = jnp.arange(wmax, dtype=jnp.int32)
    nvalid = item_end[-1].astype(jnp.int32)
    we = jnp.minimum(jnp.searchsorted(item_end, wid, side='right'), N_EXPERTS - 1).astype(jnp.int32)
    last_e = we[jnp.maximum(nvalid - 1, 0)]
    valid = wid < nvalid
    we = jnp.where(valid, we, last_e)
    local = wid - (item_end - n_items)[we]
    row0 = jnp.where(valid, starts[we] + local * FFN_R, 0).astype(jnp.int32)
    ntile = jnp.where(valid, jnp.minimum(FFN_R, padded[we] - local * FFN_R) // FFN_TM, 0).astype(jnp.int32)

    xs = _gather_rows(h, src_tok)
    b1p = b1.reshape(N_EXPERTS, EXPERT_FF // FFN_FC, FFN_FC, 2).transpose(0, 1, 3, 2).reshape(N_EXPERTS, 1, 2 * EXPERT_FF)
    pm = np.zeros((2 * FFN_FC, 2 * FFN_FC), np.float32)
    jj = np.arange(FFN_FC)
    pm[2 * jj, jj] = 1.0
    pm[2 * jj + 1, FFN_FC + jj] = 1.0
    ys = _ffn(xs, w1, b1p, w2, b2.reshape(N_EXPERTS, 1, D_MODEL), jnp.asarray(pm, BF16),
              we, row0, ntile, jnp.stack([nvalid, jnp.sum(padded).astype(jnp.int32)]))
    return _combine(ys, dest.astype(jnp.int32), gates, x2, c_mod, final_g, seq, final)


def _conv_pw1_kernel(x_ref, g_ref, shift_ref, scale_ref, wa_ref, wb_ref, ba_ref, bb_ref, o_ref, h_ref):
    j = pl.program_id(1)

    @pl.when(j == 0)
    def _():
        h_ref[...] = _norm_mod(x_ref[...], g_ref[...], scale_ref[0], shift_ref[0]).astype(BF16)

    h = h_ref[...]
    a = _dot(h, wa_ref[...]) + ba_ref[...]
    b = _dot(h, wb_ref[...]) + bb_ref[...]
    o_ref[...] = a * jax.nn.sigmoid(b)


def _conv_pw1(x2, g, mod, w_pw1, b_pw1, seq):
    t = x2.shape[0]
    tm = min(PROJ_TM, seq)
    tpb = seq // tm
    tn = 512
    nj = D_MODEL // tn
    mod2 = lambda which: pl.BlockSpec((1, 1, D_MODEL), lambda i, j: (i // tpb, 0, which))
    b2 = b_pw1.reshape(1, -1)
    return pl.pallas_call(
        _conv_pw1_kernel,
        grid=(t // tm, nj),
        in_specs=[pl.BlockSpec((tm, D_MODEL), lambda i, j: (i, 0)),
                  pl.BlockSpec((1, D_MODEL), lambda i, j: (0, 0)),
                  mod2(0), mod2(1),
                  pl.BlockSpec((D_MODEL, tn), lambda i, j: (0, j)),
                  pl.BlockSpec((D_MODEL, tn), lambda i, j: (0, j + nj)),
                  pl.BlockSpec((1, tn), lambda i, j: (0, j)),
                  pl.BlockSpec((1, tn), lambda i, j: (0, j + nj))],
        out_specs=pl.BlockSpec((tm, tn), lambda i, j: (i, j)),
        out_shape=jax.ShapeDtypeStruct((t, D_MODEL), F32),
        scratch_shapes=[pltpu.VMEM((tm, D_MODEL), BF16)],
        compiler_params=_cparams(("arbitrary", "arbitrary")),
        name="conv_pw1",
    )(x2, g.reshape(1, -1), mod, mod, w_pw1, w_pw1, b2, b2)


def _conv_dw_kernel(y_ref, halo_ref, wdw_ref, bdw_ref, lng_ref, lnb_ref, wp_ref, bp_ref, x_ref, gate_ref,
                    o_ref, buf_ref, z_ref, *, tpb):
    i = pl.program_id(0)
    tm = y_ref.shape[0]
    first = (i % tpb) == 0
    halo = halo_ref[...]
    buf_ref[0:CONV_HALO, :] = jnp.where(first, 0.0, halo)
    buf_ref[CONV_HALO:, :] = y_ref[...]
    cw = 256
    rb = 64
    off = CONV_HALO - (CONV_KERNEL - 1)
    for c0 in range(0, D_MODEL, cw):
        for r0 in range(0, tm, rb):
            acc = jnp.zeros((rb, cw), F32)
            for k in range(CONV_KERNEL):
                acc = acc + wdw_ref[k:k + 1, c0:c0 + cw] * buf_ref[r0 + off + k:r0 + off + k + rb, c0:c0 + cw]
            z_ref[r0:r0 + rb, c0:c0 + cw] = acc
    z = z_ref[...] + bdw_ref[...]
    mu = jnp.mean(z, axis=-1, keepdims=True)
    zc = z - mu
    var = jnp.mean(zc * zc, axis=-1, keepdims=True)
    zn = zc * lax.rsqrt(var + LN_EPS) * lng_ref[...] + lnb_ref[...]
    act = (zn * jax.nn.sigmoid(zn)).astype(BF16)
    o_ref[...] = x_ref[...] + gate_ref[0] * (_dot(act, wp_ref[...]) + bp_ref[...])


def _conv_dw(y1, x2, mod, w_dw, b_dw, ln_g, ln_b, w_pw2, b_pw2, seq):
    t = x2.shape[0]
    tm = min(CONV_TM, seq)
    tpb = seq // tm
    hb = tm // CONV_HALO
    wdw = jnp.zeros((CONV_HALO, D_MODEL), F32).at[:CONV_KERNEL].set(w_dw)
    row = lambda: pl.BlockSpec((tm, D_MODEL), lambda i: (i, 0))
    vec = lambda: pl.BlockSpec((1, D_MODEL), lambda i: (0, 0))
    return pl.pallas_call(
        functools.partial(_conv_dw_kernel, tpb=tpb),
        grid=(t // tm,),
        in_specs=[row(),
                  pl.BlockSpec((CONV_HALO, D_MODEL), lambda i: (jnp.maximum(i * hb - 1, 0), 0)),
                  pl.BlockSpec((CONV_HALO, D_MODEL), lambda i: (0, 0)),
                  vec(), vec(), vec(),
                  pl.BlockSpec((D_MODEL, D_MODEL), lambda i: (0, 0)),
                  vec(), row(), _mod_specs(tpb, 2)],
        out_specs=row(),
        out_shape=jax.ShapeDtypeStruct((t, D_MODEL), F32),
        scratch_shapes=[pltpu.VMEM((tm + CONV_HALO, D_MODEL), F32), pltpu.VMEM((tm, D_MODEL), F32)],
        compiler_params=_cparams(("arbitrary",)),
        name="conv_dw_pw2",
    )(y1, y1, wdw, b_dw.reshape(1, -1), ln_g.reshape(1, -1), ln_b.reshape(1, -1),
      w_pw2, b_pw2.reshape(1, -1), x2, mod)


def _rope_swap_cols(w, heads):
    k = w.shape[0]
    half = MLA_ROPE // 2
    w3 = w.reshape(k, heads, MLA_NOPE + MLA_ROPE)
    pe = w3[:, :, MLA_NOPE:]
    sw = jnp.concatenate([pe[:, :, half:], pe[:, :, :half]], axis=-1)
    return jnp.concatenate([w3, sw], axis=-1).reshape(k, heads * (MLA_NOPE + 2 * MLA_ROPE))


def kernel(x, c, positions, l0_mix_norm_g, l0_mix_ada_w, l0_mix_ada_b, l0_w_in, l0_s5_lambda_re, l0_s5_lambda_im, l0_s5_log_step, l0_s5_b_re, l0_s5_b_im, l0_s5_c_re, l0_s5_c_im, l0_s5_d, l0_s5_w_glu, l0_s5_b_glu, l0_mla_q_norm_g, l0_mla_w_uq, l0_mla_kv_norm_g, l0_mla_w_ukv, l0_w_out, l0_moe_norm_g, l0_moe_ada_w, l0_moe_ada_b, l0_router_w, l0_router_b, l0_exp_w1, l0_exp_b1, l0_exp_w2, l0_exp_b2, l1_mix_norm_g, l1_mix_ada_w, l1_mix_ada_b, l1_conv_w_pw1, l1_conv_b_pw1, l1_conv_w_dw, l1_conv_b_dw, l1_conv_ln_g, l1_conv_ln_b, l1_conv_w_pw2, l1_conv_b_pw2, l1_moe_norm_g, l1_moe_ada_w, l1_moe_ada_b, l1_router_w, l1_router_b, l1_exp_w1, l1_exp_b1, l1_exp_w2, l1_exp_b2, final_norm_g):
    nbatch, seq, d = x.shape
    t = nbatch * seq
    x2 = x.reshape(t, d)
    pos = positions.reshape(t, 1).astype(jnp.int32)

    mod = _ada_mod(c, l0_mix_ada_w, l0_mix_ada_b)
    half = MLA_ROPE // 2
    kcol = S5_WIDTH + MLA_Q_RANK + MLA_KV_RANK
    w_in_ext = jnp.concatenate([l0_w_in, l0_w_in[:, kcol + half:], l0_w_in[:, kcol:kcol + half]], axis=1).astype(BF16)
    u, cq, ckv, kr = _proj_in(x2, l0_mix_norm_g, mod, w_in_ext, seq)

    kt, wt, vv, a16 = _s5_prep(l0_s5_lambda_re, l0_s5_lambda_im, l0_s5_log_step, l0_s5_b_re, l0_s5_b_im,
                               l0_s5_c_re, l0_s5_c_im, l0_s5_d)
    nchunk = seq // S5_CHUNK
    u_r = (u.reshape(nbatch, nchunk, S5_CHUNK, S5_GROUPS, S5_GROUP_CH)
           .transpose(3, 0, 1, 2, 4).reshape(S5_GROUPS, nbatch * nchunk, S5_ROW))
    y_r = _s5_main(u_r, kt, wt, vv, a16, nbatch, nchunk)
    ys = (y_r.reshape(S5_GROUPS, nbatch, nchunk, S5_CHUNK, S5_GROUP_CH)
          .transpose(1, 2, 3, 0, 4).reshape(t, S5_WIDTH))

    wq_ext = _rope_swap_cols(l0_mla_w_uq, MLA_HEADS).astype(BF16)
    q, k, v = _mla_proj(cq, ckv, kr, pos, l0_mla_q_norm_g, l0_mla_kv_norm_g, wq_ext,
                        l0_mla_w_ukv.astype(BF16), nbatch, seq)
    ym = _flash_attention(q, k, v).reshape(t, MLA_HEADS * MLA_V)
    x2 = _mixer_out(ys, ym, x2, mod, l0_s5_w_glu.astype(BF16), l0_s5_b_glu, l0_w_out.astype(BF16), seq)

    mod = _ada_mod(c, l0_moe_ada_w, l0_moe_ada_b)
    x2 = _moe_sublayer(x2, mod, l0_moe_norm_g, l0_router_w, l0_router_b, l0_exp_w1, l0_exp_b1,
                       l0_exp_w2, l0_exp_b2, final_norm_g, seq, final=False)

    mod = _ada_mod(c, l1_mix_ada_w, l1_mix_ada_b)
    y1 = _conv_pw1(x2, l1_mix_norm_g, mod, l1_conv_w_pw1.astype(BF16), l1_conv_b_pw1, seq)
    x2 = _conv_dw(y1, x2, mod, l1_conv_w_dw, l1_conv_b_dw, l1_conv_ln_g, l1_conv_ln_b,
                  l1_conv_w_pw2.astype(BF16), l1_conv_b_pw2, seq)

    mod = _ada_mod(c, l1_moe_ada_w, l1_moe_ada_b)
    x2 = _moe_sublayer(x2, mod, l1_moe_norm_g, l1_router_w, l1_router_b, l1_exp_w1, l1_exp_b1,
                       l1_exp_w2, l1_exp_b2, final_norm_g, seq, final=True)
    return x2.reshape(nbatch, seq, d)
```

```python
import functools
import math

import numpy as np
import jax
import jax.numpy as jnp
from jax import lax
from jax.experimental import pallas as pl
from jax.experimental.pallas import tpu as pltpu

F32 = jnp.float32
BF16 = jnp.bfloat16

D_MODEL = 2048
S5_WIDTH = 1024
S5_GROUP_CH = 16
S5_GROUPS = 64
S5_STATE = 64
MLA_HEADS = 8
MLA_NOPE = 128
MLA_ROPE = 64
MLA_V = 128
MLA_Q_RANK = 512
MLA_KV_RANK = 512
ROPE_THETA = 10000.0
CONV_KERNEL = 31
N_EXPERTS = 32
TOP_K = 4
EXPERT_FF = 2048
SWIGLU_ALPHA = 1.702
SWIGLU_LIMIT = 7.0
EPS = 1e-6
LN_EPS = 1e-5
NEG_BIG = -1e30

LANES = 128
SUBLANES = 8
VMEM_LIMIT_BYTES = 56 * 1024 * 1024

S5_CHUNK = 16
S5_ROW = S5_CHUNK * S5_GROUP_CH
PROJ_TM = 512
MLA_TM = 256
ATT_T = 512
OUT_TM = 256
ROUTE_TM = 512
CONV_TM = 256
CONV_HALO = 32
FFN_TM = 256
FFN_R = 2048
FFN_FC = 512
GATHER_T = 256
COMBINE_T = 128


def _cparams(sem):
    return pltpu.CompilerParams(dimension_semantics=sem, vmem_limit_bytes=VMEM_LIMIT_BYTES)


def _norm_mod(x, g, scale, shift):
    ms = jnp.mean(x * x, axis=-1, keepdims=True)
    return (x * lax.rsqrt(ms + EPS)) * g * (1.0 + scale) + shift


def _rms(x, g):
    ms = jnp.mean(x * x, axis=-1, keepdims=True)
    return (x * lax.rsqrt(ms + EPS)) * g


def _dot(a, b):
    return jnp.dot(a, b, preferred_element_type=F32)


def _dot_nt(a, b):
    return lax.dot_general(a, b, (((1,), (1,)), ((), ())), preferred_element_type=F32)


U32 = jnp.uint32
PACK = D_MODEL // (2 * LANES)
HI_MASK = 0xFFFF0000


def _pack_store(ref, val, lead=()):
    n = val.shape[0]
    half = D_MODEL // 2
    for s in range(PACK):
        lo = val[:, s * LANES:(s + 1) * LANES].astype(BF16).astype(F32)
        hi = val[:, half + s * LANES:half + (s + 1) * LANES].astype(BF16).astype(F32)
        word = (lax.bitcast_convert_type(lo, U32) >> 16) | (lax.bitcast_convert_type(hi, U32) & jnp.uint32(HI_MASK))
        ref[lead + (pl.ds(s, n, stride=PACK), slice(None))] = word


def _unpack_load(ref, n, s, lead=()):
    word = ref[lead + (pl.ds(s, n, stride=PACK), slice(None))]
    lo = lax.bitcast_convert_type(word << 16, F32)
    hi = lax.bitcast_convert_type(word & jnp.uint32(HI_MASK), F32)
    return lo, hi


def _prow(r, n=1):
    return pl.ds(pl.multiple_of(r * PACK, PACK), n * PACK)


ADA_TN = 768
ADA_KC = 256


def _ada_kernel(ct_ref, w_ref, b_ref, o_ref):
    nb = ct_ref.shape[0]
    kdim = w_ref.shape[0]
    rows = []
    for b in range(nb):
        acc = jnp.zeros((1, w_ref.shape[1]), F32)
        for k0 in range(0, kdim, ADA_KC):
            c = ct_ref[b, k0:k0 + ADA_KC, :]
            cs = c * jax.nn.sigmoid(c)
            acc = acc + jnp.sum(w_ref[k0:k0 + ADA_KC, :] * cs, axis=0, keepdims=True)
        rows.append(acc)
    o_ref[...] = jnp.concatenate(rows, axis=0) + b_ref[...]


def _ada_mod(c, w, b):
    nb, d = c.shape
    n = w.shape[1]
    ct = c.reshape(nb, d, 1)
    m = pl.pallas_call(
        _ada_kernel,
        grid=(n // ADA_TN,),
        in_specs=[pl.BlockSpec((nb, d, 1), lambda j: (0, 0, 0)),
                  pl.BlockSpec((d, ADA_TN), lambda j: (0, j)),
                  pl.BlockSpec((1, ADA_TN), lambda j: (0, j))],
        out_specs=pl.BlockSpec((nb, ADA_TN), lambda j: (0, j)),
        out_shape=jax.ShapeDtypeStruct((nb, n), F32),
        compiler_params=_cparams(("arbitrary",)),
        name="ada_mod",
    )(ct, w, b.reshape(1, n))
    return m.reshape(nb, 1, n)


def _mod_specs(tiles_per_batch, which):
    return pl.BlockSpec((1, 1, D_MODEL), lambda i, *_: (i // tiles_per_batch, 0, which))


def _proj_in_kernel(x_ref, g_ref, shift_ref, scale_ref, w_ref, u_ref, cq_ref, ckv_ref, kr_ref):
    h = _norm_mod(x_ref[...], g_ref[...], scale_ref[0], shift_ref[0]).astype(BF16)
    acc = _dot(h, w_ref[...])
    c0, c1, c2 = S5_WIDTH, S5_WIDTH + MLA_Q_RANK, S5_WIDTH + MLA_Q_RANK + MLA_KV_RANK
    u_ref[...] = acc[:, :c0].astype(BF16)
    cq_ref[...] = acc[:, c0:c1]
    ckv_ref[...] = acc[:, c1:c2]
    kr_ref[...] = acc[:, c2:]


def _proj_in(x2, g, mod, w_ext, seq):
    t = x2.shape[0]
    tm = min(PROJ_TM, seq)
    tpb = seq // tm
    n = w_ext.shape[1]
    row = lambda w: pl.BlockSpec((tm, w), lambda i: (i, 0))
    return pl.pallas_call(
        _proj_in_kernel,
        grid=(t // tm,),
        in_specs=[row(D_MODEL),
                  pl.BlockSpec((1, D_MODEL), lambda i: (0, 0)),
                  _mod_specs(tpb, 0), _mod_specs(tpb, 1),
                  pl.BlockSpec((D_MODEL, n), lambda i: (0, 0))],
        out_specs=[row(S5_WIDTH), row(MLA_Q_RANK), row(MLA_KV_RANK), row(2 * MLA_ROPE)],
        out_shape=[jax.ShapeDtypeStruct((t, S5_WIDTH), BF16),
                   jax.ShapeDtypeStruct((t, MLA_Q_RANK), F32),
                   jax.ShapeDtypeStruct((t, MLA_KV_RANK), F32),
                   jax.ShapeDtypeStruct((t, 2 * MLA_ROPE), F32)],
        compiler_params=_cparams(("arbitrary",)),
        name="proj_in",
    )(x2, g.reshape(1, -1), mod, mod, w_ext)


def _s5_prep_kernel(lamc_ref, lamr_ref, step_ref, bt_ref, btt_ref, ct_ref, d_ref,
                    kt_ref, wt_ref, v_ref, a16_ref):
    P, H, C = S5_STATE, S5_GROUP_CH, S5_CHUNK
    step = step_ref[0]
    step = jnp.exp(step)
    lr_c = jnp.minimum(lamc_ref[0, 0], -1e-4)
    li_c = lamc_ref[0, 1]
    lr_r = jnp.minimum(lamr_ref[0, 0:1, :], -1e-4)
    li_r = lamr_ref[0, 1:2, :]

    def ratio(lr, li):
        mag = jnp.exp(lr * step)
        ab_re = mag * jnp.cos(li * step)
        ab_im = mag * jnp.sin(li * step)
        denom = lr * lr + li * li
        nr, ni = ab_re - 1.0, ab_im
        return (nr * lr + ni * li) / denom, (ni * lr - nr * li) / denom

    rr_c, ri_c = ratio(lr_c, li_c)
    rr_r, ri_r = ratio(lr_r, li_r)

    lane = lax.broadcasted_iota(jnp.int32, (1, S5_ROW), 1)
    kk = (lane // H).astype(F32)

    def powers(k):
        mag = jnp.exp(lr_c * step * k)
        return mag * jnp.cos(li_c * step * k), mag * jnp.sin(li_c * step * k)

    bre_t, bim_t = bt_ref[0, 0], bt_ref[0, 1]
    bbt_re = rr_c * bre_t - ri_c * bim_t
    bbt_im = rr_c * bim_t + ri_c * bre_t
    cre_t, cim_t = ct_ref[0, 0], ct_ref[0, 1]

    er, ei = powers(float(C - 1) - kk)
    wt_ref[0, 0:P, :] = (er * bbt_re - ei * bbt_im).astype(BF16)
    wt_ref[0, P:2 * P, :] = (er * bbt_im + ei * bbt_re).astype(BF16)

    er, ei = powers(kk + 1.0)
    v_ref[0, 0:P, :] = (cre_t * er - cim_t * ei).astype(BF16)
    v_ref[0, P:2 * P, :] = (-cre_t * ei - cim_t * er).astype(BF16)

    er, ei = powers(kk)
    q_re = er * cre_t - ei * cim_t
    q_im = er * cim_t + ei * cre_t
    brt, bit = btt_ref[0, 0], btt_ref[0, 1]
    bbr = rr_r * brt - ri_r * bit
    bbi = rr_r * bit + ri_r * brt
    hi = lax.Precision.HIGHEST
    mall = (jnp.dot(bbr, q_re, precision=hi, preferred_element_type=F32)
            - jnp.dot(bbi, q_im, precision=hi, preferred_element_type=F32))
    rowh = lax.broadcasted_iota(jnp.int32, (H, S5_ROW), 0)
    laneh = lax.broadcasted_iota(jnp.int32, (H, S5_ROW), 1)
    mall = mall + jnp.where(laneh == rowh, d_ref[0], 0.0)
    for s in range(C):
        piece = mall if s == 0 else pltpu.roll(mall, H * s, 1)
        piece = jnp.where(laneh >= H * s, piece, 0.0)
        kt_ref[0, s * H:(s + 1) * H, :] = piece.astype(BF16)

    mag = jnp.exp(lr_r * step * float(C))
    a16_ref[0, 0:1, :] = mag * jnp.cos(li_r * step * float(C))
    a16_ref[0, 1:2, :] = mag * jnp.sin(li_r * step * float(C))


def _s5_prep(lam_re, lam_im, log_step, b_re, b_im, c_re, c_im, d):
    G, P, H, C = S5_GROUPS, S5_STATE, S5_GROUP_CH, S5_CHUNK
    lam = jnp.stack([lam_re, lam_im], axis=1).astype(F32)
    lamc = lam.reshape(G, 2, P, 1)
    b = jnp.stack([b_re, b_im], axis=1).astype(F32)
    bt = jnp.tile(b, (1, 1, 1, C))
    btt = jnp.swapaxes(b, 2, 3)
    c = jnp.stack([c_re, c_im], axis=1).astype(F32)
    ct = jnp.tile(jnp.swapaxes(c, 2, 3), (1, 1, 1, C))
    dt = jnp.tile(d.astype(F32), (1, C)).reshape(G, 1, S5_ROW)
    step = log_step.astype(F32).reshape(G, 1, 1)
    g4 = lambda *shape: pl.BlockSpec((1,) + shape, lambda g: (g,) + (0,) * len(shape))
    return pl.pallas_call(
        _s5_prep_kernel,
        grid=(G,),
        in_specs=[g4(2, P, 1), g4(2, P), g4(1, 1), g4(2, P, S5_ROW), g4(2, H, P),
                  g4(2, P, S5_ROW), g4(1, S5_ROW)],
        out_specs=[g4(S5_ROW, S5_ROW), g4(2 * P, S5_ROW), g4(2 * P, S5_ROW), g4(2, P)],
        out_shape=[jax.ShapeDtypeStruct((G, S5_ROW, S5_ROW), BF16),
                   jax.ShapeDtypeStruct((G, 2 * P, S5_ROW), BF16),
                   jax.ShapeDtypeStruct((G, 2 * P, S5_ROW), BF16),
                   jax.ShapeDtypeStruct((G, 2, P), F32)],
        compiler_params=_cparams(("arbitrary",)),
        name="s5_prep",
    )(lamc, lam, step, bt, btt, ct, dt)


def _s5_main_kernel(u_ref, kt_ref, wt_ref, v_ref, a16_ref, y_ref, sr_ref, si_ref, xr_ref, xi_ref,
                    *, nbatch, nchunk):
    P = S5_STATE
    u = u_ref[0]
    sr_ref[...] = _dot_nt(u, wt_ref[0, 0:P, :])
    si_ref[...] = _dot_nt(u, wt_ref[0, P:2 * P, :])
    ar = a16_ref[0, 0:1, :]
    ai = a16_ref[0, 1:2, :]

    def step(c, carry):
        out = []
        for b in range(nbatch):
            xr, xi = carry[2 * b], carry[2 * b + 1]
            r = b * nchunk + c
            xr_ref[pl.ds(r, 1), :] = xr
            xi_ref[pl.ds(r, 1), :] = xi
            nxr = ar * xr - ai * xi + sr_ref[pl.ds(r, 1), :]
            nxi = ar * xi + ai * xr + si_ref[pl.ds(r, 1), :]
            out += [nxr, nxi]
        return tuple(out)

    zero = jnp.zeros((1, P), F32)
    lax.fori_loop(0, nchunk, step, (zero,) * (2 * nbatch))
    y = _dot(u, kt_ref[0])
    y = y + _dot(xr_ref[...].astype(BF16), v_ref[0, 0:P, :])
    y = y + _dot(xi_ref[...].astype(BF16), v_ref[0, P:2 * P, :])
    y_ref[0] = y


def _s5_main(u_r, kt, wt, v, a16, nbatch, nchunk):
    G, rows, _ = u_r.shape
    P = S5_STATE
    g3 = lambda a, b: pl.BlockSpec((1, a, b), lambda g: (g, 0, 0))
    return pl.pallas_call(
        functools.partial(_s5_main_kernel, nbatch=nbatch, nchunk=nchunk),
        grid=(G,),
        in_specs=[g3(rows, S5_ROW), g3(S5_ROW, S5_ROW), g3(2 * P, S5_ROW), g3(2 * P, S5_ROW), g3(2, P)],
        out_specs=g3(rows, S5_ROW),
        out_shape=jax.ShapeDtypeStruct((G, rows, S5_ROW), F32),
        scratch_shapes=[pltpu.VMEM((rows, P), F32) for _ in range(4)],
        compiler_params=_cparams(("arbitrary",)),
        name="s5_main",
    )(u_r, kt, wt, v, a16)


def _mla_proj_kernel(cq_ref, ckv_ref, kr_ref, pos_ref, qg_ref, kvg_ref, wq_ref, wkv_ref,
                     invf_ref, sgn_ref, q_ref, k_ref, v_ref):
    qscale = 1.0 / math.sqrt(MLA_NOPE + MLA_ROPE)
    qa = _dot(_rms(cq_ref[...], qg_ref[...]).astype(BF16), wq_ref[...])
    kva = _dot(_rms(ckv_ref[...], kvg_ref[...]).astype(BF16), wkv_ref[...])
    ang = pos_ref[...].astype(F32) * invf_ref[...]
    cc = jnp.cos(ang)
    ss = jnp.sin(ang) * sgn_ref[...]

    def rope(slab):
        return slab * cc + pltpu.roll(slab, MLA_ROPE, 1) * ss

    kpe = rope(kr_ref[...])[:, :MLA_ROPE].astype(BF16)
    hw = MLA_NOPE + 2 * MLA_ROPE
    for h in range(MLA_HEADS):
        blk = qa[:, h * hw:(h + 1) * hw]
        q_ref[0, h, :, 0:MLA_NOPE] = (blk[:, :MLA_NOPE] * qscale).astype(BF16)
        qpe = rope(blk[:, MLA_NOPE:]) * qscale
        q_ref[0, h, :, MLA_NOPE:MLA_NOPE + MLA_ROPE] = qpe[:, :MLA_ROPE].astype(BF16)
        kvb = kva[:, h * hw:(h + 1) * hw]
        k_ref[0, h, :, 0:MLA_NOPE] = kvb[:, :MLA_NOPE].astype(BF16)
        k_ref[0, h, :, MLA_NOPE:MLA_NOPE + MLA_ROPE] = kpe
        v_ref[0, h] = kvb[:, MLA_NOPE:].astype(BF16)


def _mla_proj(cq, ckv, kr, pos, qg, kvg, wq_ext, wkv, nbatch, seq):
    tm = min(MLA_TM, seq)
    nl = seq // tm
    dqk = MLA_NOPE + MLA_ROPE
    half = MLA_ROPE // 2
    inv_freq = 1.0 / (ROPE_THETA ** (jnp.arange(0, MLA_ROPE, 2, dtype=F32) / MLA_ROPE))
    invf = jnp.tile(inv_freq, 4).reshape(1, 2 * MLA_ROPE)
    sgn = jnp.tile(jnp.concatenate([-jnp.ones((half,), F32), jnp.ones((half,), F32)]), 2).reshape(1, 2 * MLA_ROPE)
    row = lambda w: pl.BlockSpec((tm, w), lambda b, i: (b * nl + i, 0))
    full = lambda a, b_: pl.BlockSpec((a, b_), lambda b, i: (0, 0))
    head = lambda w: pl.BlockSpec((1, MLA_HEADS, tm, w), lambda b, i: (b, 0, i, 0))
    return pl.pallas_call(
        _mla_proj_kernel,
        grid=(nbatch, nl),
        in_specs=[row(MLA_Q_RANK), row(MLA_KV_RANK), row(2 * MLA_ROPE), row(1),
                  full(1, MLA_Q_RANK), full(1, MLA_KV_RANK),
                  full(MLA_Q_RANK, wq_ext.shape[1]), full(MLA_KV_RANK, wkv.shape[1]),
                  full(1, 2 * MLA_ROPE), full(1, 2 * MLA_ROPE)],
        out_specs=[head(dqk), head(dqk), head(MLA_V)],
        out_shape=[jax.ShapeDtypeStruct((nbatch, MLA_HEADS, seq, dqk), BF16),
                   jax.ShapeDtypeStruct((nbatch, MLA_HEADS, seq, dqk), BF16),
                   jax.ShapeDtypeStruct((nbatch, MLA_HEADS, seq, MLA_V), BF16)],
        compiler_params=_cparams(("arbitrary", "arbitrary")),
        name="mla_proj",
    )(cq, ckv, kr, pos, qg.reshape(1, -1), kvg.reshape(1, -1), wq_ext, wkv, invf, sgn)


def _lane_groups(x):
    return [x[:, j * LANES:(j + 1) * LANES] for j in range(x.shape[1] // LANES)]


def _flash_kernel(q_ref, k_ref, v_ref, o_ref, s_ref, *, t):
    qi = pl.program_id(2)
    q = q_ref[0, 0]

    def scores(c):
        kc = k_ref[0, 0, pl.ds(pl.multiple_of(c * t, t), t), :]
        return _dot_nt(q, kc)

    def pass1(c, mx):
        s = scores(c)
        s_ref[c] = s
        return functools.reduce(jnp.maximum, _lane_groups(s), mx)

    mx = lax.fori_loop(0, qi, pass1, jnp.full((t, LANES), NEG_BIG, F32))
    s = scores(qi)
    row = lax.broadcasted_iota(jnp.int32, s.shape, 0)
    col = lax.broadcasted_iota(jnp.int32, s.shape, 1)
    s = jnp.where(col <= row, s, NEG_BIG)
    s_ref[qi] = s
    mx = functools.reduce(jnp.maximum, _lane_groups(s), mx)
    m = jnp.broadcast_to(jnp.max(mx, axis=-1, keepdims=True), (t, LANES))

    def pass2(c, carry):
        acc, ls = carry
        parts = []
        for sj in _lane_groups(s_ref[c]):
            pj = jnp.exp(sj - m)
            ls = ls + pj
            parts.append(pj.astype(BF16))
        vc = v_ref[0, 0, pl.ds(pl.multiple_of(c * t, t), t), :]
        return acc + _dot(jnp.concatenate(parts, axis=1), vc), ls

    acc, ls = lax.fori_loop(0, qi + 1, pass2, (jnp.zeros((t, MLA_V), F32), jnp.zeros((t, LANES), F32)))
    o_ref[0] = (acc / jnp.sum(ls, axis=-1, keepdims=True)).astype(o_ref.dtype)


def _flash_attention(q, k, v):
    nbatch, nh, seq, dqk = q.shape
    t = min(ATT_T, seq)
    nq = seq // t
    return pl.pallas_call(
        functools.partial(_flash_kernel, t=t),
        grid=(nbatch, nh, nq),
        in_specs=[pl.BlockSpec((1, 1, t, dqk), lambda b, h, i: (b, h, i, 0)),
                  pl.BlockSpec((1, 1, seq, dqk), lambda b, h, i: (b, h, 0, 0)),
                  pl.BlockSpec((1, 1, seq, MLA_V), lambda b, h, i: (b, h, 0, 0))],
        out_specs=pl.BlockSpec((1, t, MLA_V), lambda b, h, i: (b, i, h)),
        out_shape=jax.ShapeDtypeStruct((nbatch, seq, nh * MLA_V), BF16),
        scratch_shapes=[pltpu.VMEM((nq, t, t), F32)],
        compiler_params=_cparams(("arbitrary", "arbitrary", "arbitrary")),
        name="flash_attn",
    )(q, k, v)


def _gelu_tanh(x):
    c = math.sqrt(2.0 / math.pi)
    return 0.5 * x * (1.0 + jnp.tanh(c * (x + 0.044715 * (x * x * x))))


def _mixer_out_kernel(ys_ref, ym_ref, x_ref, gate_ref, wglu_ref, bglu_ref, wo_ref, o_ref):
    y = _gelu_tanh(ys_ref[...])
    g = _dot(y.astype(BF16), wglu_ref[...]) + bglu_ref[...]
    s5o = (y * jax.nn.sigmoid(g)).astype(BF16)
    acc = _dot(s5o, wo_ref[0:S5_WIDTH, :]) + _dot(ym_ref[...], wo_ref[S5_WIDTH:, :])
    o_ref[...] = x_ref[...] + gate_ref[0] * acc


def _mixer_out(ys, ym, x2, mod, w_glu, b_glu, w_out, seq):
    t = x2.shape[0]
    tm = min(OUT_TM, seq)
    tpb = seq // tm
    row = lambda w: pl.BlockSpec((tm, w), lambda i: (i, 0))
    full = lambda a, b: pl.BlockSpec((a, b), lambda i: (0, 0))
    return pl.pallas_call(
        _mixer_out_kernel,
        grid=(t // tm,),
        in_specs=[row(S5_WIDTH), row(MLA_HEADS * MLA_V), row(D_MODEL), _mod_specs(tpb, 2),
                  full(S5_WIDTH, S5_WIDTH), full(1, S5_WIDTH), full(2 * S5_WIDTH, D_MODEL)],
        out_specs=row(D_MODEL),
        out_shape=jax.ShapeDtypeStruct((t, D_MODEL), F32),
        compiler_params=_cparams(("arbitrary",)),
        name="mixer_out",
    )(ys, ym, x2, mod, w_glu, b_glu.reshape(1, -1), w_out)


def _route_kernel(x_ref, g_ref, shift_ref, scale_ref, rw_ref, rb_ref, tri_ref,
                  h_ref, route_ref, gates_ref, cnt_ref, carry_ref):
    i = pl.program_id(0)

    @pl.when(i == 0)
    def _():
        carry_ref[...] = jnp.zeros(carry_ref.shape, F32)

    h = _norm_mod(x_ref[...], g_ref[...], scale_ref[0], shift_ref[0])
    _pack_store(h_ref, h)
    logits = jnp.dot(h, rw_ref[...], precision=lax.Precision.HIGHEST,
                     preferred_element_type=F32) + rb_ref[...]
    lane = lax.broadcasted_iota(jnp.int32, logits.shape, 1)
    lanef = lane.astype(F32)
    lg = jnp.where(lane < N_EXPERTS, logits, -jnp.inf)
    vals, hots, idxs = [], [], []
    sel = jnp.zeros(logits.shape, F32)
    for _ in range(TOP_K):
        m = jnp.max(lg, axis=-1, keepdims=True)
        idx = jnp.min(jnp.where(lg == m, lanef, float(LANES)), axis=-1, keepdims=True)
        hot = lanef == idx
        vals.append(m)
        idxs.append(idx)
        hots.append(hot)
        sel = jnp.where(hot, 1.0, sel)
        lg = jnp.where(hot, -jnp.inf, lg)
    es = [jnp.exp(v - vals[0]) for v in vals]
    denom = es[0] + es[1] + es[2] + es[3]
    before = _dot(tri_ref[...], sel.astype(BF16)) + carry_ref[...]
    route = jnp.zeros(logits.shape, F32)
    gates = jnp.zeros(logits.shape, F32)
    for k in range(TOP_K):
        rank = jnp.sum(jnp.where(hots[k], before, 0.0), axis=-1, keepdims=True)
        route = jnp.where(lane == k, idxs[k], route)
        route = jnp.where(lane == TOP_K + k, rank, route)
        gates = jnp.where(lane == k, es[k] / denom, gates)
    route_ref[...] = route.astype(jnp.int32)
    gates_ref[...] = gates
    carry_ref[...] = carry_ref[...] + jnp.sum(sel, axis=0, keepdims=True)
    cnt_ref[...] = carry_ref[...]


def _route(x2, g, mod, router_w, router_b, seq):
    t = x2.shape[0]
    tm = min(ROUTE_TM, seq)
    tpb = seq // tm
    rw = jnp.zeros((D_MODEL, LANES), F32).at[:, :N_EXPERTS].set(router_w)
    rb = jnp.zeros((1, LANES), F32).at[0, :N_EXPERTS].set(router_b)
    tri = jnp.asarray(np.tril(np.ones((tm, tm), np.float32), -1), BF16)
    row = lambda w: pl.BlockSpec((tm, w), lambda i: (i, 0))
    full = lambda a, b: pl.BlockSpec((a, b), lambda i: (0, 0))
    return pl.pallas_call(
        _route_kernel,
        grid=(t // tm,),
        in_specs=[row(D_MODEL), full(1, D_MODEL), _mod_specs(tpb, 0), _mod_specs(tpb, 1),
                  full(D_MODEL, LANES), full(1, LANES), full(tm, tm)],
        out_specs=[pl.BlockSpec((tm * PACK, LANES), lambda i: (i, 0)), row(LANES), row(LANES), full(1, LANES)],
        out_shape=[jax.ShapeDtypeStruct((t * PACK, LANES), U32),
                   jax.ShapeDtypeStruct((t, LANES), jnp.int32),
                   jax.ShapeDtypeStruct((t, LANES), F32),
                   jax.ShapeDtypeStruct((1, LANES), F32)],
        scratch_shapes=[pltpu.VMEM((1, LANES), F32)],
        compiler_params=_cparams(("arbitrary",)),
        name="moe_route",
    )(x2, g.reshape(1, -1), mod, mod, rw, rb, tri)


def _gather_kernel(cur_ref, nxt_ref, h_hbm, o_ref, buf_ref, sem):
    i = pl.program_id(0)
    n = GATHER_T
    slot = i % 2

    def row_copy(idx_ref, b, r):
        return pltpu.make_async_copy(h_hbm.at[_prow(idx_ref[0, 0, r]), :], buf_ref.at[b, _prow(r), :], sem.at[b])

    def issue(idx_ref, b):
        def body(r, c):
            row_copy(idx_ref, b, r).start()
            return c
        lax.fori_loop(0, n, body, 0, unroll=8)

    @pl.when(i == 0)
    def _():
        issue(cur_ref, 0)

    @pl.when(i + 1 < pl.num_programs(0))
    def _():
        issue(nxt_ref, 1 - slot)

    def wait(r, c):
        row_copy(cur_ref, slot, r).wait()
        return c

    lax.fori_loop(0, n, wait, 0, unroll=8)
    half = D_MODEL // 2
    for s in range(PACK):
        lo, hi = _unpack_load(buf_ref, n, s, lead=(slot,))
        o_ref[:, s * LANES:(s + 1) * LANES] = lo.astype(BF16)
        o_ref[:, half + s * LANES:half + (s + 1) * LANES] = hi.astype(BF16)


def _gather_rows(h_slabs, src_tok):
    ns = src_tok.shape[0]
    nt = ns // GATHER_T
    idx = src_tok.reshape(nt, 1, GATHER_T)
    return pl.pallas_call(
        _gather_kernel,
        grid=(nt,),
        in_specs=[pl.BlockSpec((1, 1, GATHER_T), lambda i: (i, 0, 0), memory_space=pltpu.SMEM),
                  pl.BlockSpec((1, 1, GATHER_T), lambda i: (jnp.minimum(i + 1, nt - 1), 0, 0),
                               memory_space=pltpu.SMEM),
                  pl.BlockSpec(memory_space=pl.ANY)],
        out_specs=pl.BlockSpec((GATHER_T, D_MODEL), lambda i: (i, 0)),
        out_shape=jax.ShapeDtypeStruct((ns, D_MODEL), BF16),
        scratch_shapes=[pltpu.VMEM((2, GATHER_T * PACK, LANES), U32), pltpu.SemaphoreType.DMA((2,))],
        compiler_params=_cparams(("arbitrary",)),
        name="moe_gather",
    )(idx, idx, h_slabs)


PERM_W = 256
W1_BLK = 1024


def _w1_prep_kernel(w_ref, p_ref, o_ref):
    for b in range(W1_BLK // PERM_W):
        cols = slice(b * PERM_W, (b + 1) * PERM_W)
        o_ref[0, :, cols] = _dot(w_ref[0, :, cols].astype(BF16), p_ref[...]).astype(BF16)


def _w1_prep(w1):
    e, d, n = w1.shape
    pm = np.zeros((PERM_W, PERM_W), np.float32)
    jj = np.arange(PERM_W // 2)
    pm[2 * jj, jj] = 1.0
    pm[2 * jj + 1, PERM_W // 2 + jj] = 1.0
    return pl.pallas_call(
        _w1_prep_kernel,
        grid=(e, n // W1_BLK),
        in_specs=[pl.BlockSpec((1, d, W1_BLK), lambda i, j: (i, 0, j)),
                  pl.BlockSpec((PERM_W, PERM_W), lambda i, j: (0, 0))],
        out_specs=pl.BlockSpec((1, d, W1_BLK), lambda i, j: (i, 0, j)),
        out_shape=jax.ShapeDtypeStruct((e, d, n), BF16),
        compiler_params=_cparams(("arbitrary", "arbitrary")),
        name="moe_w1_prep",
    )(w1, jnp.asarray(pm, BF16))


def _ffn_kernel(we_ref, row0_ref, nt_ref, nv_ref, xs_hbm, w1_ref, b1_ref, w2_ref, b2_ref,
                ys_hbm, x_ref, acc_ref, stage_ref, sem_in, sem_out):
    w = pl.program_id(0)
    f = pl.program_id(1)
    nf = pl.num_programs(1)
    nt = nt_ref[w]
    row0 = row0_ref[w]
    tm = FFN_TM
    hw = PERM_W // 2

    @pl.when(jnp.logical_and(f == 0, nt > 0))
    def _():
        def tile_copy(t):
            return pltpu.make_async_copy(xs_hbm.at[pl.ds(pl.multiple_of(row0 + t * tm, tm), tm), :],
                                         x_ref.at[pl.ds(pl.multiple_of(t * tm, tm), tm), :], sem_in)

        def start(t, c):
            tile_copy(t).start()
            return c

        def wait(t, c):
            tile_copy(t).wait()
            return c

        lax.fori_loop(0, nt, start, 0)
        lax.fori_loop(0, nt, wait, 0)

    def out_copy(t, slot):
        return pltpu.make_async_copy(stage_ref.at[slot], ys_hbm.at[_prow(row0 + t * tm, tm), :], sem_out.at[slot])

    @pl.when(nt > 0)
    def _():
        def tile(t, c):
            rows = pl.ds(pl.multiple_of(t * tm, tm), tm)
            a = _dot(x_ref[rows, :], w1_ref[0]) + b1_ref[0]
            nblk = a.shape[1] // PERM_W
            glu = jnp.concatenate([a[:, b * PERM_W:b * PERM_W + hw] for b in range(nblk)], axis=1)
            lin = jnp.concatenate([a[:, b * PERM_W + hw:(b + 1) * PERM_W] for b in range(nblk)], axis=1)
            glu = jnp.minimum(glu, SWIGLU_LIMIT)
            lin = jnp.clip(lin, -SWIGLU_LIMIT, SWIGLU_LIMIT)
            act = glu * jax.nn.sigmoid(SWIGLU_ALPHA * glu) * (lin + 1.0)
            contrib = _dot(act.astype(BF16), w2_ref[0])

            @pl.when(f == 0)
            def _():
                acc_ref[rows, :] = contrib + b2_ref[0]

            @pl.when(f > 0)
            def _():
                acc_ref[rows, :] = acc_ref[rows, :] + contrib

            @pl.when(f == nf - 1)
            def _():
                slot = t % 2

                @pl.when(t >= 2)
                def _():
                    out_copy(t - 2, slot).wait()

                _pack_store(stage_ref, acc_ref[rows, :], lead=(slot,))
                out_copy(t, slot).start()

            return c

        lax.fori_loop(0, nt, tile, 0)

        @pl.when(f == nf - 1)
        def _():
            for back in (2, 1):
                @pl.when(nt >= back)
                def _():
                    out_copy(nt - back, (nt - back) % 2).wait()

    @pl.when(jnp.logical_and(w == pl.num_programs(0) - 1, f == nf - 1))
    def _():
        used = nv_ref[1]
        ntail = (ys_hbm.shape[0] // PACK - used) // tm
        stage_ref[0] = jnp.zeros(stage_ref.shape[1:], U32)

        def fill(t, c):
            cp = pltpu.make_async_copy(stage_ref.at[0], ys_hbm.at[_prow(used + t * tm, tm), :], sem_out.at[0])
            cp.start()
            cp.wait()
            return c

        lax.fori_loop(0, ntail, fill, 0)


def _ffn(xs, w1p, b1p, w2b, b2, we, row0, ntile, nvalid):
    ns, d = xs.shape
    nf = EXPERT_FF // FFN_FC
    wmax = we.shape[0]

    def wmap(axis):
        def index(w, f, we_r, row0_r, nt_r, nv_r):
            fe = jnp.where(w < nv_r[0], f, nf - 1)
            return (we_r[w], 0, fe) if axis == 2 else (we_r[w], fe, 0)
        return index

    grid_spec = pltpu.PrefetchScalarGridSpec(
        num_scalar_prefetch=4,
        grid=(wmax, nf),
        in_specs=[pl.BlockSpec(memory_space=pl.ANY),
                  pl.BlockSpec((1, d, 2 * FFN_FC), wmap(2)),
                  pl.BlockSpec((1, 1, 2 * FFN_FC), wmap(2)),
                  pl.BlockSpec((1, FFN_FC, d), wmap(1)),
                  pl.BlockSpec((1, 1, d), lambda w, f, we_r, *_: (we_r[w], 0, 0))],
        out_specs=pl.BlockSpec(memory_space=pl.ANY),
        scratch_shapes=[pltpu.VMEM((FFN_R, d), BF16),
                        pltpu.VMEM((FFN_R, d), F32),
                        pltpu.VMEM((2, FFN_TM * PACK, LANES), U32),
                        pltpu.SemaphoreType.DMA(()),
                        pltpu.SemaphoreType.DMA((2,))],
    )
    return pl.pallas_call(
        _ffn_kernel,
        grid_spec=grid_spec,
        out_shape=jax.ShapeDtypeStruct((ns * PACK, LANES), U32),
        compiler_params=_cparams(("arbitrary", "arbitrary")),
        name="moe_ffn",
    )(we, row0, ntile, nvalid, xs, w1p, b1p, w2b, b2)


def _combine_kernel(cur_ref, nxt_ref, ys_hbm, gates_ref, x_ref, gmod_ref, fg_ref, o_ref, buf_ref, sem, *, final):
    i = pl.program_id(0)
    n = x_ref.shape[0]
    slot = i % 2

    def row_copy(idx_ref, b, k, r):
        src = ys_hbm.at[_prow(idx_ref[0, 0, k * n + r]), :]
        return pltpu.make_async_copy(src, buf_ref.at[b, k, _prow(r), :], sem.at[b])

    def issue(idx_ref, b):
        for k in range(TOP_K):
            def body(r, c):
                row_copy(idx_ref, b, k, r).start()
                return c
            lax.fori_loop(0, n, body, 0, unroll=8)

    @pl.when(i == 0)
    def _():
        issue(cur_ref, 0)

    @pl.when(i + 1 < pl.num_programs(0))
    def _():
        issue(nxt_ref, 1 - slot)

    for k in range(TOP_K):
        def wait(r, c):
            row_copy(cur_ref, slot, k, r).wait()
            return c
        lax.fori_loop(0, n, wait, 0, unroll=8)

    gates = gates_ref[...]
    gk = [gates[:, k:k + 1] for k in range(TOP_K)]
    half = D_MODEL // 2
    for s in range(PACK):
        lo = hi = None
        for k in range(TOP_K):
            lo_k, hi_k = _unpack_load(buf_ref, n, s, lead=(slot, k))
            lo = gk[k] * lo_k if lo is None else lo + gk[k] * lo_k
            hi = gk[k] * hi_k if hi is None else hi + gk[k] * hi_k
        for base, acc in ((s * LANES, lo), (half + s * LANES, hi)):
            cols = slice(base, base + LANES)
            o_ref[:, cols] = x_ref[:, cols] + gmod_ref[0, :, cols] * acc
    if final:
        o_ref[...] = _rms(o_ref[...], fg_ref[...])


def _combine(ys, dest, gates, x2, mod, final_g, seq, final):
    t, d = x2.shape
    tm = min(COMBINE_T, seq)
    tpb = seq // tm
    nt = t // tm
    idx = dest.reshape(nt, tm, TOP_K).transpose(0, 2, 1).reshape(nt, 1, TOP_K * tm)
    row = lambda w: pl.BlockSpec((tm, w), lambda i: (i, 0))
    return pl.pallas_call(
        functools.partial(_combine_kernel, final=final),
        grid=(nt,),
        in_specs=[pl.BlockSpec((1, 1, tm * TOP_K), lambda i: (i, 0, 0), memory_space=pltpu.SMEM),
                  pl.BlockSpec((1, 1, tm * TOP_K), lambda i: (jnp.minimum(i + 1, nt - 1), 0, 0),
                               memory_space=pltpu.SMEM),
                  pl.BlockSpec(memory_space=pl.ANY),
                  row(LANES), row(d), _mod_specs(tpb, 2),
                  pl.BlockSpec((1, d), lambda i: (0, 0))],
        out_specs=row(d),
        out_shape=jax.ShapeDtypeStruct((t, d), F32),
        scratch_shapes=[pltpu.VMEM((2, TOP_K, tm * PACK, LANES), U32), pltpu.SemaphoreType.DMA((2,))],
        compiler_params=_cparams(("arbitrary",)),
        name="moe_combine",
    )(idx, idx, ys, gates, x2, mod, final_g.reshape(1, -1))


def _moe_sublayer(x2, c_mod, norm_g, router_w, router_b, w1, b1, w2, b2, final_g, seq, final):
    t = x2.shape[0]
    h, route, gates, cnt = _route(x2, norm_g, c_mod, router_w, router_b, seq)
    top_e = route[:, :TOP_K]
    rank = route[:, TOP_K:2 * TOP_K]
    counts = cnt[0, :N_EXPERTS].astype(jnp.int32)
    padded = ((counts + FFN_TM - 1) // FFN_TM) * FFN_TM
    starts = jnp.cumsum(padded) - padded
    dest = starts[top_e] + rank
    ns = t * TOP_K + N_EXPERTS * FFN_TM
    ns = ((ns + GATHER_T - 1) // GATHER_T) * GATHER_T
    tok = jnp.broadcast_to(jnp.arange(t, dtype=jnp.int32)[:, None], (t, TOP_K))
    src_tok = jnp.zeros((ns,), jnp.int32).at[dest.reshape(-1)].set(tok.reshape(-1))
    n_items = (padded + FFN_R - 1) // FFN_R
    item_end = jnp.cumsum(n_items)
    wmax = (t * TOP_K) // FFN_R + N_EXPERTS
    wid = jnp.arange(wmax, dtype=jnp.int32)
    nvalid = item_end[-1].astype(jnp.int32)
    we = jnp.minimum(jnp.searchsorted(item_end, wid, side='right'), N_EXPERTS - 1).astype(jnp.int32)
    last_e = we[jnp.maximum(nvalid - 1, 0)]
    valid = wid < nvalid
    we = jnp.where(valid, we, last_e)
    local = wid - (item_end - n_items)[we]
    row0 = jnp.where(valid, starts[we] + local * FFN_R, 0).astype(jnp.int32)
    ntile = jnp.where(valid, jnp.minimum(FFN_R, padded[we] - local * FFN_R) // FFN_TM, 0).astype(jnp.int32)

    xs = _gather_rows(h, src_tok)
    hw = PERM_W // 2
    b1p = b1.reshape(N_EXPERTS, 2 * EXPERT_FF // PERM_W, hw, 2).transpose(0, 1, 3, 2).reshape(N_EXPERTS, 1, 2 * EXPERT_FF)
    ys = _ffn(xs, _w1_prep(w1), b1p, w2.astype(BF16), b2.reshape(N_EXPERTS, 1, D_MODEL),
              we, row0, ntile, jnp.stack([nvalid, jnp.sum(padded).astype(jnp.int32)]))
    return _combine(ys, dest.astype(jnp.int32), gates, x2, c_mod, final_g, seq, final)


def _conv_pw1_kernel(x_ref, g_ref, shift_ref, scale_ref, wa_ref, wb_ref, ba_ref, bb_ref, o_ref, h_ref):
    j = pl.program_id(1)

    @pl.when(j == 0)
    def _():
        h_ref[...] = _norm_mod(x_ref[...], g_ref[...], scale_ref[0], shift_ref[0]).astype(BF16)

    h = h_ref[...]
    a = _dot(h, wa_ref[...]) + ba_ref[...]
    b = _dot(h, wb_ref[...]) + bb_ref[...]
    o_ref[...] = a * jax.nn.sigmoid(b)


def _conv_pw1(x2, g, mod, w_pw1, b_pw1, seq):
    t = x2.shape[0]
    tm = min(PROJ_TM, seq)
    tpb = seq // tm
    tn = 512
    nj = D_MODEL // tn
    mod2 = lambda which: pl.BlockSpec((1, 1, D_MODEL), lambda i, j: (i // tpb, 0, which))
    b2 = b_pw1.reshape(1, -1)
    return pl.pallas_call(
        _conv_pw1_kernel,
        grid=(t // tm, nj),
        in_specs=[pl.BlockSpec((tm, D_MODEL), lambda i, j: (i, 0)),
                  pl.BlockSpec((1, D_MODEL), lambda i, j: (0, 0)),
                  mod2(0), mod2(1),
                  pl.BlockSpec((D_MODEL, tn), lambda i, j: (0, j)),
                  pl.BlockSpec((D_MODEL, tn), lambda i, j: (0, j + nj)),
                  pl.BlockSpec((1, tn), lambda i, j: (0, j)),
                  pl.BlockSpec((1, tn), lambda i, j: (0, j + nj))],
        out_specs=pl.BlockSpec((tm, tn), lambda i, j: (i, j)),
        out_shape=jax.ShapeDtypeStruct((t, D_MODEL), F32),
        scratch_shapes=[pltpu.VMEM((tm, D_MODEL), BF16)],
        compiler_params=_cparams(("arbitrary", "arbitrary")),
        name="conv_pw1",
    )(x2, g.reshape(1, -1), mod, mod, w_pw1, w_pw1, b2, b2)


def _conv_dw_kernel(y_ref, halo_ref, wdw_ref, bdw_ref, lng_ref, lnb_ref, wp_ref, bp_ref, x_ref, gate_ref,
                    o_ref, buf_ref, z_ref, *, tpb):
    i = pl.program_id(0)
    tm = y_ref.shape[0]
    first = (i % tpb) == 0
    halo = halo_ref[...]
    buf_ref[0:CONV_HALO, :] = jnp.where(first, 0.0, halo)
    buf_ref[CONV_HALO:, :] = y_ref[...]
    cw = 256
    rb = 64
    off = CONV_HALO - (CONV_KERNEL - 1)
    for c0 in range(0, D_MODEL, cw):
        for r0 in range(0, tm, rb):
            acc = jnp.zeros((rb, cw), F32)
            for k in range(CONV_KERNEL):
                acc = acc + wdw_ref[k:k + 1, c0:c0 + cw] * buf_ref[r0 + off + k:r0 + off + k + rb, c0:c0 + cw]
            z_ref[r0:r0 + rb, c0:c0 + cw] = acc
    z = z_ref[...] + bdw_ref[...]
    mu = jnp.mean(z, axis=-1, keepdims=True)
    zc = z - mu
    var = jnp.mean(zc * zc, axis=-1, keepdims=True)
    zn = zc * lax.rsqrt(var + LN_EPS) * lng_ref[...] + lnb_ref[...]
    act = (zn * jax.nn.sigmoid(zn)).astype(BF16)
    o_ref[...] = x_ref[...] + gate_ref[0] * (_dot(act, wp_ref[...]) + bp_ref[...])


def _conv_dw(y1, x2, mod, w_dw, b_dw, ln_g, ln_b, w_pw2, b_pw2, seq):
    t = x2.shape[0]
    tm = min(CONV_TM, seq)
    tpb = seq // tm
    hb = tm // CONV_HALO
    wdw = jnp.zeros((CONV_HALO, D_MODEL), F32).at[:CONV_KERNEL].set(w_dw)
    row = lambda: pl.BlockSpec((tm, D_MODEL), lambda i: (i, 0))
    vec = lambda: pl.BlockSpec((1, D_MODEL), lambda i: (0, 0))
    return pl.pallas_call(
        functools.partial(_conv_dw_kernel, tpb=tpb),
        grid=(t // tm,),
        in_specs=[row(),
                  pl.BlockSpec((CONV_HALO, D_MODEL), lambda i: (jnp.maximum(i * hb - 1, 0), 0)),
                  pl.BlockSpec((CONV_HALO, D_MODEL), lambda i: (0, 0)),
                  vec(), vec(), vec(),
                  pl.BlockSpec((D_MODEL, D_MODEL), lambda i: (0, 0)),
                  vec(), row(), _mod_specs(tpb, 2)],
        out_specs=row(),
        out_shape=jax.ShapeDtypeStruct((t, D_MODEL), F32),
        scratch_shapes=[pltpu.VMEM((tm + CONV_HALO, D_MODEL), F32), pltpu.VMEM((tm, D_MODEL), F32)],
        compiler_params=_cparams(("arbitrary",)),
        name="conv_dw_pw2",
    )(y1, y1, wdw, b_dw.reshape(1, -1), ln_g.reshape(1, -1), ln_b.reshape(1, -1),
      w_pw2, b_pw2.reshape(1, -1), x2, mod)


def _rope_swap_cols(w, heads):
    k = w.shape[0]
    half = MLA_ROPE // 2
    w3 = w.reshape(k, heads, MLA_NOPE + MLA_ROPE)
    pe = w3[:, :, MLA_NOPE:]
    sw = jnp.concatenate([pe[:, :, half:], pe[:, :, :half]], axis=-1)
    return jnp.concatenate([w3, sw], axis=-1).reshape(k, heads * (MLA_NOPE + 2 * MLA_ROPE))


def kernel(x, c, positions, l0_mix_norm_g, l0_mix_ada_w, l0_mix_ada_b, l0_w_in, l0_s5_lambda_re, l0_s5_lambda_im, l0_s5_log_step, l0_s5_b_re, l0_s5_b_im, l0_s5_c_re, l0_s5_c_im, l0_s5_d, l0_s5_w_glu, l0_s5_b_glu, l0_mla_q_norm_g, l0_mla_w_uq, l0_mla_kv_norm_g, l0_mla_w_ukv, l0_w_out, l0_moe_norm_g, l0_moe_ada_w, l0_moe_ada_b, l0_router_w, l0_router_b, l0_exp_w1, l0_exp_b1, l0_exp_w2, l0_exp_b2, l1_mix_norm_g, l1_mix_ada_w, l1_mix_ada_b, l1_conv_w_pw1, l1_conv_b_pw1, l1_conv_w_dw, l1_conv_b_dw, l1_conv_ln_g, l1_conv_ln_b, l1_conv_w_pw2, l1_conv_b_pw2, l1_moe_norm_g, l1_moe_ada_w, l1_moe_ada_b, l1_router_w, l1_router_b, l1_exp_w1, l1_exp_b1, l1_exp_w2, l1_exp_b2, final_norm_g):
    nbatch, seq, d = x.shape
    t = nbatch * seq
    x2 = x.reshape(t, d)
    pos = positions.reshape(t, 1).astype(jnp.int32)

    mod = _ada_mod(c, l0_mix_ada_w, l0_mix_ada_b)
    half = MLA_ROPE // 2
    kcol = S5_WIDTH + MLA_Q_RANK + MLA_KV_RANK
    w_in_ext = jnp.concatenate([l0_w_in, l0_w_in[:, kcol + half:], l0_w_in[:, kcol:kcol + half]], axis=1).astype(BF16)
    u, cq, ckv, kr = _proj_in(x2, l0_mix_norm_g, mod, w_in_ext, seq)

    kt, wt, vv, a16 = _s5_prep(l0_s5_lambda_re, l0_s5_lambda_im, l0_s5_log_step, l0_s5_b_re, l0_s5_b_im,
                               l0_s5_c_re, l0_s5_c_im, l0_s5_d)
    nchunk = seq // S5_CHUNK
    u_r = (u.reshape(nbatch, nchunk, S5_CHUNK, S5_GROUPS, S5_GROUP_CH)
           .transpose(3, 0, 1, 2, 4).reshape(S5_GROUPS, nbatch * nchunk, S5_ROW))
    y_r = _s5_main(u_r, kt, wt, vv, a16, nbatch, nchunk)
    ys = (y_r.reshape(S5_GROUPS, nbatch, nchunk, S5_CHUNK, S5_GROUP_CH)
          .transpose(1, 2, 3, 0, 4).reshape(t, S5_WIDTH))

    wq_ext = _rope_swap_cols(l0_mla_w_uq, MLA_HEADS).astype(BF16)
    q, k, v = _mla_proj(cq, ckv, kr, pos, l0_mla_q_norm_g, l0_mla_kv_norm_g, wq_ext,
                        l0_mla_w_ukv.astype(BF16), nbatch, seq)
    ym = _flash_attention(q, k, v).reshape(t, MLA_HEADS * MLA_V)
    x2 = _mixer_out(ys, ym, x2, mod, l0_s5_w_glu.astype(BF16), l0_s5_b_glu, l0_w_out.astype(BF16), seq)

    mod = _ada_mod(c, l0_moe_ada_w, l0_moe_ada_b)
    x2 = _moe_sublayer(x2, mod, l0_moe_norm_g, l0_router_w, l0_router_b, l0_exp_w1, l0_exp_b1,
                       l0_exp_w2, l0_exp_b2, final_norm_g, seq, final=False)

    mod = _ada_mod(c, l1_mix_ada_w, l1_mix_ada_b)
    y1 = _conv_pw1(x2, l1_mix_norm_g, mod, l1_conv_w_pw1.astype(BF16), l1_conv_b_pw1, seq)
    x2 = _conv_dw(y1, x2, mod, l1_conv_w_dw, l1_conv_b_dw, l1_conv_ln_g, l1_conv_ln_b,
                  l1_conv_w_pw2.astype(BF16), l1_conv_b_pw2, seq)

    mod = _ada_mod(c, l1_moe_ada_w, l1_moe_ada_b)
    x2 = _moe_sublayer(x2, mod, l1_moe_norm_g, l1_router_w, l1_router_b, l1_exp_w1, l1_exp_b1,
                       l1_exp_w2, l1_exp_b2, final_norm_g, seq, final=True)
    return x2.reshape(nbatch, seq, d)
```

```python
import functools
import math

import numpy as np
import jax
import jax.numpy as jnp
from jax import lax
from jax.experimental import pallas as pl
from jax.experimental.pallas import tpu as pltpu

F32 = jnp.float32
BF16 = jnp.bfloat16

D_MODEL = 2048
S5_WIDTH = 1024
S5_GROUP_CH = 16
S5_GROUPS = 64
S5_STATE = 64
MLA_HEADS = 8
MLA_NOPE = 128
MLA_ROPE = 64
MLA_V = 128
MLA_Q_RANK = 512
MLA_KV_RANK = 512
ROPE_THETA = 10000.0
CONV_KERNEL = 31
N_EXPERTS = 32
TOP_K = 4
EXPERT_FF = 2048
SWIGLU_ALPHA = 1.702
SWIGLU_LIMIT = 7.0
EPS = 1e-6
LN_EPS = 1e-5
NEG_BIG = -1e30

LANES = 128
SUBLANES = 8
VMEM_LIMIT_BYTES = 56 * 1024 * 1024

S5_CHUNK = 16
S5_ROW = S5_CHUNK * S5_GROUP_CH
PROJ_TM = 512
MLA_TM = 256
ATT_T = 512
OUT_TM = 256
ROUTE_TM = 512
CONV_TM = 256
CONV_HALO = 32
FFN_TM = 256
FFN_R = 2048
FFN_FC = 512
GATHER_T = 256
COMBINE_T = 128


def _cparams(sem):
    return pltpu.CompilerParams(dimension_semantics=sem, vmem_limit_bytes=VMEM_LIMIT_BYTES)


def _norm_mod(x, g, scale, shift):
    ms = jnp.mean(x * x, axis=-1, keepdims=True)
    return (x * lax.rsqrt(ms + EPS)) * g * (1.0 + scale) + shift


def _rms(x, g):
    ms = jnp.mean(x * x, axis=-1, keepdims=True)
    return (x * lax.rsqrt(ms + EPS)) * g


def _dot(a, b):
    return jnp.dot(a, b, preferred_element_type=F32)


def _dot_nt(a, b):
    return lax.dot_general(a, b, (((1,), (1,)), ((), ())), preferred_element_type=F32)


U32 = jnp.uint32
PACK = D_MODEL // (2 * LANES)
HI_MASK = 0xFFFF0000


def _pack_store(ref, val, lead=()):
    n = val.shape[0]
    half = D_MODEL // 2
    for s in range(PACK):
        lo = val[:, s * LANES:(s + 1) * LANES].astype(BF16).astype(F32)
        hi = val[:, half + s * LANES:half + (s + 1) * LANES].astype(BF16).astype(F32)
        word = (lax.bitcast_convert_type(lo, U32) >> 16) | (lax.bitcast_convert_type(hi, U32) & jnp.uint32(HI_MASK))
        ref[lead + (pl.ds(s, n, stride=PACK), slice(None))] = word


def _unpack_load(ref, n, s, lead=()):
    word = ref[lead + (pl.ds(s, n, stride=PACK), slice(None))]
    lo = lax.bitcast_convert_type(word << 16, F32)
    hi = lax.bitcast_convert_type(word & jnp.uint32(HI_MASK), F32)
    return lo, hi


def _prow(r, n=1):
    return pl.ds(pl.multiple_of(r * PACK, PACK), n * PACK)


ADA_TN = 768
ADA_KC = 256


def _ada_kernel(ct_ref, w_ref, b_ref, o_ref):
    nb = ct_ref.shape[0]
    kdim = w_ref.shape[0]
    rows = []
    for b in range(nb):
        acc = jnp.zeros((1, w_ref.shape[1]), F32)
        for k0 in range(0, kdim, ADA_KC):
            c = ct_ref[b, k0:k0 + ADA_KC, :]
            cs = c * jax.nn.sigmoid(c)
            acc = acc + jnp.sum(w_ref[k0:k0 + ADA_KC, :] * cs, axis=0, keepdims=True)
        rows.append(acc)
    o_ref[...] = jnp.concatenate(rows, axis=0) + b_ref[...]


def _ada_mod(c, w, b):
    nb, d = c.shape
    n = w.shape[1]
    ct = c.reshape(nb, d, 1)
    m = pl.pallas_call(
        _ada_kernel,
        grid=(n // ADA_TN,),
        in_specs=[pl.BlockSpec((nb, d, 1), lambda j: (0, 0, 0)),
                  pl.BlockSpec((d, ADA_TN), lambda j: (0, j)),
                  pl.BlockSpec((1, ADA_TN), lambda j: (0, j))],
        out_specs=pl.BlockSpec((nb, ADA_TN), lambda j: (0, j)),
        out_shape=jax.ShapeDtypeStruct((nb, n), F32),
        compiler_params=_cparams(("arbitrary",)),
        name="ada_mod",
    )(ct, w, b.reshape(1, n))
    return m.reshape(nb, 1, n)


def _mod_specs(tiles_per_batch, which):
    return pl.BlockSpec((1, 1, D_MODEL), lambda i, *_: (i // tiles_per_batch, 0, which))


def _proj_in_kernel(x_ref, g_ref, shift_ref, scale_ref, w_ref, u_ref, cq_ref, ckv_ref, kr_ref):
    h = _norm_mod(x_ref[...], g_ref[...], scale_ref[0], shift_ref[0]).astype(BF16)
    acc = _dot(h, w_ref[...])
    c0, c1, c2 = S5_WIDTH, S5_WIDTH + MLA_Q_RANK, S5_WIDTH + MLA_Q_RANK + MLA_KV_RANK
    u_ref[...] = acc[:, :c0].astype(BF16)
    cq_ref[...] = acc[:, c0:c1]
    ckv_ref[...] = acc[:, c1:c2]
    kr_ref[...] = acc[:, c2:]


def _proj_in(x2, g, mod, w_ext, seq):
    t = x2.shape[0]
    tm = min(PROJ_TM, seq)
    tpb = seq // tm
    n = w_ext.shape[1]
    row = lambda w: pl.BlockSpec((tm, w), lambda i: (i, 0))
    return pl.pallas_call(
        _proj_in_kernel,
        grid=(t // tm,),
        in_specs=[row(D_MODEL),
                  pl.BlockSpec((1, D_MODEL), lambda i: (0, 0)),
                  _mod_specs(tpb, 0), _mod_specs(tpb, 1),
                  pl.BlockSpec((D_MODEL, n), lambda i: (0, 0))],
        out_specs=[row(S5_WIDTH), row(MLA_Q_RANK), row(MLA_KV_RANK), row(2 * MLA_ROPE)],
        out_shape=[jax.ShapeDtypeStruct((t, S5_WIDTH), BF16),
                   jax.ShapeDtypeStruct((t, MLA_Q_RANK), F32),
                   jax.ShapeDtypeStruct((t, MLA_KV_RANK), F32),
                   jax.ShapeDtypeStruct((t, 2 * MLA_ROPE), F32)],
        compiler_params=_cparams(("arbitrary",)),
        name="proj_in",
    )(x2, g.reshape(1, -1), mod, mod, w_ext)


def _s5_prep_kernel(lamc_ref, lamr_ref, step_ref, bt_ref, btt_ref, ct_ref, d_ref,
                    kt_ref, wt_ref, v_ref, a16_ref):
    P, H, C = S5_STATE, S5_GROUP_CH, S5_CHUNK
    step = step_ref[0]
    step = jnp.exp(step)
    lr_c = jnp.minimum(lamc_ref[0, 0], -1e-4)
    li_c = lamc_ref[0, 1]
    lr_r = jnp.minimum(lamr_ref[0, 0:1, :], -1e-4)
    li_r = lamr_ref[0, 1:2, :]

    def ratio(lr, li):
        mag = jnp.exp(lr * step)
        ab_re = mag * jnp.cos(li * step)
        ab_im = mag * jnp.sin(li * step)
        denom = lr * lr + li * li
        nr, ni = ab_re - 1.0, ab_im
        return (nr * lr + ni * li) / denom, (ni * lr - nr * li) / denom

    rr_c, ri_c = ratio(lr_c, li_c)
    rr_r, ri_r = ratio(lr_r, li_r)

    lane = lax.broadcasted_iota(jnp.int32, (1, S5_ROW), 1)
    kk = (lane // H).astype(F32)

    def powers(k):
        mag = jnp.exp(lr_c * step * k)
        return mag * jnp.cos(li_c * step * k), mag * jnp.sin(li_c * step * k)

    bre_t, bim_t = bt_ref[0, 0], bt_ref[0, 1]
    bbt_re = rr_c * bre_t - ri_c * bim_t
    bbt_im = rr_c * bim_t + ri_c * bre_t
    cre_t, cim_t = ct_ref[0, 0], ct_ref[0, 1]

    er, ei = powers(float(C - 1) - kk)
    wt_ref[0, 0:P, :] = (er * bbt_re - ei * bbt_im).astype(BF16)
    wt_ref[0, P:2 * P, :] = (er * bbt_im + ei * bbt_re).astype(BF16)

    er, ei = powers(kk + 1.0)
    v_ref[0, 0:P, :] = (cre_t * er - cim_t * ei).astype(BF16)
    v_ref[0, P:2 * P, :] = (-cre_t * ei - cim_t * er).astype(BF16)

    er, ei = powers(kk)
    q_re = er * cre_t - ei * cim_t
    q_im = er * cim_t + ei * cre_t
    brt, bit = btt_ref[0, 0], btt_ref[0, 1]
    bbr = rr_r * brt - ri_r * bit
    bbi = rr_r * bit + ri_r * brt
    hi = lax.Precision.HIGHEST
    mall = (jnp.dot(bbr, q_re, precision=hi, preferred_element_type=F32)
            - jnp.dot(bbi, q_im, precision=hi, preferred_element_type=F32))
    rowh = lax.broadcasted_iota(jnp.int32, (H, S5_ROW), 0)
    laneh = lax.broadcasted_iota(jnp.int32, (H, S5_ROW), 1)
    mall = mall + jnp.where(laneh == rowh, d_ref[0], 0.0)
    for s in range(C):
        piece = mall if s == 0 else pltpu.roll(mall, H * s, 1)
        piece = jnp.where(laneh >= H * s, piece, 0.0)
        kt_ref[0, s * H:(s + 1) * H, :] = piece.astype(BF16)

    mag = jnp.exp(lr_r * step * float(C))
    a16_ref[0, 0:1, :] = mag * jnp.cos(li_r * step * float(C))
    a16_ref[0, 1:2, :] = mag * jnp.sin(li_r * step * float(C))


def _s5_prep(lam_re, lam_im, log_step, b_re, b_im, c_re, c_im, d):
    G, P, H, C = S5_GROUPS, S5_STATE, S5_GROUP_CH, S5_CHUNK
    lam = jnp.stack([lam_re, lam_im], axis=1).astype(F32)
    lamc = lam.reshape(G, 2, P, 1)
    b = jnp.stack([b_re, b_im], axis=1).astype(F32)
    bt = jnp.tile(b, (1, 1, 1, C))
    btt = jnp.swapaxes(b, 2, 3)
    c = jnp.stack([c_re, c_im], axis=1).astype(F32)
    ct = jnp.tile(jnp.swapaxes(c, 2, 3), (1, 1, 1, C))
    dt = jnp.tile(d.astype(F32), (1, C)).reshape(G, 1, S5_ROW)
    step = log_step.astype(F32).reshape(G, 1, 1)
    g4 = lambda *shape: pl.BlockSpec((1,) + shape, lambda g: (g,) + (0,) * len(shape))
    return pl.pallas_call(
        _s5_prep_kernel,
        grid=(G,),
        in_specs=[g4(2, P, 1), g4(2, P), g4(1, 1), g4(2, P, S5_ROW), g4(2, H, P),
                  g4(2, P, S5_ROW), g4(1, S5_ROW)],
        out_specs=[g4(S5_ROW, S5_ROW), g4(2 * P, S5_ROW), g4(2 * P, S5_ROW), g4(2, P)],
        out_shape=[jax.ShapeDtypeStruct((G, S5_ROW, S5_ROW), BF16),
                   jax.ShapeDtypeStruct((G, 2 * P, S5_ROW), BF16),
                   jax.ShapeDtypeStruct((G, 2 * P, S5_ROW), BF16),
                   jax.ShapeDtypeStruct((G, 2, P), F32)],
        compiler_params=_cparams(("arbitrary",)),
        name="s5_prep",
    )(lamc, lam, step, bt, btt, ct, dt)


def _s5_main_kernel(u_ref, kt_ref, wt_ref, v_ref, a16_ref, y_ref, sr_ref, si_ref, xr_ref, xi_ref,
                    *, nbatch, nchunk):
    P = S5_STATE
    u = u_ref[0]
    sr_ref[...] = _dot_nt(u, wt_ref[0, 0:P, :])
    si_ref[...] = _dot_nt(u, wt_ref[0, P:2 * P, :])
    ar = a16_ref[0, 0:1, :]
    ai = a16_ref[0, 1:2, :]

    def step(c, carry):
        out = []
        for b in range(nbatch):
            xr, xi = carry[2 * b], carry[2 * b + 1]
            r = b * nchunk + c
            xr_ref[pl.ds(r, 1), :] = xr
            xi_ref[pl.ds(r, 1), :] = xi
            nxr = ar * xr - ai * xi + sr_ref[pl.ds(r, 1), :]
            nxi = ar * xi + ai * xr + si_ref[pl.ds(r, 1), :]
            out += [nxr, nxi]
        return tuple(out)

    zero = jnp.zeros((1, P), F32)
    lax.fori_loop(0, nchunk, step, (zero,) * (2 * nbatch))
    y = _dot(u, kt_ref[0])
    y = y + _dot(xr_ref[...].astype(BF16), v_ref[0, 0:P, :])
    y = y + _dot(xi_ref[...].astype(BF16), v_ref[0, P:2 * P, :])
    y_ref[0] = y.astype(y_ref.dtype)


def _s5_main(u_r, kt, wt, v, a16, nbatch, nchunk):
    G, rows, _ = u_r.shape
    P = S5_STATE
    g3 = lambda a, b: pl.BlockSpec((1, a, b), lambda g: (g, 0, 0))
    return pl.pallas_call(
        functools.partial(_s5_main_kernel, nbatch=nbatch, nchunk=nchunk),
        grid=(G,),
        in_specs=[g3(rows, S5_ROW), g3(S5_ROW, S5_ROW), g3(2 * P, S5_ROW), g3(2 * P, S5_ROW), g3(2, P)],
        out_specs=g3(rows, S5_ROW),
        out_shape=jax.ShapeDtypeStruct((G, rows, S5_ROW), BF16),
        scratch_shapes=[pltpu.VMEM((rows, P), F32) for _ in range(4)],
        compiler_params=_cparams(("arbitrary",)),
        name="s5_main",
    )(u_r, kt, wt, v, a16)


def _mla_proj_kernel(cq_ref, ckv_ref, kr_ref, pos_ref, qg_ref, kvg_ref, wq_ref, wkv_ref,
                     invf_ref, sgn_ref, q_ref, k_ref, v_ref):
    qscale = 1.0 / math.sqrt(MLA_NOPE + MLA_ROPE)
    qa = _dot(_rms(cq_ref[...], qg_ref[...]).astype(BF16), wq_ref[...])
    kva = _dot(_rms(ckv_ref[...], kvg_ref[...]).astype(BF16), wkv_ref[...])
    ang = pos_ref[...].astype(F32) * invf_ref[...]
    cc = jnp.cos(ang)
    ss = jnp.sin(ang) * sgn_ref[...]

    def rope(slab):
        return slab * cc + pltpu.roll(slab, MLA_ROPE, 1) * ss

    kpe = rope(kr_ref[...])[:, :MLA_ROPE].astype(BF16)
    hw = MLA_NOPE + 2 * MLA_ROPE
    lane = lax.broadcasted_iota(jnp.int32, (cq_ref.shape[0], MLA_V), 1)
    ones_col = jnp.where(lane == 0, 1.0, 0.0).astype(BF16)
    for h in range(MLA_HEADS):
        blk = qa[:, h * hw:(h + 1) * hw]
        q_ref[0, h, :, 0:MLA_NOPE] = (blk[:, :MLA_NOPE] * qscale).astype(BF16)
        qpe = rope(blk[:, MLA_NOPE:]) * qscale
        q_ref[0, h, :, MLA_NOPE:MLA_NOPE + MLA_ROPE] = qpe[:, :MLA_ROPE].astype(BF16)
        kvb = kva[:, h * hw:(h + 1) * hw]
        k_ref[0, h, :, 0:MLA_NOPE] = kvb[:, :MLA_NOPE].astype(BF16)
        k_ref[0, h, :, MLA_NOPE:MLA_NOPE + MLA_ROPE] = kpe
        v_ref[0, h, :, 0:MLA_V] = kvb[:, MLA_NOPE:].astype(BF16)
        v_ref[0, h, :, MLA_V:2 * MLA_V] = ones_col


def _mla_proj(cq, ckv, kr, pos, qg, kvg, wq_ext, wkv, nbatch, seq):
    tm = min(MLA_TM, seq)
    nl = seq // tm
    dqk = MLA_NOPE + MLA_ROPE
    half = MLA_ROPE // 2
    inv_freq = 1.0 / (ROPE_THETA ** (jnp.arange(0, MLA_ROPE, 2, dtype=F32) / MLA_ROPE))
    invf = jnp.tile(inv_freq, 4).reshape(1, 2 * MLA_ROPE)
    sgn = jnp.tile(jnp.concatenate([-jnp.ones((half,), F32), jnp.ones((half,), F32)]), 2).reshape(1, 2 * MLA_ROPE)
    row = lambda w: pl.BlockSpec((tm, w), lambda b, i: (b * nl + i, 0))
    full = lambda a, b_: pl.BlockSpec((a, b_), lambda b, i: (0, 0))
    head = lambda w: pl.BlockSpec((1, MLA_HEADS, tm, w), lambda b, i: (b, 0, i, 0))
    return pl.pallas_call(
        _mla_proj_kernel,
        grid=(nbatch, nl),
        in_specs=[row(MLA_Q_RANK), row(MLA_KV_RANK), row(2 * MLA_ROPE), row(1),
                  full(1, MLA_Q_RANK), full(1, MLA_KV_RANK),
                  full(MLA_Q_RANK, wq_ext.shape[1]), full(MLA_KV_RANK, wkv.shape[1]),
                  full(1, 2 * MLA_ROPE), full(1, 2 * MLA_ROPE)],
        out_specs=[head(dqk), head(dqk), head(2 * MLA_V)],
        out_shape=[jax.ShapeDtypeStruct((nbatch, MLA_HEADS, seq, dqk), BF16),
                   jax.ShapeDtypeStruct((nbatch, MLA_HEADS, seq, dqk), BF16),
                   jax.ShapeDtypeStruct((nbatch, MLA_HEADS, seq, 2 * MLA_V), BF16)],
        compiler_params=_cparams(("arbitrary", "arbitrary")),
        name="mla_proj",
    )(cq, ckv, kr, pos, qg.reshape(1, -1), kvg.reshape(1, -1), wq_ext, wkv, invf, sgn)


def _lane_groups(x):
    return [x[:, j * LANES:(j + 1) * LANES] for j in range(x.shape[1] // LANES)]


def _flash_kernel(q_ref, k_ref, v_ref, o_ref, s_ref, *, t):
    qi = pl.program_id(2)
    qh = [q_ref[0, 0, 0:t, :], q_ref[0, 0, t:2 * t, :]]

    def kblk(c):
        return k_ref[0, 0, pl.ds(pl.multiple_of(c * t, t), t), :]

    def vblk(c):
        return v_ref[0, 0, pl.ds(pl.multiple_of(c * t, t), t), :]

    def fold_max(s, mx):
        return functools.reduce(jnp.maximum, _lane_groups(s), mx)

    def pass1(c, mx):
        kc = kblk(c)
        out = []
        for h in range(2):
            s = _dot_nt(qh[h], kc)
            s_ref[h, c] = s
            out.append(fold_max(s, mx[h]))
        return tuple(out)

    neg = jnp.full((t, LANES), NEG_BIG, F32)
    mx0, mx1 = lax.fori_loop(0, 2 * qi, pass1, (neg, neg))
    c0, c1 = 2 * qi, 2 * qi + 1
    row = lax.broadcasted_iota(jnp.int32, (t, t), 0)
    col = lax.broadcasted_iota(jnp.int32, (t, t), 1)
    tri = col <= row
    k0, k1 = kblk(c0), kblk(c1)
    s00 = jnp.where(tri, _dot_nt(qh[0], k0), NEG_BIG)
    s10 = _dot_nt(qh[1], k0)
    s11 = jnp.where(tri, _dot_nt(qh[1], k1), NEG_BIG)
    s_ref[0, c0] = s00
    s_ref[1, c0] = s10
    s_ref[1, c1] = s11
    mx0 = fold_max(s00, mx0)
    mx1 = fold_max(s11, fold_max(s10, mx1))
    m = [jnp.broadcast_to(jnp.max(mx, axis=-1, keepdims=True), (t, LANES)) for mx in (mx0, mx1)]

    mt = [jnp.concatenate([mh] * (t // LANES), axis=1) for mh in m]

    def probs(h, c):
        return jnp.exp(s_ref[h, c] - mt[h]).astype(BF16)

    def pass2(j, carry):
        c = 2 * j
        vc = v_ref[0, 0, pl.ds(pl.multiple_of(c * t, 2 * t), 2 * t), :]
        out = []
        for h in range(2):
            p = jnp.concatenate([probs(h, c), probs(h, c + 1)], axis=1)
            out.append(carry[h] + _dot(p, vc))
        return tuple(out)

    za = jnp.zeros((t, 2 * MLA_V), F32)
    a0, a1 = lax.fori_loop(0, qi, pass2, (za, za))
    v0, v1 = vblk(c0), vblk(c1)
    a0 = a0 + _dot(probs(0, c0), v0)
    a1 = a1 + _dot(jnp.concatenate([probs(1, c0), probs(1, c1)], axis=1), jnp.concatenate([v0, v1], axis=0))
    o_ref[0, 0:t, :] = (a0[:, :MLA_V] / a0[:, MLA_V:MLA_V + 1]).astype(o_ref.dtype)
    o_ref[0, t:2 * t, :] = (a1[:, :MLA_V] / a1[:, MLA_V:MLA_V + 1]).astype(o_ref.dtype)


def _flash_attention(q, k, v):
    nbatch, nh, seq, dqk = q.shape
    t = min(ATT_T, seq // 2)
    nq = seq // (2 * t)
    return pl.pallas_call(
        functools.partial(_flash_kernel, t=t),
        grid=(nbatch, nh, nq),
        in_specs=[pl.BlockSpec((1, 1, 2 * t, dqk), lambda b, h, i: (b, h, i, 0)),
                  pl.BlockSpec((1, 1, seq, dqk), lambda b, h, i: (b, h, 0, 0)),
                  pl.BlockSpec((1, 1, seq, 2 * MLA_V), lambda b, h, i: (b, h, 0, 0))],
        out_specs=pl.BlockSpec((1, 2 * t, MLA_V), lambda b, h, i: (b, i, h)),
        out_shape=jax.ShapeDtypeStruct((nbatch, seq, nh * MLA_V), BF16),
        scratch_shapes=[pltpu.VMEM((2, seq // t, t, t), F32)],
        compiler_params=_cparams(("arbitrary", "arbitrary", "arbitrary")),
        name="flash_attn",
    )(q, k, v)


def _gelu_tanh(x):
    c = math.sqrt(2.0 / math.pi)
    return 0.5 * x * (1.0 + jnp.tanh(c * (x + 0.044715 * (x * x * x))))


def _mixer_out_kernel(ys_ref, ym_ref, x_ref, gate_ref, wglu_ref, bglu_ref, wo_ref, o_ref):
    y = _gelu_tanh(ys_ref[...].astype(F32))
    g = _dot(y.astype(BF16), wglu_ref[...]) + bglu_ref[...]
    s5o = (y * jax.nn.sigmoid(g)).astype(BF16)
    acc = _dot(s5o, wo_ref[0:S5_WIDTH, :]) + _dot(ym_ref[...], wo_ref[S5_WIDTH:, :])
    o_ref[...] = x_ref[...] + gate_ref[0] * acc


def _mixer_out(ys, ym, x2, mod, w_glu, b_glu, w_out, seq):
    t = x2.shape[0]
    tm = min(OUT_TM, seq)
    tpb = seq // tm
    row = lambda w: pl.BlockSpec((tm, w), lambda i: (i, 0))
    full = lambda a, b: pl.BlockSpec((a, b), lambda i: (0, 0))
    return pl.pallas_call(
        _mixer_out_kernel,
        grid=(t // tm,),
        in_specs=[row(S5_WIDTH), row(MLA_HEADS * MLA_V), row(D_MODEL), _mod_specs(tpb, 2),
                  full(S5_WIDTH, S5_WIDTH), full(1, S5_WIDTH), full(2 * S5_WIDTH, D_MODEL)],
        out_specs=row(D_MODEL),
        out_shape=jax.ShapeDtypeStruct((t, D_MODEL), F32),
        compiler_params=_cparams(("arbitrary",)),
        name="mixer_out",
    )(ys, ym, x2, mod, w_glu, b_glu.reshape(1, -1), w_out)


def _route_kernel(x_ref, g_ref, shift_ref, scale_ref, rw_ref, rb_ref, tri_ref,
                  h_ref, route_ref, gates_ref, cnt_ref, carry_ref):
    i = pl.program_id(0)

    @pl.when(i == 0)
    def _():
        carry_ref[...] = jnp.zeros(carry_ref.shape, F32)

    h = _norm_mod(x_ref[...], g_ref[...], scale_ref[0], shift_ref[0])
    _pack_store(h_ref, h)
    logits = jnp.dot(h, rw_ref[...], precision=lax.Precision.HIGHEST,
                     preferred_element_type=F32) + rb_ref[...]
    lane = lax.broadcasted_iota(jnp.int32, logits.shape, 1)
    lanef = lane.astype(F32)
    lg = jnp.where(lane < N_EXPERTS, logits, -jnp.inf)
    vals, hots, idxs = [], [], []
    sel = jnp.zeros(logits.shape, F32)
    for _ in range(TOP_K):
        m = jnp.max(lg, axis=-1, keepdims=True)
        idx = jnp.min(jnp.where(lg == m, lanef, float(LANES)), axis=-1, keepdims=True)
        hot = lanef == idx
        vals.append(m)
        idxs.append(idx)
        hots.append(hot)
        sel = jnp.where(hot, 1.0, sel)
        lg = jnp.where(hot, -jnp.inf, lg)
    es = [jnp.exp(v - vals[0]) for v in vals]
    denom = es[0] + es[1] + es[2] + es[3]
    before = _dot(tri_ref[...], sel.astype(BF16)) + carry_ref[...]
    route = jnp.zeros(logits.shape, F32)
    gates = jnp.zeros(logits.shape, F32)
    for k in range(TOP_K):
        rank = jnp.sum(jnp.where(hots[k], before, 0.0), axis=-1, keepdims=True)
        route = jnp.where(lane == k, idxs[k], route)
        route = jnp.where(lane == TOP_K + k, rank, route)
        gates = jnp.where(lane == k, es[k] / denom, gates)
    route_ref[...] = route.astype(jnp.int32)
    gates_ref[...] = gates
    carry_ref[...] = carry_ref[...] + jnp.sum(sel, axis=0, keepdims=True)
    cnt_ref[...] = carry_ref[...]


def _route(x2, g, mod, router_w, router_b, seq):
    t = x2.shape[0]
    tm = min(ROUTE_TM, seq)
    tpb = seq // tm
    rw = jnp.zeros((D_MODEL, LANES), F32).at[:, :N_EXPERTS].set(router_w)
    rb = jnp.zeros((1, LANES), F32).at[0, :N_EXPERTS].set(router_b)
    tri = jnp.asarray(np.tril(np.ones((tm, tm), np.float32), -1), BF16)
    row = lambda w: pl.BlockSpec((tm, w), lambda i: (i, 0))
    full = lambda a, b: pl.BlockSpec((a, b), lambda i: (0, 0))
    return pl.pallas_call(
        _route_kernel,
        grid=(t // tm,),
        in_specs=[row(D_MODEL), full(1, D_MODEL), _mod_specs(tpb, 0), _mod_specs(tpb, 1),
                  full(D_MODEL, LANES), full(1, LANES), full(tm, tm)],
        out_specs=[pl.BlockSpec((tm * PACK, LANES), lambda i: (i, 0)), row(LANES), row(LANES), full(1, LANES)],
        out_shape=[jax.ShapeDtypeStruct((t * PACK, LANES), U32),
                   jax.ShapeDtypeStruct((t, LANES), jnp.int32),
                   jax.ShapeDtypeStruct((t, LANES), F32),
                   jax.ShapeDtypeStruct((1, LANES), F32)],
        scratch_shapes=[pltpu.VMEM((1, LANES), F32)],
        compiler_params=_cparams(("arbitrary",)),
        name="moe_route",
    )(x2, g.reshape(1, -1), mod, mod, rw, rb, tri)


def _gather_kernel(cur_ref, nxt_ref, h_hbm, o_ref, buf_ref, sem):
    i = pl.program_id(0)
    n = GATHER_T
    slot = i % 2

    def row_copy(idx_ref, b, r):
        return pltpu.make_async_copy(h_hbm.at[_prow(idx_ref[0, 0, r]), :], buf_ref.at[b, _prow(r), :], sem.at[b])

    def issue(idx_ref, b):
        def body(j, c):
            for par in range(2):
                row_copy(idx_ref, b, 2 * j + par).start(priority=par)
            return c
        lax.fori_loop(0, n // 2, body, 0, unroll=4)

    @pl.when(i == 0)
    def _():
        issue(cur_ref, 0)

    @pl.when(i + 1 < pl.num_programs(0))
    def _():
        issue(nxt_ref, 1 - slot)

    def wait(r, c):
        row_copy(cur_ref, slot, r).wait()
        return c

    lax.fori_loop(0, n, wait, 0, unroll=8)
    half = D_MODEL // 2
    for s in range(PACK):
        lo, hi = _unpack_load(buf_ref, n, s, lead=(slot,))
        o_ref[:, s * LANES:(s + 1) * LANES] = lo.astype(BF16)
        o_ref[:, half + s * LANES:half + (s + 1) * LANES] = hi.astype(BF16)


def _gather_rows(h_slabs, src_tok):
    ns = src_tok.shape[0]
    nt = ns // GATHER_T
    idx = src_tok.reshape(nt, 1, GATHER_T)
    return pl.pallas_call(
        _gather_kernel,
        grid=(nt,),
        in_specs=[pl.BlockSpec((1, 1, GATHER_T), lambda i: (i, 0, 0), memory_space=pltpu.SMEM),
                  pl.BlockSpec((1, 1, GATHER_T), lambda i: (jnp.minimum(i + 1, nt - 1), 0, 0),
                               memory_space=pltpu.SMEM),
                  pl.BlockSpec(memory_space=pl.ANY)],
        out_specs=pl.BlockSpec((GATHER_T, D_MODEL), lambda i: (i, 0)),
        out_shape=jax.ShapeDtypeStruct((ns, D_MODEL), BF16),
        scratch_shapes=[pltpu.VMEM((2, GATHER_T * PACK, LANES), U32), pltpu.SemaphoreType.DMA((2,))],
        compiler_params=_cparams(("arbitrary",)),
        name="moe_gather",
    )(idx, idx, h_slabs)


PERM_W = 256
W1_BLK = 1024


def _w1_prep_kernel(w_ref, p_ref, o_ref):
    for b in range(W1_BLK // PERM_W):
        cols = slice(b * PERM_W, (b + 1) * PERM_W)
        o_ref[0, :, cols] = _dot(w_ref[0, :, cols].astype(BF16), p_ref[...]).astype(BF16)


def _w1_prep(w1):
    e, d, n = w1.shape
    pm = np.zeros((PERM_W, PERM_W), np.float32)
    jj = np.arange(PERM_W // 2)
    pm[2 * jj, jj] = 1.0
    pm[2 * jj + 1, PERM_W // 2 + jj] = 1.0
    return pl.pallas_call(
        _w1_prep_kernel,
        grid=(e, n // W1_BLK),
        in_specs=[pl.BlockSpec((1, d, W1_BLK), lambda i, j: (i, 0, j)),
                  pl.BlockSpec((PERM_W, PERM_W), lambda i, j: (0, 0))],
        out_specs=pl.BlockSpec((1, d, W1_BLK), lambda i, j: (i, 0, j)),
        out_shape=jax.ShapeDtypeStruct((e, d, n), BF16),
        compiler_params=_cparams(("arbitrary", "arbitrary")),
        name="moe_w1_prep",
    )(w1, jnp.asarray(pm, BF16))


def _ffn_kernel(we_ref, row0_ref, nt_ref, nv_ref, xs_hbm, w1_ref, b1_ref, w2_ref, b2_ref,
                ys_hbm, x_ref, acc_ref, w2b_ref, stage_ref, sem_in, sem_out):
    w = pl.program_id(0)
    f = pl.program_id(1)
    nf = pl.num_programs(1)
    nt = nt_ref[w]
    row0 = row0_ref[w]
    tm = FFN_TM
    hw = PERM_W // 2

    @pl.when(jnp.logical_and(f == 0, nt > 0))
    def _():
        def tile_copy(t):
            return pltpu.make_async_copy(xs_hbm.at[pl.ds(pl.multiple_of(row0 + t * tm, tm), tm), :],
                                         x_ref.at[pl.ds(pl.multiple_of(t * tm, tm), tm), :], sem_in)

        def start(t, c):
            tile_copy(t).start()
            return c

        def wait(t, c):
            tile_copy(t).wait()
            return c

        lax.fori_loop(0, nt, start, 0)
        lax.fori_loop(0, nt, wait, 0)

    def out_copy(t, slot):
        return pltpu.make_async_copy(stage_ref.at[slot], ys_hbm.at[_prow(row0 + t * tm, tm), :], sem_out.at[slot])

    @pl.when(nt > 0)
    def _():
        w2b_ref[...] = w2_ref[0].astype(BF16)

        @pl.when(f == 0)
        def _():
            def init(t, c):
                acc_ref[pl.ds(pl.multiple_of(t * tm, tm), tm), :] = jnp.broadcast_to(b2_ref[0], (tm, D_MODEL))
                return c
            lax.fori_loop(0, nt, init, 0)

        def tile(xv, av):
            a = _dot(xv[...], w1_ref[0]) + b1_ref[0]
            nblk = a.shape[1] // PERM_W
            glu = jnp.concatenate([a[:, b * PERM_W:b * PERM_W + hw] for b in range(nblk)], axis=1)
            lin = jnp.concatenate([a[:, b * PERM_W + hw:(b + 1) * PERM_W] for b in range(nblk)], axis=1)
            glu = jnp.minimum(glu, SWIGLU_LIMIT)
            lin = jnp.clip(lin, -SWIGLU_LIMIT, SWIGLU_LIMIT)
            act = glu * jax.nn.sigmoid(SWIGLU_ALPHA * glu) * (lin + 1.0)
            av[...] = av[...] + _dot(act.astype(BF16), w2b_ref[...])

        def pair(p, c):
            win = pl.ds(pl.multiple_of(p * 2 * tm, 2 * tm), 2 * tm)
            xw, aw = x_ref.at[win, :], acc_ref.at[win, :]
            for h in range(2):
                tile(xw.at[h * tm:(h + 1) * tm, :], aw.at[h * tm:(h + 1) * tm, :])
            return c

        lax.fori_loop(0, nt // 2, pair, 0)

        @pl.when(nt % 2 == 1)
        def _():
            last = pl.ds(pl.multiple_of((nt - 1) * tm, tm), tm)
            tile(x_ref.at[last, :], acc_ref.at[last, :])

        @pl.when(f == nf - 1)
        def _():
            def emit(t, c):
                slot = t % 2

                @pl.when(t >= 2)
                def _():
                    out_copy(t - 2, slot).wait()

                _pack_store(stage_ref, acc_ref[pl.ds(pl.multiple_of(t * tm, tm), tm), :], lead=(slot,))
                out_copy(t, slot).start()
                return c

            lax.fori_loop(0, nt, emit, 0)
            for back in (2, 1):
                @pl.when(nt >= back)
                def _():
                    out_copy(nt - back, (nt - back) % 2).wait()

    @pl.when(jnp.logical_and(w == pl.num_programs(0) - 1, f == nf - 1))
    def _():
        used = nv_ref[1]
        ntail = (ys_hbm.shape[0] // PACK - used) // tm
        stage_ref[0] = jnp.zeros(stage_ref.shape[1:], U32)

        def fill(t, c):
            cp = pltpu.make_async_copy(stage_ref.at[0], ys_hbm.at[_prow(used + t * tm, tm), :], sem_out.at[0])
            cp.start()
            cp.wait()
            return c

        lax.fori_loop(0, ntail, fill, 0)


def _ffn(xs, w1p, b1p, w2b, b2, we, row0, ntile, nvalid):
    ns, d = xs.shape
    nf = EXPERT_FF // FFN_FC
    wmax = we.shape[0]

    def wmap(axis):
        def index(w, f, we_r, row0_r, nt_r, nv_r):
            fe = jnp.where(w < nv_r[0], f, nf - 1)
            return (we_r[w], 0, fe) if axis == 2 else (we_r[w], fe, 0)
        return index

    grid_spec = pltpu.PrefetchScalarGridSpec(
        num_scalar_prefetch=4,
        grid=(wmax, nf),
        in_specs=[pl.BlockSpec(memory_space=pl.ANY),
                  pl.BlockSpec((1, d, 2 * FFN_FC), wmap(2)),
                  pl.BlockSpec((1, 1, 2 * FFN_FC), wmap(2)),
                  pl.BlockSpec((1, FFN_FC, d), wmap(1)),
                  pl.BlockSpec((1, 1, d), lambda w, f, we_r, *_: (we_r[w], 0, 0))],
        out_specs=pl.BlockSpec(memory_space=pl.ANY),
        scratch_shapes=[pltpu.VMEM((FFN_R, d), BF16),
                        pltpu.VMEM((FFN_R, d), F32),
                        pltpu.VMEM((FFN_FC, d), BF16),
                        pltpu.VMEM((2, FFN_TM * PACK, LANES), U32),
                        pltpu.SemaphoreType.DMA(()),
                        pltpu.SemaphoreType.DMA((2,))],
    )
    return pl.pallas_call(
        _ffn_kernel,
        grid_spec=grid_spec,
        out_shape=jax.ShapeDtypeStruct((ns * PACK, LANES), U32),
        compiler_params=_cparams(("arbitrary", "arbitrary")),
        name="moe_ffn",
    )(we, row0, ntile, nvalid, xs, w1p, b1p, w2b, b2)


def _combine_kernel(cur_ref, nxt_ref, ys_hbm, gates_ref, x_ref, gmod_ref, fg_ref, o_ref, buf_ref, sem, *, final):
    i = pl.program_id(0)
    n = x_ref.shape[0]
    slot = i % 2

    def row_copy(idx_ref, b, k, r):
        src = ys_hbm.at[_prow(idx_ref[0, 0, k * n + r]), :]
        return pltpu.make_async_copy(src, buf_ref.at[b, k, _prow(r), :], sem.at[b])

    def issue(idx_ref, b):
        for k in range(TOP_K):
            def body(j, c):
                for par in range(2):
                    row_copy(idx_ref, b, k, 2 * j + par).start(priority=par)
                return c
            lax.fori_loop(0, n // 2, body, 0, unroll=4)

    @pl.when(i == 0)
    def _():
        issue(cur_ref, 0)

    @pl.when(i + 1 < pl.num_programs(0))
    def _():
        issue(nxt_ref, 1 - slot)

    for k in range(TOP_K):
        def wait(r, c):
            row_copy(cur_ref, slot, k, r).wait()
            return c
        lax.fori_loop(0, n, wait, 0, unroll=8)

    gates = gates_ref[...]
    gk = [gates[:, k:k + 1] for k in range(TOP_K)]
    half = D_MODEL // 2
    for s in range(PACK):
        lo = hi = None
        for k in range(TOP_K):
            lo_k, hi_k = _unpack_load(buf_ref, n, s, lead=(slot, k))
            lo = gk[k] * lo_k if lo is None else lo + gk[k] * lo_k
            hi = gk[k] * hi_k if hi is None else hi + gk[k] * hi_k
        for base, acc in ((s * LANES, lo), (half + s * LANES, hi)):
            cols = slice(base, base + LANES)
            o_ref[:, cols] = x_ref[:, cols] + gmod_ref[0, :, cols] * acc
    if final:
        o_ref[...] = _rms(o_ref[...], fg_ref[...])


def _combine(ys, dest, gates, x2, mod, final_g, seq, final):
    t, d = x2.shape
    tm = min(COMBINE_T, seq)
    tpb = seq // tm
    nt = t // tm
    idx = dest.reshape(nt, tm, TOP_K).transpose(0, 2, 1).reshape(nt, 1, TOP_K * tm)
    row = lambda w: pl.BlockSpec((tm, w), lambda i: (i, 0))
    return pl.pallas_call(
        functools.partial(_combine_kernel, final=final),
        grid=(nt,),
        in_specs=[pl.BlockSpec((1, 1, tm * TOP_K), lambda i: (i, 0, 0), memory_space=pltpu.SMEM),
                  pl.BlockSpec((1, 1, tm * TOP_K), lambda i: (jnp.minimum(i + 1, nt - 1), 0, 0),
                               memory_space=pltpu.SMEM),
                  pl.BlockSpec(memory_space=pl.ANY),
                  row(LANES), row(d), _mod_specs(tpb, 2),
                  pl.BlockSpec((1, d), lambda i: (0, 0))],
        out_specs=row(d),
        out_shape=jax.ShapeDtypeStruct((t, d), F32),
        scratch_shapes=[pltpu.VMEM((2, TOP_K, tm * PACK, LANES), U32), pltpu.SemaphoreType.DMA((2,))],
        compiler_params=_cparams(("arbitrary",)),
        name="moe_combine",
    )(idx, idx, ys, gates, x2, mod, final_g.reshape(1, -1))


def _moe_sublayer(x2, c_mod, norm_g, router_w, router_b, w1, b1, w2, b2, final_g, seq, final):
    t = x2.shape[0]
    h, route, gates, cnt = _route(x2, norm_g, c_mod, router_w, router_b, seq)
    top_e = route[:, :TOP_K]
    rank = route[:, TOP_K:2 * TOP_K]
    counts = cnt[0, :N_EXPERTS].astype(jnp.int32)
    padded = ((counts + FFN_TM - 1) // FFN_TM) * FFN_TM
    starts = jnp.cumsum(padded) - padded
    dest = starts[top_e] + rank
    ns = t * TOP_K + N_EXPERTS * FFN_TM
    ns = ((ns + GATHER_T - 1) // GATHER_T) * GATHER_T
    tok = jnp.broadcast_to(jnp.arange(t, dtype=jnp.int32)[:, None], (t, TOP_K))
    src_tok = jnp.zeros((ns,), jnp.int32).at[dest.reshape(-1)].set(tok.reshape(-1))
    n_items = (padded + FFN_R - 1) // FFN_R
    item_end = jnp.cumsum(n_items)
    wmax = (t * TOP_K) // FFN_R + N_EXPERTS
    wid = jnp.arange(wmax, dtype=jnp.int32)
    nvalid = item_end[-1].astype(jnp.int32)
    we = jnp.minimum(jnp.searchsorted(item_end, wid, side='right'), N_EXPERTS - 1).astype(jnp.int32)
    last_e = we[jnp.maximum(nvalid - 1, 0)]
    valid = wid < nvalid
    we = jnp.where(valid, we, last_e)
    local = wid - (item_end - n_items)[we]
    row0 = jnp.where(valid, starts[we] + local * FFN_R, 0).astype(jnp.int32)
    ntile = jnp.where(valid, jnp.minimum(FFN_R, padded[we] - local * FFN_R) // FFN_TM, 0).astype(jnp.int32)

    xs = _gather_rows(h, src_tok)
    hw = PERM_W // 2
    b1p = b1.reshape(N_EXPERTS, 2 * EXPERT_FF // PERM_W, hw, 2).transpose(0, 1, 3, 2).reshape(N_EXPERTS, 1, 2 * EXPERT_FF)
    ys = _ffn(xs, _w1_prep(w1), b1p, w2, b2.reshape(N_EXPERTS, 1, D_MODEL),
              we, row0, ntile, jnp.stack([nvalid, jnp.sum(padded).astype(jnp.int32)]))
    return _combine(ys, dest.astype(jnp.int32), gates, x2, c_mod, final_g, seq, final)


def _conv_pw1_kernel(x_ref, g_ref, shift_ref, scale_ref, wa_ref, wb_ref, ba_ref, bb_ref, o_ref, h_ref):
    j = pl.program_id(1)

    @pl.when(j == 0)
    def _():
        h_ref[...] = _norm_mod(x_ref[...], g_ref[...], scale_ref[0], shift_ref[0]).astype(BF16)

    h = h_ref[...]
    a = _dot(h, wa_ref[...]) + ba_ref[...]
    b = _dot(h, wb_ref[...]) + bb_ref[...]
    o_ref[...] = a * jax.nn.sigmoid(b)


def _conv_pw1(x2, g, mod, w_pw1, b_pw1, seq):
    t = x2.shape[0]
    tm = min(PROJ_TM, seq)
    tpb = seq // tm
    tn = 512
    nj = D_MODEL // tn
    mod2 = lambda which: pl.BlockSpec((1, 1, D_MODEL), lambda i, j: (i // tpb, 0, which))
    b2 = b_pw1.reshape(1, -1)
    return pl.pallas_call(
        _conv_pw1_kernel,
        grid=(t // tm, nj),
        in_specs=[pl.BlockSpec((tm, D_MODEL), lambda i, j: (i, 0)),
                  pl.BlockSpec((1, D_MODEL), lambda i, j: (0, 0)),
                  mod2(0), mod2(1),
                  pl.BlockSpec((D_MODEL, tn), lambda i, j: (0, j)),
                  pl.BlockSpec((D_MODEL, tn), lambda i, j: (0, j + nj)),
                  pl.BlockSpec((1, tn), lambda i, j: (0, j)),
                  pl.BlockSpec((1, tn), lambda i, j: (0, j + nj))],
        out_specs=pl.BlockSpec((tm, tn), lambda i, j: (i, j)),
        out_shape=jax.ShapeDtypeStruct((t, D_MODEL), F32),
        scratch_shapes=[pltpu.VMEM((tm, D_MODEL), BF16)],
        compiler_params=_cparams(("arbitrary", "arbitrary")),
        name="conv_pw1",
    )(x2, g.reshape(1, -1), mod, mod, w_pw1, w_pw1, b2, b2)


def _conv_dw_kernel(y_ref, halo_ref, wdw_ref, bdw_ref, lng_ref, lnb_ref, wp_ref, bp_ref, x_ref, gate_ref,
                    o_ref, buf_ref, z_ref, *, tpb):
    i = pl.program_id(0)
    tm = y_ref.shape[0]
    first = (i % tpb) == 0
    halo = halo_ref[...]
    buf_ref[0:CONV_HALO, :] = jnp.where(first, 0.0, halo)
    buf_ref[CONV_HALO:, :] = y_ref[...]
    cw = 256
    rb = 64
    off = CONV_HALO - (CONV_KERNEL - 1)
    for c0 in range(0, D_MODEL, cw):
        for r0 in range(0, tm, rb):
            acc = jnp.zeros((rb, cw), F32)
            for k in range(CONV_KERNEL):
                acc = acc + wdw_ref[k:k + 1, c0:c0 + cw] * buf_ref[r0 + off + k:r0 + off + k + rb, c0:c0 + cw]
            z_ref[r0:r0 + rb, c0:c0 + cw] = acc
    z = z_ref[...] + bdw_ref[...]
    mu = jnp.mean(z, axis=-1, keepdims=True)
    zc = z - mu
    var = jnp.mean(zc * zc, axis=-1, keepdims=True)
    zn = zc * lax.rsqrt(var + LN_EPS) * lng_ref[...] + lnb_ref[...]
    act = (zn * jax.nn.sigmoid(zn)).astype(BF16)
    o_ref[...] = x_ref[...] + gate_ref[0] * (_dot(act, wp_ref[...]) + bp_ref[...])


def _conv_dw(y1, x2, mod, w_dw, b_dw, ln_g, ln_b, w_pw2, b_pw2, seq):
    t = x2.shape[0]
    tm = min(CONV_TM, seq)
    tpb = seq // tm
    hb = tm // CONV_HALO
    wdw = jnp.zeros((CONV_HALO, D_MODEL), F32).at[:CONV_KERNEL].set(w_dw)
    row = lambda: pl.BlockSpec((tm, D_MODEL), lambda i: (i, 0))
    vec = lambda: pl.BlockSpec((1, D_MODEL), lambda i: (0, 0))
    return pl.pallas_call(
        functools.partial(_conv_dw_kernel, tpb=tpb),
        grid=(t // tm,),
        in_specs=[row(),
                  pl.BlockSpec((CONV_HALO, D_MODEL), lambda i: (jnp.maximum(i * hb - 1, 0), 0)),
                  pl.BlockSpec((CONV_HALO, D_MODEL), lambda i: (0, 0)),
                  vec(), vec(), vec(),
                  pl.BlockSpec((D_MODEL, D_MODEL), lambda i: (0, 0)),
                  vec(), row(), _mod_specs(tpb, 2)],
        out_specs=row(),
        out_shape=jax.ShapeDtypeStruct((t, D_MODEL), F32),
        scratch_shapes=[pltpu.VMEM((tm + CONV_HALO, D_MODEL), F32), pltpu.VMEM((tm, D_MODEL), F32)],
        compiler_params=_cparams(("arbitrary",)),
        name="conv_dw_pw2",
    )(y1, y1, wdw, b_dw.reshape(1, -1), ln_g.reshape(1, -1), ln_b.reshape(1, -1),
      w_pw2, b_pw2.reshape(1, -1), x2, mod)


def _rope_swap_cols(w, heads):
    k = w.shape[0]
    half = MLA_ROPE // 2
    w3 = w.reshape(k, heads, MLA_NOPE + MLA_ROPE)
    pe = w3[:, :, MLA_NOPE:]
    sw = jnp.concatenate([pe[:, :, half:], pe[:, :, :half]], axis=-1)
    return jnp.concatenate([w3, sw], axis=-1).reshape(k, heads * (MLA_NOPE + 2 * MLA_ROPE))


def kernel(x, c, positions, l0_mix_norm_g, l0_mix_ada_w, l0_mix_ada_b, l0_w_in, l0_s5_lambda_re, l0_s5_lambda_im, l0_s5_log_step, l0_s5_b_re, l0_s5_b_im, l0_s5_c_re, l0_s5_c_im, l0_s5_d, l0_s5_w_glu, l0_s5_b_glu, l0_mla_q_norm_g, l0_mla_w_uq, l0_mla_kv_norm_g, l0_mla_w_ukv, l0_w_out, l0_moe_norm_g, l0_moe_ada_w, l0_moe_ada_b, l0_router_w, l0_router_b, l0_exp_w1, l0_exp_b1, l0_exp_w2, l0_exp_b2, l1_mix_norm_g, l1_mix_ada_w, l1_mix_ada_b, l1_conv_w_pw1, l1_conv_b_pw1, l1_conv_w_dw, l1_conv_b_dw, l1_conv_ln_g, l1_conv_ln_b, l1_conv_w_pw2, l1_conv_b_pw2, l1_moe_norm_g, l1_moe_ada_w, l1_moe_ada_b, l1_router_w, l1_router_b, l1_exp_w1, l1_exp_b1, l1_exp_w2, l1_exp_b2, final_norm_g):
    nbatch, seq, d = x.shape
    t = nbatch * seq
    x2 = x.reshape(t, d)
    pos = positions.reshape(t, 1).astype(jnp.int32)

    mod = _ada_mod(c, l0_mix_ada_w, l0_mix_ada_b)
    half = MLA_ROPE // 2
    kcol = S5_WIDTH + MLA_Q_RANK + MLA_KV_RANK
    w_in_ext = jnp.concatenate([l0_w_in, l0_w_in[:, kcol + half:], l0_w_in[:, kcol:kcol + half]], axis=1).astype(BF16)
    u, cq, ckv, kr = _proj_in(x2, l0_mix_norm_g, mod, w_in_ext, seq)

    kt, wt, vv, a16 = _s5_prep(l0_s5_lambda_re, l0_s5_lambda_im, l0_s5_log_step, l0_s5_b_re, l0_s5_b_im,
                               l0_s5_c_re, l0_s5_c_im, l0_s5_d)
    nchunk = seq // S5_CHUNK
    u_r = (u.reshape(nbatch, nchunk, S5_CHUNK, S5_GROUPS, S5_GROUP_CH)
           .transpose(3, 0, 1, 2, 4).reshape(S5_GROUPS, nbatch * nchunk, S5_ROW))
    y_r = _s5_main(u_r, kt, wt, vv, a16, nbatch, nchunk)
    ys = (y_r.reshape(S5_GROUPS, nbatch, nchunk, S5_CHUNK, S5_GROUP_CH)
          .transpose(1, 2, 3, 0, 4).reshape(t, S5_WIDTH))

    wq_ext = _rope_swap_cols(l0_mla_w_uq, MLA_HEADS).astype(BF16)
    q, k, v = _mla_proj(cq, ckv, kr, pos, l0_mla_q_norm_g, l0_mla_kv_norm_g, wq_ext,
                        l0_mla_w_ukv.astype(BF16), nbatch, seq)
    ym = _flash_attention(q, k, v).reshape(t, MLA_HEADS * MLA_V)
    x2 = _mixer_out(ys, ym, x2, mod, l0_s5_w_glu.astype(BF16), l0_s5_b_glu, l0_w_out.astype(BF16), seq)

    mod = _ada_mod(c, l0_moe_ada_w, l0_moe_ada_b)
    x2 = _moe_sublayer(x2, mod, l0_moe_norm_g, l0_router_w, l0_router_b, l0_exp_w1, l0_exp_b1,
                       l0_exp_w2, l0_exp_b2, final_norm_g, seq, final=False)

    mod = _ada_mod(c, l1_mix_ada_w, l1_mix_ada_b)
    y1 = _conv_pw1(x2, l1_mix_norm_g, mod, l1_conv_w_pw1.astype(BF16), l1_conv_b_pw1, seq)
    x2 = _conv_dw(y1, x2, mod, l1_conv_w_dw, l1_conv_b_dw, l1_conv_ln_g, l1_conv_ln_b,
                  l1_conv_w_pw2.astype(BF16), l1_conv_b_pw2, seq)

    mod = _ada_mod(c, l1_moe_ada_w, l1_moe_ada_b)
    x2 = _moe_sublayer(x2, mod, l1_moe_norm_g, l1_router_w, l1_router_b, l1_exp_w1, l1_exp_b1,
                       l1_exp_w2, l1_exp_b2, final_norm_g, seq, final=True)
    return x2.reshape(nbatch, seq, d)
```

```python
import functools
import math

import numpy as np
import jax
import jax.numpy as jnp
from jax import lax
from jax.experimental import pallas as pl
from jax.experimental.pallas import tpu as pltpu

F32 = jnp.float32
BF16 = jnp.bfloat16

D_MODEL = 2048
S5_WIDTH = 1024
S5_GROUP_CH = 16
S5_GROUPS = 64
S5_STATE = 64
MLA_HEADS = 8
MLA_NOPE = 128
MLA_ROPE = 64
MLA_V = 128
MLA_Q_RANK = 512
MLA_KV_RANK = 512
ROPE_THETA = 10000.0
CONV_KERNEL = 31
N_EXPERTS = 32
TOP_K = 4
EXPERT_FF = 2048
SWIGLU_ALPHA = 1.702
SWIGLU_LIMIT = 7.0
EPS = 1e-6
LN_EPS = 1e-5
NEG_BIG = -1e30

LANES = 128
SUBLANES = 8
VMEM_LIMIT_BYTES = 56 * 1024 * 1024

S5_CHUNK = 16
S5_ROW = S5_CHUNK * S5_GROUP_CH
PROJ_TM = 512
MLA_TM = 256
ATT_T = 512
OUT_TM = 256
ROUTE_TM = 512
CONV_TM = 256
CONV_HALO = 32
CONV_RB = 128
CONV_CW = 256
FFN_TM = 256
FFN_R = 2304
FFN_FC = 512
GATHER_T = 512
COMBINE_T = 256


def _cparams(sem):
    return pltpu.CompilerParams(dimension_semantics=sem, vmem_limit_bytes=VMEM_LIMIT_BYTES)


def _norm_mod(x, g, scale, shift):
    ms = jnp.mean(x * x, axis=-1, keepdims=True)
    return (x * lax.rsqrt(ms + EPS)) * g * (1.0 + scale) + shift


def _rms(x, g):
    ms = jnp.mean(x * x, axis=-1, keepdims=True)
    return (x * lax.rsqrt(ms + EPS)) * g


def _dot(a, b):
    return jnp.dot(a, b, preferred_element_type=F32)


def _dot_nt(a, b):
    return lax.dot_general(a, b, (((1,), (1,)), ((), ())), preferred_element_type=F32)


U32 = jnp.uint32
PACK = D_MODEL // (2 * LANES)
HI_MASK = 0xFFFF0000


def _pack_store(ref, val, lead=()):
    n = val.shape[0]
    half = D_MODEL // 2
    for s in range(PACK):
        lo = val[:, s * LANES:(s + 1) * LANES].astype(BF16).astype(F32)
        hi = val[:, half + s * LANES:half + (s + 1) * LANES].astype(BF16).astype(F32)
        word = (lax.bitcast_convert_type(lo, U32) >> 16) | (lax.bitcast_convert_type(hi, U32) & jnp.uint32(HI_MASK))
        ref[lead + (pl.ds(s, n, stride=PACK), slice(None))] = word


def _unpack_load(ref, n, s, lead=()):
    word = ref[lead + (pl.ds(s, n, stride=PACK), slice(None))]
    lo = lax.bitcast_convert_type(word << 16, F32)
    hi = lax.bitcast_convert_type(word & jnp.uint32(HI_MASK), F32)
    return lo, hi


def _prow(r, n=1):
    return pl.ds(pl.multiple_of(r * PACK, PACK), n * PACK)


ADA_TN = 768
ADA_KC = 256


def _ada_kernel(ct_ref, w_ref, b_ref, o_ref):
    nb = ct_ref.shape[0]
    kdim = w_ref.shape[0]
    rows = []
    for b in range(nb):
        acc = jnp.zeros((1, w_ref.shape[1]), F32)
        for k0 in range(0, kdim, ADA_KC):
            c = ct_ref[b, k0:k0 + ADA_KC, :]
            cs = c * jax.nn.sigmoid(c)
            acc = acc + jnp.sum(w_ref[k0:k0 + ADA_KC, :] * cs, axis=0, keepdims=True)
        rows.append(acc)
    o_ref[...] = jnp.concatenate(rows, axis=0) + b_ref[...]


def _ada_mod(c, w, b):
    nb, d = c.shape
    n = w.shape[1]
    ct = c.reshape(nb, d, 1)
    m = pl.pallas_call(
        _ada_kernel,
        grid=(n // ADA_TN,),
        in_specs=[pl.BlockSpec((nb, d, 1), lambda j: (0, 0, 0)),
                  pl.BlockSpec((d, ADA_TN), lambda j: (0, j)),
                  pl.BlockSpec((1, ADA_TN), lambda j: (0, j))],
        out_specs=pl.BlockSpec((nb, ADA_TN), lambda j: (0, j)),
        out_shape=jax.ShapeDtypeStruct((nb, n), F32),
        compiler_params=_cparams(("arbitrary",)),
        name="ada_mod",
    )(ct, w, b.reshape(1, n))
    return m.reshape(nb, 1, n)


def _mod_specs(tiles_per_batch, which):
    return pl.BlockSpec((1, 1, D_MODEL), lambda i, *_: (i // tiles_per_batch, 0, which))


def _proj_in_kernel(x_ref, g_ref, shift_ref, scale_ref, w_ref, u_ref, cq_ref, ckv_ref, kr_ref):
    h = _norm_mod(x_ref[...], g_ref[...], scale_ref[0], shift_ref[0]).astype(BF16)
    acc = _dot(h, w_ref[...])
    c0, c1, c2 = S5_WIDTH, S5_WIDTH + MLA_Q_RANK, S5_WIDTH + MLA_Q_RANK + MLA_KV_RANK
    u_ref[...] = acc[:, :c0].astype(BF16)
    cq_ref[...] = acc[:, c0:c1]
    ckv_ref[...] = acc[:, c1:c2]
    kr_ref[...] = acc[:, c2:]


def _proj_in(x2, g, mod, w_ext, seq):
    t = x2.shape[0]
    tm = min(PROJ_TM, seq)
    tpb = seq // tm
    n = w_ext.shape[1]
    row = lambda w: pl.BlockSpec((tm, w), lambda i: (i, 0))
    return pl.pallas_call(
        _proj_in_kernel,
        grid=(t // tm,),
        in_specs=[row(D_MODEL),
                  pl.BlockSpec((1, D_MODEL), lambda i: (0, 0)),
                  _mod_specs(tpb, 0), _mod_specs(tpb, 1),
                  pl.BlockSpec((D_MODEL, n), lambda i: (0, 0))],
        out_specs=[row(S5_WIDTH), row(MLA_Q_RANK), row(MLA_KV_RANK), row(2 * MLA_ROPE)],
        out_shape=[jax.ShapeDtypeStruct((t, S5_WIDTH), BF16),
                   jax.ShapeDtypeStruct((t, MLA_Q_RANK), F32),
                   jax.ShapeDtypeStruct((t, MLA_KV_RANK), F32),
                   jax.ShapeDtypeStruct((t, 2 * MLA_ROPE), F32)],
        compiler_params=_cparams(("arbitrary",)),
        name="proj_in",
    )(x2, g.reshape(1, -1), mod, mod, w_ext)


def _s5_prep_kernel(lamc_ref, lamr_ref, step_ref, bt_ref, btt_ref, ct_ref, d_ref,
                    kt_ref, wt_ref, v_ref, a16_ref):
    P, H, C = S5_STATE, S5_GROUP_CH, S5_CHUNK
    step = step_ref[0]
    step = jnp.exp(step)
    lr_c = jnp.minimum(lamc_ref[0, 0], -1e-4)
    li_c = lamc_ref[0, 1]
    lr_r = jnp.minimum(lamr_ref[0, 0:1, :], -1e-4)
    li_r = lamr_ref[0, 1:2, :]

    def ratio(lr, li):
        mag = jnp.exp(lr * step)
        ab_re = mag * jnp.cos(li * step)
        ab_im = mag * jnp.sin(li * step)
        denom = lr * lr + li * li
        nr, ni = ab_re - 1.0, ab_im
        return (nr * lr + ni * li) / denom, (ni * lr - nr * li) / denom

    rr_c, ri_c = ratio(lr_c, li_c)
    rr_r, ri_r = ratio(lr_r, li_r)

    lane = lax.broadcasted_iota(jnp.int32, (1, S5_ROW), 1)
    kk = (lane // H).astype(F32)

    def powers(k):
        mag = jnp.exp(lr_c * step * k)
        return mag * jnp.cos(li_c * step * k), mag * jnp.sin(li_c * step * k)

    bre_t, bim_t = bt_ref[0, 0], bt_ref[0, 1]
    bbt_re = rr_c * bre_t - ri_c * bim_t
    bbt_im = rr_c * bim_t + ri_c * bre_t
    cre_t, cim_t = ct_ref[0, 0], ct_ref[0, 1]

    er, ei = powers(float(C - 1) - kk)
    wt_ref[0, 0:P, :] = (er * bbt_re - ei * bbt_im).astype(BF16)
    wt_ref[0, P:2 * P, :] = (er * bbt_im + ei * bbt_re).astype(BF16)

    er, ei = powers(kk + 1.0)
    v_ref[0, 0:P, :] = (cre_t * er - cim_t * ei).astype(BF16)
    v_ref[0, P:2 * P, :] = (-cre_t * ei - cim_t * er).astype(BF16)

    er, ei = powers(kk)
    q_re = er * cre_t - ei * cim_t
    q_im = er * cim_t + ei * cre_t
    brt, bit = btt_ref[0, 0], btt_ref[0, 1]
    bbr = rr_r * brt - ri_r * bit
    bbi = rr_r * bit + ri_r * brt
    hi = lax.Precision.HIGHEST
    mall = (jnp.dot(bbr, q_re, precision=hi, preferred_element_type=F32)
            - jnp.dot(bbi, q_im, precision=hi, preferred_element_type=F32))
    rowh = lax.broadcasted_iota(jnp.int32, (H, S5_ROW), 0)
    laneh = lax.broadcasted_iota(jnp.int32, (H, S5_ROW), 1)
    mall = mall + jnp.where(laneh == rowh, d_ref[0], 0.0)
    for s in range(C):
        piece = mall if s == 0 else pltpu.roll(mall, H * s, 1)
        piece = jnp.where(laneh >= H * s, piece, 0.0)
        kt_ref[0, s * H:(s + 1) * H, :] = piece.astype(BF16)

    mag = jnp.exp(lr_r * step * float(C))
    a16_ref[0, 0:1, :] = mag * jnp.cos(li_r * step * float(C))
    a16_ref[0, 1:2, :] = mag * jnp.sin(li_r * step * float(C))


def _s5_prep(lam_re, lam_im, log_step, b_re, b_im, c_re, c_im, d):
    G, P, H, C = S5_GROUPS, S5_STATE, S5_GROUP_CH, S5_CHUNK
    lam = jnp.stack([lam_re, lam_im], axis=1).astype(F32)
    lamc = lam.reshape(G, 2, P, 1)
    b = jnp.stack([b_re, b_im], axis=1).astype(F32)
    bt = jnp.tile(b, (1, 1, 1, C))
    btt = jnp.swapaxes(b, 2, 3)
    c = jnp.stack([c_re, c_im], axis=1).astype(F32)
    ct = jnp.tile(jnp.swapaxes(c, 2, 3), (1, 1, 1, C))
    dt = jnp.tile(d.astype(F32), (1, C)).reshape(G, 1, S5_ROW)
    step = log_step.astype(F32).reshape(G, 1, 1)
    g4 = lambda *shape: pl.BlockSpec((1,) + shape, lambda g: (g,) + (0,) * len(shape))
    return pl.pallas_call(
        _s5_prep_kernel,
        grid=(G,),
        in_specs=[g4(2, P, 1), g4(2, P), g4(1, 1), g4(2, P, S5_ROW), g4(2, H, P),
                  g4(2, P, S5_ROW), g4(1, S5_ROW)],
        out_specs=[g4(S5_ROW, S5_ROW), g4(2 * P, S5_ROW), g4(2 * P, S5_ROW), g4(2, P)],
        out_shape=[jax.ShapeDtypeStruct((G, S5_ROW, S5_ROW), BF16),
                   jax.ShapeDtypeStruct((G, 2 * P, S5_ROW), BF16),
                   jax.ShapeDtypeStruct((G, 2 * P, S5_ROW), BF16),
                   jax.ShapeDtypeStruct((G, 2, P), F32)],
        compiler_params=_cparams(("arbitrary",)),
        name="s5_prep",
    )(lamc, lam, step, bt, btt, ct, dt)


def _s5_main_kernel(u_ref, kt_ref, wt_ref, v_ref, a16_ref, y_ref, sr_ref, si_ref, xr_ref, xi_ref,
                    *, nbatch, nchunk):
    P = S5_STATE
    u = u_ref[0]
    sr_ref[...] = _dot_nt(u, wt_ref[0, 0:P, :])
    si_ref[...] = _dot_nt(u, wt_ref[0, P:2 * P, :])
    ar = a16_ref[0, 0:1, :]
    ai = a16_ref[0, 1:2, :]

    def step(c, carry):
        out = []
        for b in range(nbatch):
            xr, xi = carry[2 * b], carry[2 * b + 1]
            r = b * nchunk + c
            xr_ref[pl.ds(r, 1), :] = xr
            xi_ref[pl.ds(r, 1), :] = xi
            nxr = ar * xr - ai * xi + sr_ref[pl.ds(r, 1), :]
            nxi = ar * xi + ai * xr + si_ref[pl.ds(r, 1), :]
            out += [nxr, nxi]
        return tuple(out)

    zero = jnp.zeros((1, P), F32)
    lax.fori_loop(0, nchunk, step, (zero,) * (2 * nbatch))
    y = _dot(u, kt_ref[0])
    y = y + _dot(xr_ref[...].astype(BF16), v_ref[0, 0:P, :])
    y = y + _dot(xi_ref[...].astype(BF16), v_ref[0, P:2 * P, :])
    y_ref[0] = y.astype(y_ref.dtype)


def _s5_main(u_r, kt, wt, v, a16, nbatch, nchunk):
    G, rows, _ = u_r.shape
    P = S5_STATE
    g3 = lambda a, b: pl.BlockSpec((1, a, b), lambda g: (g, 0, 0))
    return pl.pallas_call(
        functools.partial(_s5_main_kernel, nbatch=nbatch, nchunk=nchunk),
        grid=(G,),
        in_specs=[g3(rows, S5_ROW), g3(S5_ROW, S5_ROW), g3(2 * P, S5_ROW), g3(2 * P, S5_ROW), g3(2, P)],
        out_specs=g3(rows, S5_ROW),
        out_shape=jax.ShapeDtypeStruct((G, rows, S5_ROW), BF16),
        scratch_shapes=[pltpu.VMEM((rows, P), F32) for _ in range(4)],
        compiler_params=_cparams(("arbitrary",)),
        name="s5_main",
    )(u_r, kt, wt, v, a16)


def _mla_proj_kernel(cq_ref, ckv_ref, kr_ref, pos_ref, qg_ref, kvg_ref, wq_ref, wkv_ref,
                     invf_ref, sgn_ref, q_ref, k_ref, v_ref):
    qscale = 1.0 / math.sqrt(MLA_NOPE + MLA_ROPE)
    qa = _dot(_rms(cq_ref[...], qg_ref[...]).astype(BF16), wq_ref[...])
    kva = _dot(_rms(ckv_ref[...], kvg_ref[...]).astype(BF16), wkv_ref[...])
    ang = pos_ref[...].astype(F32) * invf_ref[...]
    cc = jnp.cos(ang)
    ss = jnp.sin(ang) * sgn_ref[...]

    def rope(slab):
        return slab * cc + pltpu.roll(slab, MLA_ROPE, 1) * ss

    kpe = rope(kr_ref[...])[:, :MLA_ROPE].astype(BF16)
    hw = MLA_NOPE + 2 * MLA_ROPE
    lane = lax.broadcasted_iota(jnp.int32, (cq_ref.shape[0], MLA_V), 1)
    ones_col = jnp.where(lane == 0, 1.0, 0.0).astype(BF16)
    for h in range(MLA_HEADS):
        blk = qa[:, h * hw:(h + 1) * hw]
        q_ref[0, h, :, 0:MLA_NOPE] = (blk[:, :MLA_NOPE] * qscale).astype(BF16)
        qpe = rope(blk[:, MLA_NOPE:]) * qscale
        q_ref[0, h, :, MLA_NOPE:MLA_NOPE + MLA_ROPE] = qpe[:, :MLA_ROPE].astype(BF16)
        kvb = kva[:, h * hw:(h + 1) * hw]
        k_ref[0, h, :, 0:MLA_NOPE] = kvb[:, :MLA_NOPE].astype(BF16)
        k_ref[0, h, :, MLA_NOPE:MLA_NOPE + MLA_ROPE] = kpe
        v_ref[0, h, :, 0:MLA_V] = kvb[:, MLA_NOPE:].astype(BF16)
        v_ref[0, h, :, MLA_V:2 * MLA_V] = ones_col


def _mla_proj(cq, ckv, kr, pos, qg, kvg, wq_ext, wkv, nbatch, seq):
    tm = min(MLA_TM, seq)
    nl = seq // tm
    dqk = MLA_NOPE + MLA_ROPE
    half = MLA_ROPE // 2
    inv_freq = 1.0 / (ROPE_THETA ** (jnp.arange(0, MLA_ROPE, 2, dtype=F32) / MLA_ROPE))
    invf = jnp.tile(inv_freq, 4).reshape(1, 2 * MLA_ROPE)
    sgn = jnp.tile(jnp.concatenate([-jnp.ones((half,), F32), jnp.ones((half,), F32)]), 2).reshape(1, 2 * MLA_ROPE)
    row = lambda w: pl.BlockSpec((tm, w), lambda b, i: (b * nl + i, 0))
    full = lambda a, b_: pl.BlockSpec((a, b_), lambda b, i: (0, 0))
    head = lambda w: pl.BlockSpec((1, MLA_HEADS, tm, w), lambda b, i: (b, 0, i, 0))
    return pl.pallas_call(
        _mla_proj_kernel,
        grid=(nbatch, nl),
        in_specs=[row(MLA_Q_RANK), row(MLA_KV_RANK), row(2 * MLA_ROPE), row(1),
                  full(1, MLA_Q_RANK), full(1, MLA_KV_RANK),
                  full(MLA_Q_RANK, wq_ext.shape[1]), full(MLA_KV_RANK, wkv.shape[1]),
                  full(1, 2 * MLA_ROPE), full(1, 2 * MLA_ROPE)],
        out_specs=[head(dqk), head(dqk), head(2 * MLA_V)],
        out_shape=[jax.ShapeDtypeStruct((nbatch, MLA_HEADS, seq, dqk), BF16),
                   jax.ShapeDtypeStruct((nbatch, MLA_HEADS, seq, dqk), BF16),
                   jax.ShapeDtypeStruct((nbatch, MLA_HEADS, seq, 2 * MLA_V), BF16)],
        compiler_params=_cparams(("arbitrary", "arbitrary")),
        name="mla_proj",
    )(cq, ckv, kr, pos, qg.reshape(1, -1), kvg.reshape(1, -1), wq_ext, wkv, invf, sgn)


def _lane_groups(x):
    return [x[:, j * LANES:(j + 1) * LANES] for j in range(x.shape[1] // LANES)]


def _flash_kernel(q_ref, k_ref, v_ref, o_ref, s_ref, *, t):
    qi = pl.program_id(2)
    qh = [q_ref[0, 0, 0:t, :], q_ref[0, 0, t:2 * t, :]]

    def kblk(c):
        return k_ref[0, 0, pl.ds(pl.multiple_of(c * t, t), t), :]

    def vblk(c):
        return v_ref[0, 0, pl.ds(pl.multiple_of(c * t, t), t), :]

    def fold_max(s, mx):
        return functools.reduce(jnp.maximum, _lane_groups(s), mx)

    def pass1(c, mx):
        kc = kblk(c)
        out = []
        for h in range(2):
            s = _dot_nt(qh[h], kc)
            s_ref[h, c] = s
            out.append(fold_max(s, mx[h]))
        return tuple(out)

    neg = jnp.full((t, LANES), NEG_BIG, F32)
    mx0, mx1 = lax.fori_loop(0, 2 * qi, pass1, (neg, neg))
    c0, c1 = 2 * qi, 2 * qi + 1
    row = lax.broadcasted_iota(jnp.int32, (t, t), 0)
    col = lax.broadcasted_iota(jnp.int32, (t, t), 1)
    tri = col <= row
    k0, k1 = kblk(c0), kblk(c1)
    s00 = jnp.where(tri, _dot_nt(qh[0], k0), NEG_BIG)
    s10 = _dot_nt(qh[1], k0)
    s11 = jnp.where(tri, _dot_nt(qh[1], k1), NEG_BIG)
    s_ref[0, c0] = s00
    s_ref[1, c0] = s10
    s_ref[1, c1] = s11
    mx0 = fold_max(s00, mx0)
    mx1 = fold_max(s11, fold_max(s10, mx1))
    m = [jnp.broadcast_to(jnp.max(mx, axis=-1, keepdims=True), (t, LANES)) for mx in (mx0, mx1)]

    mt = [jnp.concatenate([mh] * (t // LANES), axis=1) for mh in m]

    def probs(h, c):
        return jnp.exp(s_ref[h, c] - mt[h]).astype(BF16)

    def pass2(j, carry):
        c = 2 * j
        vc = v_ref[0, 0, pl.ds(pl.multiple_of(c * t, 2 * t), 2 * t), :]
        out = []
        for h in range(2):
            p = jnp.concatenate([probs(h, c), probs(h, c + 1)], axis=1)
            out.append(carry[h] + _dot(p, vc))
        return tuple(out)

    za = jnp.zeros((t, 2 * MLA_V), F32)
    a0, a1 = lax.fori_loop(0, qi, pass2, (za, za))
    v0, v1 = vblk(c0), vblk(c1)
    a0 = a0 + _dot(probs(0, c0), v0)
    a1 = a1 + _dot(jnp.concatenate([probs(1, c0), probs(1, c1)], axis=1), jnp.concatenate([v0, v1], axis=0))
    o_ref[0, 0:t, :] = (a0[:, :MLA_V] / a0[:, MLA_V:MLA_V + 1]).astype(o_ref.dtype)
    o_ref[0, t:2 * t, :] = (a1[:, :MLA_V] / a1[:, MLA_V:MLA_V + 1]).astype(o_ref.dtype)


def _flash_attention(q, k, v):
    nbatch, nh, seq, dqk = q.shape
    t = min(ATT_T, seq // 2)
    nq = seq // (2 * t)
    return pl.pallas_call(
        functools.partial(_flash_kernel, t=t),
        grid=(nbatch, nh, nq),
        in_specs=[pl.BlockSpec((1, 1, 2 * t, dqk), lambda b, h, i: (b, h, i, 0)),
                  pl.BlockSpec((1, 1, seq, dqk), lambda b, h, i: (b, h, 0, 0)),
                  pl.BlockSpec((1, 1, seq, 2 * MLA_V), lambda b, h, i: (b, h, 0, 0))],
        out_specs=pl.BlockSpec((1, 2 * t, MLA_V), lambda b, h, i: (b, i, h)),
        out_shape=jax.ShapeDtypeStruct((nbatch, seq, nh * MLA_V), BF16),
        scratch_shapes=[pltpu.VMEM((2, seq // t, t, t), F32)],
        compiler_params=_cparams(("arbitrary", "arbitrary", "arbitrary")),
        name="flash_attn",
    )(q, k, v)


def _gelu_tanh(x):
    c = math.sqrt(2.0 / math.pi)
    return 0.5 * x * (1.0 + jnp.tanh(c * (x + 0.044715 * (x * x * x))))


def _mixer_out_kernel(ys_ref, ym_ref, x_ref, gate_ref, wglu_ref, bglu_ref, wo_ref, o_ref):
    y = _gelu_tanh(ys_ref[...].astype(F32))
    g = _dot(y.astype(BF16), wglu_ref[...]) + bglu_ref[...]
    s5o = (y * jax.nn.sigmoid(g)).astype(BF16)
    acc = _dot(s5o, wo_ref[0:S5_WIDTH, :]) + _dot(ym_ref[...], wo_ref[S5_WIDTH:, :])
    o_ref[...] = x_ref[...] + gate_ref[0] * acc


def _mixer_out(ys, ym, x2, mod, w_glu, b_glu, w_out, seq):
    t = x2.shape[0]
    tm = min(OUT_TM, seq)
    tpb = seq // tm
    row = lambda w: pl.BlockSpec((tm, w), lambda i: (i, 0))
    full = lambda a, b: pl.BlockSpec((a, b), lambda i: (0, 0))
    return pl.pallas_call(
        _mixer_out_kernel,
        grid=(t // tm,),
        in_specs=[row(S5_WIDTH), row(MLA_HEADS * MLA_V), row(D_MODEL), _mod_specs(tpb, 2),
                  full(S5_WIDTH, S5_WIDTH), full(1, S5_WIDTH), full(2 * S5_WIDTH, D_MODEL)],
        out_specs=row(D_MODEL),
        out_shape=jax.ShapeDtypeStruct((t, D_MODEL), F32),
        compiler_params=_cparams(("arbitrary",)),
        name="mixer_out",
    )(ys, ym, x2, mod, w_glu, b_glu.reshape(1, -1), w_out)


def _route_kernel(x_ref, g_ref, shift_ref, scale_ref, rw_ref, rb_ref, tri_ref,
                  h_ref, route_ref, gates_ref, cnt_ref, carry_ref):
    i = pl.program_id(0)

    @pl.when(i == 0)
    def _():
        carry_ref[...] = jnp.zeros(carry_ref.shape, F32)

    h = _norm_mod(x_ref[...], g_ref[...], scale_ref[0], shift_ref[0])
    _pack_store(h_ref, h)
    logits = jnp.dot(h, rw_ref[...], precision=lax.Precision.HIGHEST,
                     preferred_element_type=F32) + rb_ref[...]
    lane = lax.broadcasted_iota(jnp.int32, logits.shape, 1)
    lanef = lane.astype(F32)
    lg = jnp.where(lane < N_EXPERTS, logits, -jnp.inf)
    vals, hots, idxs = [], [], []
    sel = jnp.zeros(logits.shape, F32)
    for _ in range(TOP_K):
        m = jnp.max(lg, axis=-1, keepdims=True)
        idx = jnp.min(jnp.where(lg == m, lanef, float(LANES)), axis=-1, keepdims=True)
        hot = lanef == idx
        vals.append(m)
        idxs.append(idx)
        hots.append(hot)
        sel = jnp.where(hot, 1.0, sel)
        lg = jnp.where(hot, -jnp.inf, lg)
    es = [jnp.exp(v - vals[0]) for v in vals]
    denom = es[0] + es[1] + es[2] + es[3]
    before = _dot(tri_ref[...], sel.astype(BF16)) + carry_ref[...]
    route = jnp.zeros(logits.shape, F32)
    gates = jnp.zeros(logits.shape, F32)
    for k in range(TOP_K):
        rank = jnp.sum(jnp.where(hots[k], before, 0.0), axis=-1, keepdims=True)
        route = jnp.where(lane == k, idxs[k], route)
        route = jnp.where(lane == TOP_K + k, rank, route)
        gates = jnp.where(lane == k, es[k] / denom, gates)
    route_ref[...] = route.astype(jnp.int32)
    gates_ref[...] = gates
    carry_ref[...] = carry_ref[...] + jnp.sum(sel, axis=0, keepdims=True)
    cnt_ref[...] = carry_ref[...]


def _route(x2, g, mod, router_w, router_b, seq):
    t = x2.shape[0]
    tm = min(ROUTE_TM, seq)
    tpb = seq // tm
    rw = jnp.zeros((D_MODEL, LANES), F32).at[:, :N_EXPERTS].set(router_w)
    rb = jnp.zeros((1, LANES), F32).at[0, :N_EXPERTS].set(router_b)
    tri = jnp.asarray(np.tril(np.ones((tm, tm), np.float32), -1), BF16)
    row = lambda w: pl.BlockSpec((tm, w), lambda i: (i, 0))
    full = lambda a, b: pl.BlockSpec((a, b), lambda i: (0, 0))
    return pl.pallas_call(
        _route_kernel,
        grid=(t // tm,),
        in_specs=[row(D_MODEL), full(1, D_MODEL), _mod_specs(tpb, 0), _mod_specs(tpb, 1),
                  full(D_MODEL, LANES), full(1, LANES), full(tm, tm)],
        out_specs=[pl.BlockSpec((tm * PACK, LANES), lambda i: (i, 0)), row(LANES), row(LANES), full(1, LANES)],
        out_shape=[jax.ShapeDtypeStruct((t * PACK, LANES), U32),
                   jax.ShapeDtypeStruct((t, LANES), jnp.int32),
                   jax.ShapeDtypeStruct((t, LANES), F32),
                   jax.ShapeDtypeStruct((1, LANES), F32)],
        scratch_shapes=[pltpu.VMEM((1, LANES), F32)],
        compiler_params=_cparams(("arbitrary",)),
        name="moe_route",
    )(x2, g.reshape(1, -1), mod, mod, rw, rb, tri)


def _gather_kernel(cur_ref, nxt_ref, h_hbm, o_ref, buf_ref, sem):
    i = pl.program_id(0)
    n = GATHER_T
    slot = i % 2

    def row_copy(idx_ref, b, r):
        return pltpu.make_async_copy(h_hbm.at[_prow(idx_ref[0, 0, r]), :], buf_ref.at[b, _prow(r), :], sem.at[b])

    def issue(idx_ref, b):
        def body(j, c):
            for par in range(2):
                row_copy(idx_ref, b, 2 * j + par).start(priority=par)
            return c
        lax.fori_loop(0, n // 2, body, 0, unroll=4)

    @pl.when(i == 0)
    def _():
        issue(cur_ref, 0)

    @pl.when(i + 1 < pl.num_programs(0))
    def _():
        issue(nxt_ref, 1 - slot)

    def wait(r, c):
        row_copy(cur_ref, slot, r).wait()
        return c

    lax.fori_loop(0, n, wait, 0, unroll=8)
    half = D_MODEL // 2
    for s in range(PACK):
        lo, hi = _unpack_load(buf_ref, n, s, lead=(slot,))
        o_ref[:, s * LANES:(s + 1) * LANES] = lo.astype(BF16)
        o_ref[:, half + s * LANES:half + (s + 1) * LANES] = hi.astype(BF16)


def _gather_rows(h_slabs, src_tok):
    ns = src_tok.shape[0]
    nt = ns // GATHER_T
    idx = src_tok.reshape(nt, 1, GATHER_T)
    return pl.pallas_call(
        _gather_kernel,
        grid=(nt,),
        in_specs=[pl.BlockSpec((1, 1, GATHER_T), lambda i: (i, 0, 0), memory_space=pltpu.SMEM),
                  pl.BlockSpec((1, 1, GATHER_T), lambda i: (jnp.minimum(i + 1, nt - 1), 0, 0),
                               memory_space=pltpu.SMEM),
                  pl.BlockSpec(memory_space=pl.ANY)],
        out_specs=pl.BlockSpec((GATHER_T, D_MODEL), lambda i: (i, 0)),
        out_shape=jax.ShapeDtypeStruct((ns, D_MODEL), BF16),
        scratch_shapes=[pltpu.VMEM((2, GATHER_T * PACK, LANES), U32), pltpu.SemaphoreType.DMA((2,))],
        compiler_params=_cparams(("arbitrary",)),
        name="moe_gather",
    )(idx, idx, h_slabs)


PERM_W = 256
W1_BLK = 1024


def _w1_prep_kernel(w_ref, p_ref, o_ref):
    for b in range(W1_BLK // PERM_W):
        cols = slice(b * PERM_W, (b + 1) * PERM_W)
        o_ref[0, :, cols] = _dot(w_ref[0, :, cols].astype(BF16), p_ref[...]).astype(BF16)


def _w1_prep(w1):
    e, d, n = w1.shape
    pm = np.zeros((PERM_W, PERM_W), np.float32)
    jj = np.arange(PERM_W // 2)
    pm[2 * jj, jj] = 1.0
    pm[2 * jj + 1, PERM_W // 2 + jj] = 1.0
    return pl.pallas_call(
        _w1_prep_kernel,
        grid=(e, n // W1_BLK),
        in_specs=[pl.BlockSpec((1, d, W1_BLK), lambda i, j: (i, 0, j)),
                  pl.BlockSpec((PERM_W, PERM_W), lambda i, j: (0, 0))],
        out_specs=pl.BlockSpec((1, d, W1_BLK), lambda i, j: (i, 0, j)),
        out_shape=jax.ShapeDtypeStruct((e, d, n), BF16),
        compiler_params=_cparams(("arbitrary", "arbitrary")),
        name="moe_w1_prep",
    )(w1, jnp.asarray(pm, BF16))


def _ffn_kernel(we_ref, row0_ref, nt_ref, nv_ref, xs_hbm, w1_ref, b1_ref, w2_ref, b2_ref,
                ys_hbm, x_ref, acc_ref, w2b_ref, stage_ref, sem_in, sem_out):
    w = pl.program_id(0)
    f = pl.program_id(1)
    nf = pl.num_programs(1)
    nt = nt_ref[w]
    row0 = row0_ref[w]
    tm = FFN_TM
    hw = PERM_W // 2

    @pl.when(jnp.logical_and(f == 0, nt > 0))
    def _():
        def tile_copy(t):
            return pltpu.make_async_copy(xs_hbm.at[pl.ds(pl.multiple_of(row0 + t * tm, tm), tm), :],
                                         x_ref.at[pl.ds(pl.multiple_of(t * tm, tm), tm), :], sem_in)

        def start(t, c):
            tile_copy(t).start()
            return c

        def wait(t, c):
            tile_copy(t).wait()
            return c

        lax.fori_loop(0, nt, start, 0)
        lax.fori_loop(0, nt, wait, 0)

    def out_copy(t, slot):
        return pltpu.make_async_copy(stage_ref.at[slot], ys_hbm.at[_prow(row0 + t * tm, tm), :], sem_out.at[slot])

    @pl.when(nt > 0)
    def _():
        w2b_ref[...] = w2_ref[0].astype(BF16)

        @pl.when(f == 0)
        def _():
            def init(t, c):
                acc_ref[pl.ds(pl.multiple_of(t * tm, tm), tm), :] = jnp.broadcast_to(b2_ref[0], (tm, D_MODEL))
                return c
            lax.fori_loop(0, nt, init, 0)

        def tile(xv, av):
            a = _dot(xv[...], w1_ref[0]) + b1_ref[0]
            nblk = a.shape[1] // PERM_W
            glu = jnp.concatenate([a[:, b * PERM_W:b * PERM_W + hw] for b in range(nblk)], axis=1)
            lin = jnp.concatenate([a[:, b * PERM_W + hw:(b + 1) * PERM_W] for b in range(nblk)], axis=1)
            glu = jnp.minimum(glu, SWIGLU_LIMIT)
            lin = jnp.clip(lin, -SWIGLU_LIMIT, SWIGLU_LIMIT)
            act = glu * jax.nn.sigmoid(SWIGLU_ALPHA * glu) * (lin + 1.0)
            av[...] = av[...] + _dot(act.astype(BF16), w2b_ref[...])

        def pair(p, c):
            win = pl.ds(pl.multiple_of(p * 2 * tm, 2 * tm), 2 * tm)
            xw, aw = x_ref.at[win, :], acc_ref.at[win, :]
            for h in range(2):
                tile(xw.at[h * tm:(h + 1) * tm, :], aw.at[h * tm:(h + 1) * tm, :])
            return c

        lax.fori_loop(0, nt // 2, pair, 0)

        @pl.when(nt % 2 == 1)
        def _():
            last = pl.ds(pl.multiple_of((nt - 1) * tm, tm), tm)
            tile(x_ref.at[last, :], acc_ref.at[last, :])

        @pl.when(f == nf - 1)
        def _():
            def emit(t, c):
                slot = t % 2

                @pl.when(t >= 2)
                def _():
                    out_copy(t - 2, slot).wait()

                _pack_store(stage_ref, acc_ref[pl.ds(pl.multiple_of(t * tm, tm), tm), :], lead=(slot,))
                out_copy(t, slot).start()
                return c

            lax.fori_loop(0, nt, emit, 0)
            for back in (2, 1):
                @pl.when(nt >= back)
                def _():
                    out_copy(nt - back, (nt - back) % 2).wait()

    @pl.when(jnp.logical_and(w == pl.num_programs(0) - 1, f == nf - 1))
    def _():
        used = nv_ref[1]
        ntail = (ys_hbm.shape[0] // PACK - used) // tm
        stage_ref[0] = jnp.zeros(stage_ref.shape[1:], U32)

        def fill(t, c):
            cp = pltpu.make_async_copy(stage_ref.at[0], ys_hbm.at[_prow(used + t * tm, tm), :], sem_out.at[0])
            cp.start()
            cp.wait()
            return c

        lax.fori_loop(0, ntail, fill, 0)


def _ffn(xs, w1p, b1p, w2b, b2, we, row0, ntile, nvalid):
    ns, d = xs.shape
    nf = EXPERT_FF // FFN_FC
    wmax = we.shape[0]

    def wmap(axis):
        def index(w, f, we_r, row0_r, nt_r, nv_r):
            fe = jnp.where(w < nv_r[0], f, nf - 1)
            return (we_r[w], 0, fe) if axis == 2 else (we_r[w], fe, 0)
        return index

    grid_spec = pltpu.PrefetchScalarGridSpec(
        num_scalar_prefetch=4,
        grid=(wmax, nf),
        in_specs=[pl.BlockSpec(memory_space=pl.ANY),
                  pl.BlockSpec((1, d, 2 * FFN_FC), wmap(2)),
                  pl.BlockSpec((1, 1, 2 * FFN_FC), wmap(2)),
                  pl.BlockSpec((1, FFN_FC, d), wmap(1)),
                  pl.BlockSpec((1, 1, d), lambda w, f, we_r, *_: (we_r[w], 0, 0))],
        out_specs=pl.BlockSpec(memory_space=pl.ANY),
        scratch_shapes=[pltpu.VMEM((FFN_R, d), BF16),
                        pltpu.VMEM((FFN_R, d), F32),
                        pltpu.VMEM((FFN_FC, d), BF16),
                        pltpu.VMEM((2, FFN_TM * PACK, LANES), U32),
                        pltpu.SemaphoreType.DMA(()),
                        pltpu.SemaphoreType.DMA((2,))],
    )
    return pl.pallas_call(
        _ffn_kernel,
        grid_spec=grid_spec,
        out_shape=jax.ShapeDtypeStruct((ns * PACK, LANES), U32),
        compiler_params=_cparams(("arbitrary", "arbitrary")),
        name="moe_ffn",
    )(we, row0, ntile, nvalid, xs, w1p, b1p, w2b, b2)


def _combine_kernel(cur_ref, nxt_ref, ys_hbm, gates_ref, x_ref, gmod_ref, fg_ref, o_ref, buf_ref, sem, *, final):
    i = pl.program_id(0)
    n = x_ref.shape[0]
    slot = i % 2

    def row_copy(idx_ref, b, k, r):
        src = ys_hbm.at[_prow(idx_ref[0, 0, k * n + r]), :]
        return pltpu.make_async_copy(src, buf_ref.at[b, k, _prow(r), :], sem.at[b])

    def issue(idx_ref, b):
        for k in range(TOP_K):
            def body(j, c):
                for par in range(2):
                    row_copy(idx_ref, b, k, 2 * j + par).start(priority=par)
                return c
            lax.fori_loop(0, n // 2, body, 0, unroll=4)

    @pl.when(i == 0)
    def _():
        issue(cur_ref, 0)

    @pl.when(i + 1 < pl.num_programs(0))
    def _():
        issue(nxt_ref, 1 - slot)

    for k in range(TOP_K):
        def wait(r, c):
            row_copy(cur_ref, slot, k, r).wait()
            return c
        lax.fori_loop(0, n, wait, 0, unroll=8)

    gates = gates_ref[...]
    gk = [gates[:, k:k + 1] for k in range(TOP_K)]
    half = D_MODEL // 2
    for s in range(PACK):
        lo = hi = None
        for k in range(TOP_K):
            lo_k, hi_k = _unpack_load(buf_ref, n, s, lead=(slot, k))
            lo = gk[k] * lo_k if lo is None else lo + gk[k] * lo_k
            hi = gk[k] * hi_k if hi is None else hi + gk[k] * hi_k
        for base, acc in ((s * LANES, lo), (half + s * LANES, hi)):
            cols = slice(base, base + LANES)
            o_ref[:, cols] = x_ref[:, cols] + gmod_ref[0, :, cols] * acc
    if final:
        o_ref[...] = _rms(o_ref[...], fg_ref[...])


def _combine(ys, dest, gates, x2, mod, final_g, seq, final):
    t, d = x2.shape
    tm = min(COMBINE_T, seq)
    tpb = seq // tm
    nt = t // tm
    idx = dest.reshape(nt, tm, TOP_K).transpose(0, 2, 1).reshape(nt, 1, TOP_K * tm)
    row = lambda w: pl.BlockSpec((tm, w), lambda i: (i, 0))
    return pl.pallas_call(
        functools.partial(_combine_kernel, final=final),
        grid=(nt,),
        in_specs=[pl.BlockSpec((1, 1, tm * TOP_K), lambda i: (i, 0, 0), memory_space=pltpu.SMEM),
                  pl.BlockSpec((1, 1, tm * TOP_K), lambda i: (jnp.minimum(i + 1, nt - 1), 0, 0),
                               memory_space=pltpu.SMEM),
                  pl.BlockSpec(memory_space=pl.ANY),
                  row(LANES), row(d), _mod_specs(tpb, 2),
                  pl.BlockSpec((1, d), lambda i: (0, 0))],
        out_specs=row(d),
        out_shape=jax.ShapeDtypeStruct((t, d), F32),
        scratch_shapes=[pltpu.VMEM((2, TOP_K, tm * PACK, LANES), U32), pltpu.SemaphoreType.DMA((2,))],
        compiler_params=_cparams(("arbitrary",)),
        name="moe_combine",
    )(idx, idx, ys, gates, x2, mod, final_g.reshape(1, -1))


def _moe_sublayer(x2, c_mod, norm_g, router_w, router_b, w1, b1, w2, b2, final_g, seq, final):
    t = x2.shape[0]
    h, route, gates, cnt = _route(x2, norm_g, c_mod, router_w, router_b, seq)
    top_e = route[:, :TOP_K]
    rank = route[:, TOP_K:2 * TOP_K]
    counts = cnt[0, :N_EXPERTS].astype(jnp.int32)
    padded = ((counts + FFN_TM - 1) // FFN_TM) * FFN_TM
    starts = jnp.cumsum(padded) - padded
    dest = starts[top_e] + rank
    ns = t * TOP_K + N_EXPERTS * FFN_TM
    ns = ((ns + GATHER_T - 1) // GATHER_T) * GATHER_T
    tok = jnp.broadcast_to(jnp.arange(t, dtype=jnp.int32)[:, None], (t, TOP_K))
    src_tok = jnp.zeros((ns,), jnp.int32).at[dest.reshape(-1)].set(tok.reshape(-1))
    n_items = (padded + FFN_R - 1) // FFN_R
    item_end = jnp.cumsum(n_items)
    wmax = (t * TOP_K) // FFN_R + N_EXPERTS
    wid = jnp.arange(wmax, dtype=jnp.int32)
    nvalid = item_end[-1].astype(jnp.int32)
    we = jnp.minimum(jnp.searchsorted(item_end, wid, side='right'), N_EXPERTS - 1).astype(jnp.int32)
    last_e = we[jnp.maximum(nvalid - 1, 0)]
    valid = wid < nvalid
    we = jnp.where(valid, we, last_e)
    local = wid - (item_end - n_items)[we]
    row0 = jnp.where(valid, starts[we] + local * FFN_R, 0).astype(jnp.int32)
    ntile = jnp.where(valid, jnp.minimum(FFN_R, padded[we] - local * FFN_R) // FFN_TM, 0).astype(jnp.int32)

    xs = _gather_rows(h, src_tok)
    hw = PERM_W // 2
    b1p = b1.reshape(N_EXPERTS, 2 * EXPERT_FF // PERM_W, hw, 2).transpose(0, 1, 3, 2).reshape(N_EXPERTS, 1, 2 * EXPERT_FF)
    ys = _ffn(xs, _w1_prep(w1), b1p, w2, b2.reshape(N_EXPERTS, 1, D_MODEL),
              we, row0, ntile, jnp.stack([nvalid, jnp.sum(padded).astype(jnp.int32)]))
    return _combine(ys, dest.astype(jnp.int32), gates, x2, c_mod, final_g, seq, final)


def _conv_pw1_kernel(x_ref, g_ref, shift_ref, scale_ref, wa_ref, wb_ref, ba_ref, bb_ref, o_ref, h_ref):
    j = pl.program_id(1)

    @pl.when(j == 0)
    def _():
        h_ref[...] = _norm_mod(x_ref[...], g_ref[...], scale_ref[0], shift_ref[0]).astype(BF16)

    h = h_ref[...]
    a = _dot(h, wa_ref[...]) + ba_ref[...]
    b = _dot(h, wb_ref[...]) + bb_ref[...]
    o_ref[...] = a * jax.nn.sigmoid(b)


def _conv_pw1(x2, g, mod, w_pw1, b_pw1, seq):
    t = x2.shape[0]
    tm = min(PROJ_TM, seq)
    tpb = seq // tm
    tn = 512
    nj = D_MODEL // tn
    mod2 = lambda which: pl.BlockSpec((1, 1, D_MODEL), lambda i, j: (i // tpb, 0, which))
    b2 = b_pw1.reshape(1, -1)
    return pl.pallas_call(
        _conv_pw1_kernel,
        grid=(t // tm, nj),
        in_specs=[pl.BlockSpec((tm, D_MODEL), lambda i, j: (i, 0)),
                  pl.BlockSpec((1, D_MODEL), lambda i, j: (0, 0)),
                  mod2(0), mod2(1),
                  pl.BlockSpec((D_MODEL, tn), lambda i, j: (0, j)),
                  pl.BlockSpec((D_MODEL, tn), lambda i, j: (0, j + nj)),
                  pl.BlockSpec((1, tn), lambda i, j: (0, j)),
                  pl.BlockSpec((1, tn), lambda i, j: (0, j + nj))],
        out_specs=pl.BlockSpec((tm, tn), lambda i, j: (i, j)),
        out_shape=jax.ShapeDtypeStruct((t, D_MODEL), F32),
        scratch_shapes=[pltpu.VMEM((tm, D_MODEL), BF16)],
        compiler_params=_cparams(("arbitrary", "arbitrary")),
        name="conv_pw1",
    )(x2, g.reshape(1, -1), mod, mod, w_pw1, w_pw1, b2, b2)


def _conv_dw_kernel(y_ref, halo_ref, wdw_ref, bdw_ref, lng_ref, lnb_ref, wp_ref, bp_ref, x_ref, gate_ref,
                    o_ref, buf_ref, z_ref, win_ref, *, tpb):
    i = pl.program_id(0)
    tm = y_ref.shape[0]
    first = (i % tpb) == 0
    halo = halo_ref[...]
    buf_ref[0:CONV_HALO, :] = jnp.where(first, 0.0, halo)
    buf_ref[CONV_HALO:, :] = y_ref[...]
    cw = CONV_CW
    rb = min(CONV_RB, tm)
    off = CONV_HALO - (CONV_KERNEL - 1)
    for c0 in range(0, D_MODEL, cw):
        for r0 in range(0, tm, rb):
            acc = jnp.zeros((rb, cw), F32)
            for j in range(SUBLANES):
                taps = [k for k in range(CONV_KERNEL) if (off + k) % SUBLANES == j]
                span = max(off + k for k in taps) - j + rb
                win_ref[0:span, :] = buf_ref[r0 + j:r0 + j + span, c0:c0 + cw]
                for k in taps:
                    q = off + k - j
                    acc = acc + wdw_ref[k:k + 1, c0:c0 + cw] * win_ref[q:q + rb, :]
            z_ref[r0:r0 + rb, c0:c0 + cw] = acc
    z = z_ref[...] + bdw_ref[...]
    mu = jnp.mean(z, axis=-1, keepdims=True)
    zc = z - mu
    var = jnp.mean(zc * zc, axis=-1, keepdims=True)
    zn = zc * lax.rsqrt(var + LN_EPS) * lng_ref[...] + lnb_ref[...]
    act = (zn * jax.nn.sigmoid(zn)).astype(BF16)
    o_ref[...] = x_ref[...] + gate_ref[0] * (_dot(act, wp_ref[...]) + bp_ref[...])


def _conv_dw(y1, x2, mod, w_dw, b_dw, ln_g, ln_b, w_pw2, b_pw2, seq):
    t = x2.shape[0]
    tm = min(CONV_TM, seq)
    tpb = seq // tm
    hb = tm // CONV_HALO
    wdw = jnp.zeros((CONV_HALO, D_MODEL), F32).at[:CONV_KERNEL].set(w_dw)
    row = lambda: pl.BlockSpec((tm, D_MODEL), lambda i: (i, 0))
    vec = lambda: pl.BlockSpec((1, D_MODEL), lambda i: (0, 0))
    return pl.pallas_call(
        functools.partial(_conv_dw_kernel, tpb=tpb),
        grid=(t // tm,),
        in_specs=[row(),
                  pl.BlockSpec((CONV_HALO, D_MODEL), lambda i: (jnp.maximum(i * hb - 1, 0), 0)),
                  pl.BlockSpec((CONV_HALO, D_MODEL), lambda i: (0, 0)),
                  vec(), vec(), vec(),
                  pl.BlockSpec((D_MODEL, D_MODEL), lambda i: (0, 0)),
                  vec(), row(), _mod_specs(tpb, 2)],
        out_specs=row(),
        out_shape=jax.ShapeDtypeStruct((t, D_MODEL), F32),
        scratch_shapes=[pltpu.VMEM((tm + CONV_HALO, D_MODEL), F32), pltpu.VMEM((tm, D_MODEL), F32),
                        pltpu.VMEM((CONV_RB + CONV_HALO, CONV_CW), F32)],
        compiler_params=_cparams(("arbitrary",)),
        name="conv_dw_pw2",
    )(y1, y1, wdw, b_dw.reshape(1, -1), ln_g.reshape(1, -1), ln_b.reshape(1, -1),
      w_pw2, b_pw2.reshape(1, -1), x2, mod)


def _rope_swap_cols(w, heads):
    k = w.shape[0]
    half = MLA_ROPE // 2
    w3 = w.reshape(k, heads, MLA_NOPE + MLA_ROPE)
    pe = w3[:, :, MLA_NOPE:]
    sw = jnp.concatenate([pe[:, :, half:], pe[:, :, :half]], axis=-1)
    return jnp.concatenate([w3, sw], axis=-1).reshape(k, heads * (MLA_NOPE + 2 * MLA_ROPE))


def kernel(x, c, positions, l0_mix_norm_g, l0_mix_ada_w, l0_mix_ada_b, l0_w_in, l0_s5_lambda_re, l0_s5_lambda_im, l0_s5_log_step, l0_s5_b_re, l0_s5_b_im, l0_s5_c_re, l0_s5_c_im, l0_s5_d, l0_s5_w_glu, l0_s5_b_glu, l0_mla_q_norm_g, l0_mla_w_uq, l0_mla_kv_norm_g, l0_mla_w_ukv, l0_w_out, l0_moe_norm_g, l0_moe_ada_w, l0_moe_ada_b, l0_router_w, l0_router_b, l0_exp_w1, l0_exp_b1, l0_exp_w2, l0_exp_b2, l1_mix_norm_g, l1_mix_ada_w, l1_mix_ada_b, l1_conv_w_pw1, l1_conv_b_pw1, l1_conv_w_dw, l1_conv_b_dw, l1_conv_ln_g, l1_conv_ln_b, l1_conv_w_pw2, l1_conv_b_pw2, l1_moe_norm_g, l1_moe_ada_w, l1_moe_ada_b, l1_router_w, l1_router_b, l1_exp_w1, l1_exp_b1, l1_exp_w2, l1_exp_b2, final_norm_g):
    nbatch, seq, d = x.shape
    t = nbatch * seq
    x2 = x.reshape(t, d)
    pos = positions.reshape(t, 1).astype(jnp.int32)

    mod = _ada_mod(c, l0_mix_ada_w, l0_mix_ada_b)
    half = MLA_ROPE // 2
    kcol = S5_WIDTH + MLA_Q_RANK + MLA_KV_RANK
    w_in_ext = jnp.concatenate([l0_w_in, l0_w_in[:, kcol + half:], l0_w_in[:, kcol:kcol + half]], axis=1).astype(BF16)
    u, cq, ckv, kr = _proj_in(x2, l0_mix_norm_g, mod, w_in_ext, seq)

    kt, wt, vv, a16 = _s5_prep(l0_s5_lambda_re, l0_s5_lambda_im, l0_s5_log_step, l0_s5_b_re, l0_s5_b_im,
                               l0_s5_c_re, l0_s5_c_im, l0_s5_d)
    nchunk = seq // S5_CHUNK
    u_r = (u.reshape(nbatch, nchunk, S5_CHUNK, S5_GROUPS, S5_GROUP_CH)
           .transpose(3, 0, 1, 2, 4).reshape(S5_GROUPS, nbatch * nchunk, S5_ROW))
    y_r = _s5_main(u_r, kt, wt, vv, a16, nbatch, nchunk)
    ys = (y_r.reshape(S5_GROUPS, nbatch, nchunk, S5_CHUNK, S5_GROUP_CH)
          .transpose(1, 2, 3, 0, 4).reshape(t, S5_WIDTH))

    wq_ext = _rope_swap_cols(l0_mla_w_uq, MLA_HEADS).astype(BF16)
    q, k, v = _mla_proj(cq, ckv, kr, pos, l0_mla_q_norm_g, l0_mla_kv_norm_g, wq_ext,
                        l0_mla_w_ukv.astype(BF16), nbatch, seq)
    ym = _flash_attention(q, k, v).reshape(t, MLA_HEADS * MLA_V)
    x2 = _mixer_out(ys, ym, x2, mod, l0_s5_w_glu.astype(BF16), l0_s5_b_glu, l0_w_out.astype(BF16), seq)

    mod = _ada_mod(c, l0_moe_ada_w, l0_moe_ada_b)
    x2 = _moe_sublayer(x2, mod, l0_moe_norm_g, l0_router_w, l0_router_b, l0_exp_w1, l0_exp_b1,
                       l0_exp_w2, l0_exp_b2, final_norm_g, seq, final=False)

    mod = _ada_mod(c, l1_mix_ada_w, l1_mix_ada_b)
    y1 = _conv_pw1(x2, l1_mix_norm_g, mod, l1_conv_w_pw1.astype(BF16), l1_conv_b_pw1, seq)
    x2 = _conv_dw(y1, x2, mod, l1_conv_w_dw, l1_conv_b_dw, l1_conv_ln_g, l1_conv_ln_b,
                  l1_conv_w_pw2.astype(BF16), l1_conv_b_pw2, seq)

    mod = _ada_mod(c, l1_moe_ada_w, l1_moe_ada_b)
    x2 = _moe_sublayer(x2, mod, l1_moe_norm_g, l1_router_w, l1_router_b, l1_exp_w1, l1_exp_b1,
                       l1_exp_w2, l1_exp_b2, final_norm_g, seq, final=True)
    return x2.reshape(nbatch, seq, d)
```

```python
import functools
import math

import numpy as np
import jax
import jax.numpy as jnp
from jax import lax
from jax.experimental import pallas as pl
from jax.experimental.pallas import tpu as pltpu

F32 = jnp.float32
BF16 = jnp.bfloat16

D_MODEL = 2048
S5_WIDTH = 1024
S5_GROUP_CH = 16
S5_GROUPS = 64
S5_STATE = 64
MLA_HEADS = 8
MLA_NOPE = 128
MLA_ROPE = 64
MLA_V = 128
MLA_Q_RANK = 512
MLA_KV_RANK = 512
ROPE_THETA = 10000.0
CONV_KERNEL = 31
N_EXPERTS = 32
TOP_K = 4
EXPERT_FF = 2048
SWIGLU_ALPHA = 1.702
SWIGLU_LIMIT = 7.0
EPS = 1e-6
LN_EPS = 1e-5
NEG_BIG = -1e30

LANES = 128
SUBLANES = 8
VMEM_LIMIT_BYTES = 56 * 1024 * 1024

S5_CHUNK = 16
S5_ROW = S5_CHUNK * S5_GROUP_CH
PROJ_TM = 512
MLA_TM = 256
ATT_T = 512
OUT_TM = 256
ROUTE_TM = 512
CONV_TM = 256
CONV_HALO = 32
CONV_RB = 128
CONV_CW = 256
FFN_TM = 256
FFN_R = 2304
FFN_FC = 512
GATHER_T = 512
DISPATCH_T = 512
COMBINE_T = 256


def _cparams(sem):
    return pltpu.CompilerParams(dimension_semantics=sem, vmem_limit_bytes=VMEM_LIMIT_BYTES)


def _norm_mod(x, g, scale, shift):
    ms = jnp.mean(x * x, axis=-1, keepdims=True)
    return (x * lax.rsqrt(ms + EPS)) * g * (1.0 + scale) + shift


def _rms(x, g):
    ms = jnp.mean(x * x, axis=-1, keepdims=True)
    return (x * lax.rsqrt(ms + EPS)) * g


def _dot(a, b):
    return jnp.dot(a, b, preferred_element_type=F32)


def _dot_nt(a, b):
    return lax.dot_general(a, b, (((1,), (1,)), ((), ())), preferred_element_type=F32)


U32 = jnp.uint32
PACK = D_MODEL // (2 * LANES)
HI_MASK = 0xFFFF0000


def _pack_store(ref, val, lead=()):
    n = val.shape[0]
    half = D_MODEL // 2
    for s in range(PACK):
        lo = val[:, s * LANES:(s + 1) * LANES].astype(BF16).astype(F32)
        hi = val[:, half + s * LANES:half + (s + 1) * LANES].astype(BF16).astype(F32)
        word = (lax.bitcast_convert_type(lo, U32) >> 16) | (lax.bitcast_convert_type(hi, U32) & jnp.uint32(HI_MASK))
        ref[lead + (pl.ds(s, n, stride=PACK), slice(None))] = word


def _unpack_load(ref, n, s, lead=()):
    word = ref[lead + (pl.ds(s, n, stride=PACK), slice(None))]
    lo = lax.bitcast_convert_type(word << 16, F32)
    hi = lax.bitcast_convert_type(word & jnp.uint32(HI_MASK), F32)
    return lo, hi


def _prow(r, n=1):
    return pl.ds(pl.multiple_of(r * PACK, PACK), n * PACK)


ADA_TN = 768
ADA_KC = 256


def _ada_kernel(ct_ref, w_ref, b_ref, o_ref):
    nb = ct_ref.shape[0]
    kdim = w_ref.shape[0]
    rows = []
    for b in range(nb):
        acc = jnp.zeros((1, w_ref.shape[1]), F32)
        for k0 in range(0, kdim, ADA_KC):
            c = ct_ref[b, k0:k0 + ADA_KC, :]
            cs = c * jax.nn.sigmoid(c)
            acc = acc + jnp.sum(w_ref[k0:k0 + ADA_KC, :] * cs, axis=0, keepdims=True)
        rows.append(acc)
    o_ref[...] = jnp.concatenate(rows, axis=0) + b_ref[...]


def _ada_mod(c, w, b):
    nb, d = c.shape
    n = w.shape[1]
    ct = c.reshape(nb, d, 1)
    m = pl.pallas_call(
        _ada_kernel,
        grid=(n // ADA_TN,),
        in_specs=[pl.BlockSpec((nb, d, 1), lambda j: (0, 0, 0)),
                  pl.BlockSpec((d, ADA_TN), lambda j: (0, j)),
                  pl.BlockSpec((1, ADA_TN), lambda j: (0, j))],
        out_specs=pl.BlockSpec((nb, ADA_TN), lambda j: (0, j)),
        out_shape=jax.ShapeDtypeStruct((nb, n), F32),
        compiler_params=_cparams(("arbitrary",)),
        name="ada_mod",
    )(ct, w, b.reshape(1, n))
    return m.reshape(nb, 1, n)


def _mod_specs(tiles_per_batch, which):
    return pl.BlockSpec((1, 1, D_MODEL), lambda i, *_: (i // tiles_per_batch, 0, which))


def _proj_in_kernel(x_ref, g_ref, shift_ref, scale_ref, w_ref, u_ref, cq_ref, ckv_ref, kr_ref):
    h = _norm_mod(x_ref[...], g_ref[...], scale_ref[0], shift_ref[0]).astype(BF16)
    acc = _dot(h, w_ref[...])
    c0, c1, c2 = S5_WIDTH, S5_WIDTH + MLA_Q_RANK, S5_WIDTH + MLA_Q_RANK + MLA_KV_RANK
    u_ref[...] = acc[:, :c0].astype(BF16)
    cq_ref[...] = acc[:, c0:c1]
    ckv_ref[...] = acc[:, c1:c2]
    kr_ref[...] = acc[:, c2:]


def _proj_in(x2, g, mod, w_ext, seq):
    t = x2.shape[0]
    tm = min(PROJ_TM, seq)
    tpb = seq // tm
    n = w_ext.shape[1]
    row = lambda w: pl.BlockSpec((tm, w), lambda i: (i, 0))
    return pl.pallas_call(
        _proj_in_kernel,
        grid=(t // tm,),
        in_specs=[row(D_MODEL),
                  pl.BlockSpec((1, D_MODEL), lambda i: (0, 0)),
                  _mod_specs(tpb, 0), _mod_specs(tpb, 1),
                  pl.BlockSpec((D_MODEL, n), lambda i: (0, 0))],
        out_specs=[row(S5_WIDTH), row(MLA_Q_RANK), row(MLA_KV_RANK), row(2 * MLA_ROPE)],
        out_shape=[jax.ShapeDtypeStruct((t, S5_WIDTH), BF16),
                   jax.ShapeDtypeStruct((t, MLA_Q_RANK), F32),
                   jax.ShapeDtypeStruct((t, MLA_KV_RANK), F32),
                   jax.ShapeDtypeStruct((t, 2 * MLA_ROPE), F32)],
        compiler_params=_cparams(("arbitrary",)),
        name="proj_in",
    )(x2, g.reshape(1, -1), mod, mod, w_ext)


def _s5_prep_kernel(lamc_ref, lamr_ref, step_ref, bt_ref, btt_ref, ct_ref, d_ref,
                    kt_ref, wt_ref, v_ref, a16_ref):
    P, H, C = S5_STATE, S5_GROUP_CH, S5_CHUNK
    step = step_ref[0]
    step = jnp.exp(step)
    lr_c = jnp.minimum(lamc_ref[0, 0], -1e-4)
    li_c = lamc_ref[0, 1]
    lr_r = jnp.minimum(lamr_ref[0, 0:1, :], -1e-4)
    li_r = lamr_ref[0, 1:2, :]

    def ratio(lr, li):
        mag = jnp.exp(lr * step)
        ab_re = mag * jnp.cos(li * step)
        ab_im = mag * jnp.sin(li * step)
        denom = lr * lr + li * li
        nr, ni = ab_re - 1.0, ab_im
        return (nr * lr + ni * li) / denom, (ni * lr - nr * li) / denom

    rr_c, ri_c = ratio(lr_c, li_c)
    rr_r, ri_r = ratio(lr_r, li_r)

    lane = lax.broadcasted_iota(jnp.int32, (1, S5_ROW), 1)
    kk = (lane // H).astype(F32)

    def powers(k):
        mag = jnp.exp(lr_c * step * k)
        return mag * jnp.cos(li_c * step * k), mag * jnp.sin(li_c * step * k)

    bre_t, bim_t = bt_ref[0, 0], bt_ref[0, 1]
    bbt_re = rr_c * bre_t - ri_c * bim_t
    bbt_im = rr_c * bim_t + ri_c * bre_t
    cre_t, cim_t = ct_ref[0, 0], ct_ref[0, 1]

    er, ei = powers(float(C - 1) - kk)
    wt_ref[0, 0:P, :] = (er * bbt_re - ei * bbt_im).astype(BF16)
    wt_ref[0, P:2 * P, :] = (er * bbt_im + ei * bbt_re).astype(BF16)

    er, ei = powers(kk + 1.0)
    v_ref[0, 0:P, :] = (cre_t * er - cim_t * ei).astype(BF16)
    v_ref[0, P:2 * P, :] = (-cre_t * ei - cim_t * er).astype(BF16)

    er, ei = powers(kk)
    q_re = er * cre_t - ei * cim_t
    q_im = er * cim_t + ei * cre_t
    brt, bit = btt_ref[0, 0], btt_ref[0, 1]
    bbr = rr_r * brt - ri_r * bit
    bbi = rr_r * bit + ri_r * brt
    hi = lax.Precision.HIGHEST
    mall = (jnp.dot(bbr, q_re, precision=hi, preferred_element_type=F32)
            - jnp.dot(bbi, q_im, precision=hi, preferred_element_type=F32))
    rowh = lax.broadcasted_iota(jnp.int32, (H, S5_ROW), 0)
    laneh = lax.broadcasted_iota(jnp.int32, (H, S5_ROW), 1)
    mall = mall + jnp.where(laneh == rowh, d_ref[0], 0.0)
    for s in range(C):
        piece = mall if s == 0 else pltpu.roll(mall, H * s, 1)
        piece = jnp.where(laneh >= H * s, piece, 0.0)
        kt_ref[0, s * H:(s + 1) * H, :] = piece.astype(BF16)

    mag = jnp.exp(lr_r * step * float(C))
    a16_ref[0, 0:1, :] = mag * jnp.cos(li_r * step * float(C))
    a16_ref[0, 1:2, :] = mag * jnp.sin(li_r * step * float(C))


def _s5_prep(lam_re, lam_im, log_step, b_re, b_im, c_re, c_im, d):
    G, P, H, C = S5_GROUPS, S5_STATE, S5_GROUP_CH, S5_CHUNK
    lam = jnp.stack([lam_re, lam_im], axis=1).astype(F32)
    lamc = lam.reshape(G, 2, P, 1)
    b = jnp.stack([b_re, b_im], axis=1).astype(F32)
    bt = jnp.tile(b, (1, 1, 1, C))
    btt = jnp.swapaxes(b, 2, 3)
    c = jnp.stack([c_re, c_im], axis=1).astype(F32)
    ct = jnp.tile(jnp.swapaxes(c, 2, 3), (1, 1, 1, C))
    dt = jnp.tile(d.astype(F32), (1, C)).reshape(G, 1, S5_ROW)
    step = log_step.astype(F32).reshape(G, 1, 1)
    g4 = lambda *shape: pl.BlockSpec((1,) + shape, lambda g: (g,) + (0,) * len(shape))
    return pl.pallas_call(
        _s5_prep_kernel,
        grid=(G,),
        in_specs=[g4(2, P, 1), g4(2, P), g4(1, 1), g4(2, P, S5_ROW), g4(2, H, P),
                  g4(2, P, S5_ROW), g4(1, S5_ROW)],
        out_specs=[g4(S5_ROW, S5_ROW), g4(2 * P, S5_ROW), g4(2 * P, S5_ROW), g4(2, P)],
        out_shape=[jax.ShapeDtypeStruct((G, S5_ROW, S5_ROW), BF16),
                   jax.ShapeDtypeStruct((G, 2 * P, S5_ROW), BF16),
                   jax.ShapeDtypeStruct((G, 2 * P, S5_ROW), BF16),
                   jax.ShapeDtypeStruct((G, 2, P), F32)],
        compiler_params=_cparams(("arbitrary",)),
        name="s5_prep",
    )(lamc, lam, step, bt, btt, ct, dt)


def _s5_main_kernel(u_ref, kt_ref, wt_ref, v_ref, a16_ref, y_ref, sr_ref, si_ref, xr_ref, xi_ref,
                    *, nbatch, nchunk):
    P = S5_STATE
    u = u_ref[0]
    sr_ref[...] = _dot_nt(u, wt_ref[0, 0:P, :])
    si_ref[...] = _dot_nt(u, wt_ref[0, P:2 * P, :])
    ar = a16_ref[0, 0:1, :]
    ai = a16_ref[0, 1:2, :]

    def step(c, carry):
        out = []
        for b in range(nbatch):
            xr, xi = carry[2 * b], carry[2 * b + 1]
            r = b * nchunk + c
            xr_ref[pl.ds(r, 1), :] = xr
            xi_ref[pl.ds(r, 1), :] = xi
            nxr = ar * xr - ai * xi + sr_ref[pl.ds(r, 1), :]
            nxi = ar * xi + ai * xr + si_ref[pl.ds(r, 1), :]
            out += [nxr, nxi]
        return tuple(out)

    zero = jnp.zeros((1, P), F32)
    lax.fori_loop(0, nchunk, step, (zero,) * (2 * nbatch))
    y = _dot(u, kt_ref[0])
    y = y + _dot(xr_ref[...].astype(BF16), v_ref[0, 0:P, :])
    y = y + _dot(xi_ref[...].astype(BF16), v_ref[0, P:2 * P, :])
    y_ref[0] = y.astype(y_ref.dtype)


def _s5_main(u_r, kt, wt, v, a16, nbatch, nchunk):
    G, rows, _ = u_r.shape
    P = S5_STATE
    g3 = lambda a, b: pl.BlockSpec((1, a, b), lambda g: (g, 0, 0))
    return pl.pallas_call(
        functools.partial(_s5_main_kernel, nbatch=nbatch, nchunk=nchunk),
        grid=(G,),
        in_specs=[g3(rows, S5_ROW), g3(S5_ROW, S5_ROW), g3(2 * P, S5_ROW), g3(2 * P, S5_ROW), g3(2, P)],
        out_specs=g3(rows, S5_ROW),
        out_shape=jax.ShapeDtypeStruct((G, rows, S5_ROW), BF16),
        scratch_shapes=[pltpu.VMEM((rows, P), F32) for _ in range(4)],
        compiler_params=_cparams(("arbitrary",)),
        name="s5_main",
    )(u_r, kt, wt, v, a16)


def _mla_proj_kernel(cq_ref, ckv_ref, kr_ref, pos_ref, qg_ref, kvg_ref, wq_ref, wkv_ref,
                     invf_ref, sgn_ref, q_ref, k_ref, v_ref):
    qscale = 1.0 / math.sqrt(MLA_NOPE + MLA_ROPE)
    qa = _dot(_rms(cq_ref[...], qg_ref[...]).astype(BF16), wq_ref[...])
    kva = _dot(_rms(ckv_ref[...], kvg_ref[...]).astype(BF16), wkv_ref[...])
    ang = pos_ref[...].astype(F32) * invf_ref[...]
    cc = jnp.cos(ang)
    ss = jnp.sin(ang) * sgn_ref[...]

    def rope(slab):
        return slab * cc + pltpu.roll(slab, MLA_ROPE, 1) * ss

    kpe = rope(kr_ref[...])[:, :MLA_ROPE].astype(BF16)
    hw = MLA_NOPE + 2 * MLA_ROPE
    lane = lax.broadcasted_iota(jnp.int32, (cq_ref.shape[0], MLA_V), 1)
    ones_col = jnp.where(lane == 0, 1.0, 0.0).astype(BF16)
    for h in range(MLA_HEADS):
        blk = qa[:, h * hw:(h + 1) * hw]
        q_ref[0, h, :, 0:MLA_NOPE] = (blk[:, :MLA_NOPE] * qscale).astype(BF16)
        qpe = rope(blk[:, MLA_NOPE:]) * qscale
        q_ref[0, h, :, MLA_NOPE:MLA_NOPE + MLA_ROPE] = qpe[:, :MLA_ROPE].astype(BF16)
        kvb = kva[:, h * hw:(h + 1) * hw]
        k_ref[0, h, :, 0:MLA_NOPE] = kvb[:, :MLA_NOPE].astype(BF16)
        k_ref[0, h, :, MLA_NOPE:MLA_NOPE + MLA_ROPE] = kpe
        v_ref[0, h, :, 0:MLA_V] = kvb[:, MLA_NOPE:].astype(BF16)
        v_ref[0, h, :, MLA_V:2 * MLA_V] = ones_col


def _mla_proj(cq, ckv, kr, pos, qg, kvg, wq_ext, wkv, nbatch, seq):
    tm = min(MLA_TM, seq)
    nl = seq // tm
    dqk = MLA_NOPE + MLA_ROPE
    half = MLA_ROPE // 2
    inv_freq = 1.0 / (ROPE_THETA ** (jnp.arange(0, MLA_ROPE, 2, dtype=F32) / MLA_ROPE))
    invf = jnp.tile(inv_freq, 4).reshape(1, 2 * MLA_ROPE)
    sgn = jnp.tile(jnp.concatenate([-jnp.ones((half,), F32), jnp.ones((half,), F32)]), 2).reshape(1, 2 * MLA_ROPE)
    row = lambda w: pl.BlockSpec((tm, w), lambda b, i: (b * nl + i, 0))
    full = lambda a, b_: pl.BlockSpec((a, b_), lambda b, i: (0, 0))
    head = lambda w: pl.BlockSpec((1, MLA_HEADS, tm, w), lambda b, i: (b, 0, i, 0))
    return pl.pallas_call(
        _mla_proj_kernel,
        grid=(nbatch, nl),
        in_specs=[row(MLA_Q_RANK), row(MLA_KV_RANK), row(2 * MLA_ROPE), row(1),
                  full(1, MLA_Q_RANK), full(1, MLA_KV_RANK),
                  full(MLA_Q_RANK, wq_ext.shape[1]), full(MLA_KV_RANK, wkv.shape[1]),
                  full(1, 2 * MLA_ROPE), full(1, 2 * MLA_ROPE)],
        out_specs=[head(dqk), head(dqk), head(2 * MLA_V)],
        out_shape=[jax.ShapeDtypeStruct((nbatch, MLA_HEADS, seq, dqk), BF16),
                   jax.ShapeDtypeStruct((nbatch, MLA_HEADS, seq, dqk), BF16),
                   jax.ShapeDtypeStruct((nbatch, MLA_HEADS, seq, 2 * MLA_V), BF16)],
        compiler_params=_cparams(("arbitrary", "arbitrary")),
        name="mla_proj",
    )(cq, ckv, kr, pos, qg.reshape(1, -1), kvg.reshape(1, -1), wq_ext, wkv, invf, sgn)


def _lane_groups(x):
    return [x[:, j * LANES:(j + 1) * LANES] for j in range(x.shape[1] // LANES)]


def _flash_kernel(q_ref, k_ref, v_ref, o_ref, s_ref, *, t):
    qi = pl.program_id(2)
    qh = [q_ref[0, 0, 0:t, :], q_ref[0, 0, t:2 * t, :]]

    def kblk(c):
        return k_ref[0, 0, pl.ds(pl.multiple_of(c * t, t), t), :]

    def vblk(c):
        return v_ref[0, 0, pl.ds(pl.multiple_of(c * t, t), t), :]

    def fold_max(s, mx):
        return functools.reduce(jnp.maximum, _lane_groups(s), mx)

    def pass1(c, mx):
        kc = kblk(c)
        out = []
        for h in range(2):
            s = _dot_nt(qh[h], kc)
            s_ref[h, c] = s
            out.append(fold_max(s, mx[h]))
        return tuple(out)

    neg = jnp.full((t, LANES), NEG_BIG, F32)
    mx0, mx1 = lax.fori_loop(0, 2 * qi, pass1, (neg, neg))
    c0, c1 = 2 * qi, 2 * qi + 1
    row = lax.broadcasted_iota(jnp.int32, (t, t), 0)
    col = lax.broadcasted_iota(jnp.int32, (t, t), 1)
    tri = col <= row
    k0, k1 = kblk(c0), kblk(c1)
    s00 = jnp.where(tri, _dot_nt(qh[0], k0), NEG_BIG)
    s10 = _dot_nt(qh[1], k0)
    s11 = jnp.where(tri, _dot_nt(qh[1], k1), NEG_BIG)
    s_ref[0, c0] = s00
    s_ref[1, c0] = s10
    s_ref[1, c1] = s11
    mx0 = fold_max(s00, mx0)
    mx1 = fold_max(s11, fold_max(s10, mx1))
    m = [jnp.broadcast_to(jnp.max(mx, axis=-1, keepdims=True), (t, LANES)) for mx in (mx0, mx1)]

    mt = [jnp.concatenate([mh] * (t // LANES), axis=1) for mh in m]

    def probs(h, c):
        return jnp.exp(s_ref[h, c] - mt[h]).astype(BF16)

    def pass2(j, carry):
        c = 2 * j
        vc = v_ref[0, 0, pl.ds(pl.multiple_of(c * t, 2 * t), 2 * t), :]
        out = []
        for h in range(2):
            p = jnp.concatenate([probs(h, c), probs(h, c + 1)], axis=1)
            out.append(carry[h] + _dot(p, vc))
        return tuple(out)

    za = jnp.zeros((t, 2 * MLA_V), F32)
    a0, a1 = lax.fori_loop(0, qi, pass2, (za, za))
    v0, v1 = vblk(c0), vblk(c1)
    a0 = a0 + _dot(probs(0, c0), v0)
    a1 = a1 + _dot(jnp.concatenate([probs(1, c0), probs(1, c1)], axis=1), jnp.concatenate([v0, v1], axis=0))
    o_ref[0, 0:t, :] = (a0[:, :MLA_V] / a0[:, MLA_V:MLA_V + 1]).astype(o_ref.dtype)
    o_ref[0, t:2 * t, :] = (a1[:, :MLA_V] / a1[:, MLA_V:MLA_V + 1]).astype(o_ref.dtype)


def _flash_attention(q, k, v):
    nbatch, nh, seq, dqk = q.shape
    t = min(ATT_T, seq // 2)
    nq = seq // (2 * t)
    return pl.pallas_call(
        functools.partial(_flash_kernel, t=t),
        grid=(nbatch, nh, nq),
        in_specs=[pl.BlockSpec((1, 1, 2 * t, dqk), lambda b, h, i: (b, h, i, 0)),
                  pl.BlockSpec((1, 1, seq, dqk), lambda b, h, i: (b, h, 0, 0)),
                  pl.BlockSpec((1, 1, seq, 2 * MLA_V), lambda b, h, i: (b, h, 0, 0))],
        out_specs=pl.BlockSpec((1, 2 * t, MLA_V), lambda b, h, i: (b, i, h)),
        out_shape=jax.ShapeDtypeStruct((nbatch, seq, nh * MLA_V), BF16),
        scratch_shapes=[pltpu.VMEM((2, seq // t, t, t), F32)],
        compiler_params=_cparams(("arbitrary", "arbitrary", "arbitrary")),
        name="flash_attn",
    )(q, k, v)


def _gelu_tanh(x):
    c = math.sqrt(2.0 / math.pi)
    return 0.5 * x * (1.0 + jnp.tanh(c * (x + 0.044715 * (x * x * x))))


def _mixer_out_kernel(ys_ref, ym_ref, x_ref, gate_ref, wglu_ref, bglu_ref, wo_ref, o_ref):
    y = _gelu_tanh(ys_ref[...].astype(F32))
    g = _dot(y.astype(BF16), wglu_ref[...]) + bglu_ref[...]
    s5o = (y * jax.nn.sigmoid(g)).astype(BF16)
    acc = _dot(s5o, wo_ref[0:S5_WIDTH, :]) + _dot(ym_ref[...], wo_ref[S5_WIDTH:, :])
    o_ref[...] = x_ref[...] + gate_ref[0] * acc


def _mixer_out(ys, ym, x2, mod, w_glu, b_glu, w_out, seq):
    t = x2.shape[0]
    tm = min(OUT_TM, seq)
    tpb = seq // tm
    row = lambda w: pl.BlockSpec((tm, w), lambda i: (i, 0))
    full = lambda a, b: pl.BlockSpec((a, b), lambda i: (0, 0))
    return pl.pallas_call(
        _mixer_out_kernel,
        grid=(t // tm,),
        in_specs=[row(S5_WIDTH), row(MLA_HEADS * MLA_V), row(D_MODEL), _mod_specs(tpb, 2),
                  full(S5_WIDTH, S5_WIDTH), full(1, S5_WIDTH), full(2 * S5_WIDTH, D_MODEL)],
        out_specs=row(D_MODEL),
        out_shape=jax.ShapeDtypeStruct((t, D_MODEL), F32),
        compiler_params=_cparams(("arbitrary",)),
        name="mixer_out",
    )(ys, ym, x2, mod, w_glu, b_glu.reshape(1, -1), w_out)


def _route_kernel(x_ref, g_ref, shift_ref, scale_ref, rw_ref, rb_ref, tri_ref,
                  h_ref, route_ref, gates_ref, cnt_ref, carry_ref):
    i = pl.program_id(0)

    @pl.when(i == 0)
    def _():
        carry_ref[...] = jnp.zeros(carry_ref.shape, F32)

    h = _norm_mod(x_ref[...], g_ref[...], scale_ref[0], shift_ref[0])
    _pack_store(h_ref, h)
    logits = jnp.dot(h, rw_ref[...], precision=lax.Precision.HIGHEST,
                     preferred_element_type=F32) + rb_ref[...]
    lane = lax.broadcasted_iota(jnp.int32, logits.shape, 1)
    lanef = lane.astype(F32)
    lg = jnp.where(lane < N_EXPERTS, logits, -jnp.inf)
    vals, hots, idxs = [], [], []
    sel = jnp.zeros(logits.shape, F32)
    for _ in range(TOP_K):
        m = jnp.max(lg, axis=-1, keepdims=True)
        idx = jnp.min(jnp.where(lg == m, lanef, float(LANES)), axis=-1, keepdims=True)
        hot = lanef == idx
        vals.append(m)
        idxs.append(idx)
        hots.append(hot)
        sel = jnp.where(hot, 1.0, sel)
        lg = jnp.where(hot, -jnp.inf, lg)
    es = [jnp.exp(v - vals[0]) for v in vals]
    denom = es[0] + es[1] + es[2] + es[3]
    before = _dot(tri_ref[...], sel.astype(BF16)) + carry_ref[...]
    route = jnp.zeros(logits.shape, F32)
    gates = jnp.zeros(logits.shape, F32)
    for k in range(TOP_K):
        rank = jnp.sum(jnp.where(hots[k], before, 0.0), axis=-1, keepdims=True)
        route = jnp.where(lane == k, idxs[k], route)
        route = jnp.where(lane == TOP_K + k, rank, route)
        gates = jnp.where(lane == k, es[k] / denom, gates)
    route_ref[...] = route.astype(jnp.int32)
    gates_ref[...] = gates
    carry_ref[...] = carry_ref[...] + jnp.sum(sel, axis=0, keepdims=True)
    cnt_ref[...] = carry_ref[...]


def _route(x2, g, mod, router_w, router_b, seq):
    t = x2.shape[0]
    tm = min(ROUTE_TM, seq)
    tpb = seq // tm
    rw = jnp.zeros((D_MODEL, LANES), F32).at[:, :N_EXPERTS].set(router_w)
    rb = jnp.zeros((1, LANES), F32).at[0, :N_EXPERTS].set(router_b)
    tri = jnp.asarray(np.tril(np.ones((tm, tm), np.float32), -1), BF16)
    row = lambda w: pl.BlockSpec((tm, w), lambda i: (i, 0))
    full = lambda a, b: pl.BlockSpec((a, b), lambda i: (0, 0))
    return pl.pallas_call(
        _route_kernel,
        grid=(t // tm,),
        in_specs=[row(D_MODEL), full(1, D_MODEL), _mod_specs(tpb, 0), _mod_specs(tpb, 1),
                  full(D_MODEL, LANES), full(1, LANES), full(tm, tm)],
        out_specs=[pl.BlockSpec((tm * PACK, LANES), lambda i: (i, 0)), row(LANES), row(LANES), full(1, LANES)],
        out_shape=[jax.ShapeDtypeStruct((t * PACK, LANES), U32),
                   jax.ShapeDtypeStruct((t, LANES), jnp.int32),
                   jax.ShapeDtypeStruct((t, LANES), F32),
                   jax.ShapeDtypeStruct((1, LANES), F32)],
        scratch_shapes=[pltpu.VMEM((1, LANES), F32)],
        compiler_params=_cparams(("arbitrary",)),
        name="moe_route",
    )(x2, g.reshape(1, -1), mod, mod, rw, rb, tri)


def _dispatch_kernel(pstart_ref, npad_ref, used_ref, dest_ref, h_hbm, xs_hbm, zero_ref, sem, zsem):
    i = pl.program_id(0)
    n = DISPATCH_T
    nslot = xs_hbm.shape[0] // PACK

    def pad_copy(slot):
        return pltpu.make_async_copy(zero_ref.at[_prow(0), :], xs_hbm.at[_prow(slot), :], zsem)

    def tail_copy(t):
        return pltpu.make_async_copy(zero_ref, xs_hbm.at[_prow(used_ref[0] + t * FFN_TM, FFN_TM), :], zsem)

    @pl.when(i == 0)
    def _():
        zero_ref[...] = jnp.zeros(zero_ref.shape, U32)
        ntail = (nslot - used_ref[0]) // FFN_TM
        for wait in (False, True):
            for e in range(N_EXPERTS):
                def pad(j, c):
                    cp = pad_copy(pstart_ref[e] + j)
                    cp.wait() if wait else cp.start()
                    return c
                lax.fori_loop(0, npad_ref[e], pad, 0)

            def tail(t, c):
                cp = tail_copy(t)
                cp.wait() if wait else cp.start()
                return c
            lax.fori_loop(0, ntail, tail, 0)

    def row_copy(tok, slot):
        return pltpu.make_async_copy(h_hbm.at[_prow(tok), :], xs_hbm.at[_prow(slot), :], sem)

    for k in range(TOP_K):
        def body(j, c):
            for par in range(2):
                r = 2 * j + par
                row_copy(i * n + r, dest_ref[0, 0, k * n + r]).start(priority=par)
            return c
        lax.fori_loop(0, n // 2, body, 0, unroll=4)

    def wait_rows(r, c):
        row_copy(0, 0).wait()
        return c

    @pl.when(i > 0)
    def _():
        lax.fori_loop(0, n * TOP_K, wait_rows, 0, unroll=8)

    @pl.when(i == pl.num_programs(0) - 1)
    def _():
        lax.fori_loop(0, n * TOP_K, wait_rows, 0, unroll=8)


def _unpack_kernel(x_ref, o_ref):
    n = o_ref.shape[0]
    half = D_MODEL // 2
    for s in range(PACK):
        lo, hi = _unpack_load(x_ref, n, s)
        o_ref[:, s * LANES:(s + 1) * LANES] = lo.astype(BF16)
        o_ref[:, half + s * LANES:half + (s + 1) * LANES] = hi.astype(BF16)


def _dispatch_rows(h_slabs, dest, pad_start, npad, used, ns):
    t = dest.shape[0]
    tm = min(DISPATCH_T, t)
    nt = t // tm
    idx = dest.reshape(nt, tm, TOP_K).transpose(0, 2, 1).reshape(nt, 1, TOP_K * tm)
    grid_spec = pltpu.PrefetchScalarGridSpec(
        num_scalar_prefetch=3,
        grid=(nt,),
        in_specs=[pl.BlockSpec((1, 1, TOP_K * tm), lambda i, *_: (i, 0, 0), memory_space=pltpu.SMEM),
                  pl.BlockSpec(memory_space=pl.ANY)],
        out_specs=pl.BlockSpec(memory_space=pl.ANY),
        scratch_shapes=[pltpu.VMEM((FFN_TM * PACK, LANES), U32),
                        pltpu.SemaphoreType.DMA(()), pltpu.SemaphoreType.DMA(())],
    )
    xs_slabs = pl.pallas_call(
        _dispatch_kernel,
        grid_spec=grid_spec,
        out_shape=jax.ShapeDtypeStruct((ns * PACK, LANES), U32),
        compiler_params=_cparams(("arbitrary",)),
        name="moe_dispatch",
    )(pad_start, npad, used, idx, h_slabs)
    return pl.pallas_call(
        _unpack_kernel,
        grid=(ns // GATHER_T,),
        in_specs=[pl.BlockSpec((GATHER_T * PACK, LANES), lambda i: (i, 0))],
        out_specs=pl.BlockSpec((GATHER_T, D_MODEL), lambda i: (i, 0)),
        out_shape=jax.ShapeDtypeStruct((ns, D_MODEL), BF16),
        compiler_params=_cparams(("arbitrary",)),
        name="moe_unpack",
    )(xs_slabs)


PERM_W = 256
W1_BLK = 1024


def _w1_prep_kernel(w_ref, p_ref, o_ref):
    for b in range(W1_BLK // PERM_W):
        cols = slice(b * PERM_W, (b + 1) * PERM_W)
        o_ref[0, :, cols] = _dot(w_ref[0, :, cols].astype(BF16), p_ref[...]).astype(BF16)


def _w1_prep(w1):
    e, d, n = w1.shape
    pm = np.zeros((PERM_W, PERM_W), np.float32)
    jj = np.arange(PERM_W // 2)
    pm[2 * jj, jj] = 1.0
    pm[2 * jj + 1, PERM_W // 2 + jj] = 1.0
    return pl.pallas_call(
        _w1_prep_kernel,
        grid=(e, n // W1_BLK),
        in_specs=[pl.BlockSpec((1, d, W1_BLK), lambda i, j: (i, 0, j)),
                  pl.BlockSpec((PERM_W, PERM_W), lambda i, j: (0, 0))],
        out_specs=pl.BlockSpec((1, d, W1_BLK), lambda i, j: (i, 0, j)),
        out_shape=jax.ShapeDtypeStruct((e, d, n), BF16),
        compiler_params=_cparams(("arbitrary", "arbitrary")),
        name="moe_w1_prep",
    )(w1, jnp.asarray(pm, BF16))


def _ffn_kernel(we_ref, row0_ref, nt_ref, nv_ref, xs_hbm, w1_ref, b1_ref, w2_ref, b2_ref,
                ys_hbm, x_ref, acc_ref, w2b_ref, stage_ref, sem_in, sem_out):
    w = pl.program_id(0)
    f = pl.program_id(1)
    nf = pl.num_programs(1)
    nt = nt_ref[w]
    row0 = row0_ref[w]
    tm = FFN_TM
    hw = PERM_W // 2

    @pl.when(jnp.logical_and(f == 0, nt > 0))
    def _():
        def tile_copy(t):
            return pltpu.make_async_copy(xs_hbm.at[pl.ds(pl.multiple_of(row0 + t * tm, tm), tm), :],
                                         x_ref.at[pl.ds(pl.multiple_of(t * tm, tm), tm), :], sem_in)

        def start(t, c):
            tile_copy(t).start()
            return c

        def wait(t, c):
            tile_copy(t).wait()
            return c

        lax.fori_loop(0, nt, start, 0)
        lax.fori_loop(0, nt, wait, 0)

    def out_copy(t, slot):
        return pltpu.make_async_copy(stage_ref.at[slot], ys_hbm.at[_prow(row0 + t * tm, tm), :], sem_out.at[slot])

    @pl.when(nt > 0)
    def _():
        w2b_ref[...] = w2_ref[0].astype(BF16)

        @pl.when(f == 0)
        def _():
            def init(t, c):
                acc_ref[pl.ds(pl.multiple_of(t * tm, tm), tm), :] = jnp.broadcast_to(b2_ref[0], (tm, D_MODEL))
                return c
            lax.fori_loop(0, nt, init, 0)

        def tile(xv, av):
            a = _dot(xv[...], w1_ref[0]) + b1_ref[0]
            nblk = a.shape[1] // PERM_W
            glu = jnp.concatenate([a[:, b * PERM_W:b * PERM_W + hw] for b in range(nblk)], axis=1)
            lin = jnp.concatenate([a[:, b * PERM_W + hw:(b + 1) * PERM_W] for b in range(nblk)], axis=1)
            glu = jnp.minimum(glu, SWIGLU_LIMIT)
            lin = jnp.clip(lin, -SWIGLU_LIMIT, SWIGLU_LIMIT)
            act = glu * jax.nn.sigmoid(SWIGLU_ALPHA * glu) * (lin + 1.0)
            av[...] = av[...] + _dot(act.astype(BF16), w2b_ref[...])

        def pair(p, c):
            win = pl.ds(pl.multiple_of(p * 2 * tm, 2 * tm), 2 * tm)
            xw, aw = x_ref.at[win, :], acc_ref.at[win, :]
            for h in range(2):
                tile(xw.at[h * tm:(h + 1) * tm, :], aw.at[h * tm:(h + 1) * tm, :])
            return c

        lax.fori_loop(0, nt // 2, pair, 0)

        @pl.when(nt % 2 == 1)
        def _():
            last = pl.ds(pl.multiple_of((nt - 1) * tm, tm), tm)
            tile(x_ref.at[last, :], acc_ref.at[last, :])

        @pl.when(f == nf - 1)
        def _():
            def emit(t, c):
                slot = t % 2

                @pl.when(t >= 2)
                def _():
                    out_copy(t - 2, slot).wait()

                _pack_store(stage_ref, acc_ref[pl.ds(pl.multiple_of(t * tm, tm), tm), :], lead=(slot,))
                out_copy(t, slot).start()
                return c

            lax.fori_loop(0, nt, emit, 0)
            for back in (2, 1):
                @pl.when(nt >= back)
                def _():
                    out_copy(nt - back, (nt - back) % 2).wait()

    @pl.when(jnp.logical_and(w == pl.num_programs(0) - 1, f == nf - 1))
    def _():
        used = nv_ref[1]
        ntail = (ys_hbm.shape[0] // PACK - used) // tm
        stage_ref[0] = jnp.zeros(stage_ref.shape[1:], U32)

        def fill(t, c):
            cp = pltpu.make_async_copy(stage_ref.at[0], ys_hbm.at[_prow(used + t * tm, tm), :], sem_out.at[0])
            cp.start()
            cp.wait()
            return c

        lax.fori_loop(0, ntail, fill, 0)


def _ffn(xs, w1p, b1p, w2b, b2, we, row0, ntile, nvalid):
    ns, d = xs.shape
    nf = EXPERT_FF // FFN_FC
    wmax = we.shape[0]

    def wmap(axis):
        def index(w, f, we_r, row0_r, nt_r, nv_r):
            fe = jnp.where(w < nv_r[0], f, nf - 1)
            return (we_r[w], 0, fe) if axis == 2 else (we_r[w], fe, 0)
        return index

    grid_spec = pltpu.PrefetchScalarGridSpec(
        num_scalar_prefetch=4,
        grid=(wmax, nf),
        in_specs=[pl.BlockSpec(memory_space=pl.ANY),
                  pl.BlockSpec((1, d, 2 * FFN_FC), wmap(2)),
                  pl.BlockSpec((1, 1, 2 * FFN_FC), wmap(2)),
                  pl.BlockSpec((1, FFN_FC, d), wmap(1)),
                  pl.BlockSpec((1, 1, d), lambda w, f, we_r, *_: (we_r[w], 0, 0))],
        out_specs=pl.BlockSpec(memory_space=pl.ANY),
        scratch_shapes=[pltpu.VMEM((FFN_R, d), BF16),
                        pltpu.VMEM((FFN_R, d), F32),
                        pltpu.VMEM((FFN_FC, d), BF16),
                        pltpu.VMEM((2, FFN_TM * PACK, LANES), U32),
                        pltpu.SemaphoreType.DMA(()),
                        pltpu.SemaphoreType.DMA((2,))],
    )
    return pl.pallas_call(
        _ffn_kernel,
        grid_spec=grid_spec,
        out_shape=jax.ShapeDtypeStruct((ns * PACK, LANES), U32),
        compiler_params=_cparams(("arbitrary", "arbitrary")),
        name="moe_ffn",
    )(we, row0, ntile, nvalid, xs, w1p, b1p, w2b, b2)


def _combine_kernel(cur_ref, nxt_ref, ys_hbm, gates_ref, x_ref, gmod_ref, fg_ref, o_ref, buf_ref, sem, *, final):
    i = pl.program_id(0)
    n = x_ref.shape[0]
    slot = i % 2

    def row_copy(idx_ref, b, k, r):
        src = ys_hbm.at[_prow(idx_ref[0, 0, k * n + r]), :]
        return pltpu.make_async_copy(src, buf_ref.at[b, k, _prow(r), :], sem.at[b])

    def issue(idx_ref, b):
        for k in range(TOP_K):
            def body(j, c):
                for par in range(2):
                    row_copy(idx_ref, b, k, 2 * j + par).start(priority=par)
                return c
            lax.fori_loop(0, n // 2, body, 0, unroll=4)

    @pl.when(i == 0)
    def _():
        issue(cur_ref, 0)

    @pl.when(i + 1 < pl.num_programs(0))
    def _():
        issue(nxt_ref, 1 - slot)

    for k in range(TOP_K):
        def wait(r, c):
            row_copy(cur_ref, slot, k, r).wait()
            return c
        lax.fori_loop(0, n, wait, 0, unroll=8)

    gates = gates_ref[...]
    gk = [gates[:, k:k + 1] for k in range(TOP_K)]
    half = D_MODEL // 2
    for s in range(PACK):
        lo = hi = None
        for k in range(TOP_K):
            lo_k, hi_k = _unpack_load(buf_ref, n, s, lead=(slot, k))
            lo = gk[k] * lo_k if lo is None else lo + gk[k] * lo_k
            hi = gk[k] * hi_k if hi is None else hi + gk[k] * hi_k
        for base, acc in ((s * LANES, lo), (half + s * LANES, hi)):
            cols = slice(base, base + LANES)
            o_ref[:, cols] = x_ref[:, cols] + gmod_ref[0, :, cols] * acc
    if final:
        o_ref[...] = _rms(o_ref[...], fg_ref[...])


def _combine(ys, dest, gates, x2, mod, final_g, seq, final):
    t, d = x2.shape
    tm = min(COMBINE_T, seq)
    tpb = seq // tm
    nt = t // tm
    idx = dest.reshape(nt, tm, TOP_K).transpose(0, 2, 1).reshape(nt, 1, TOP_K * tm)
    row = lambda w: pl.BlockSpec((tm, w), lambda i: (i, 0))
    return pl.pallas_call(
        functools.partial(_combine_kernel, final=final),
        grid=(nt,),
        in_specs=[pl.BlockSpec((1, 1, tm * TOP_K), lambda i: (i, 0, 0), memory_space=pltpu.SMEM),
                  pl.BlockSpec((1, 1, tm * TOP_K), lambda i: (jnp.minimum(i + 1, nt - 1), 0, 0),
                               memory_space=pltpu.SMEM),
                  pl.BlockSpec(memory_space=pl.ANY),
                  row(LANES), row(d), _mod_specs(tpb, 2),
                  pl.BlockSpec((1, d), lambda i: (0, 0))],
        out_specs=row(d),
        out_shape=jax.ShapeDtypeStruct((t, d), F32),
        scratch_shapes=[pltpu.VMEM((2, TOP_K, tm * PACK, LANES), U32), pltpu.SemaphoreType.DMA((2,))],
        compiler_params=_cparams(("arbitrary",)),
        name="moe_combine",
    )(idx, idx, ys, gates, x2, mod, final_g.reshape(1, -1))


def _moe_sublayer(x2, c_mod, norm_g, router_w, router_b, w1, b1, w2, b2, final_g, seq, final):
    t = x2.shape[0]
    h, route, gates, cnt = _route(x2, norm_g, c_mod, router_w, router_b, seq)
    top_e = route[:, :TOP_K]
    rank = route[:, TOP_K:2 * TOP_K]
    counts = cnt[0, :N_EXPERTS].astype(jnp.int32)
    padded = ((counts + FFN_TM - 1) // FFN_TM) * FFN_TM
    starts = jnp.cumsum(padded) - padded
    dest = starts[top_e] + rank
    ns = t * TOP_K + N_EXPERTS * FFN_TM
    ns = ((ns + GATHER_T - 1) // GATHER_T) * GATHER_T
    used = jnp.sum(padded).astype(jnp.int32)
    n_items = (padded + FFN_R - 1) // FFN_R
    item_end = jnp.cumsum(n_items)
    wmax = (t * TOP_K) // FFN_R + N_EXPERTS
    wid = jnp.arange(wmax, dtype=jnp.int32)
    nvalid = item_end[-1].astype(jnp.int32)
    we = jnp.minimum(jnp.searchsorted(item_end, wid, side='right'), N_EXPERTS - 1).astype(jnp.int32)
    last_e = we[jnp.maximum(nvalid - 1, 0)]
    valid = wid < nvalid
    we = jnp.where(valid, we, last_e)
    local = wid - (item_end - n_items)[we]
    row0 = jnp.where(valid, starts[we] + local * FFN_R, 0).astype(jnp.int32)
    ntile = jnp.where(valid, jnp.minimum(FFN_R, padded[we] - local * FFN_R) // FFN_TM, 0).astype(jnp.int32)

    xs = _dispatch_rows(h, dest.astype(jnp.int32), (starts + counts).astype(jnp.int32),
                        (padded - counts).astype(jnp.int32), used.reshape(1), ns)
    hw = PERM_W // 2
    b1p = b1.reshape(N_EXPERTS, 2 * EXPERT_FF // PERM_W, hw, 2).transpose(0, 1, 3, 2).reshape(N_EXPERTS, 1, 2 * EXPERT_FF)
    ys = _ffn(xs, _w1_prep(w1), b1p, w2, b2.reshape(N_EXPERTS, 1, D_MODEL),
              we, row0, ntile, jnp.stack([nvalid, used]))
    return _combine(ys, dest.astype(jnp.int32), gates, x2, c_mod, final_g, seq, final)


def _conv_pw1_kernel(x_ref, g_ref, shift_ref, scale_ref, wa_ref, wb_ref, ba_ref, bb_ref, o_ref, h_ref):
    j = pl.program_id(1)

    @pl.when(j == 0)
    def _():
        h_ref[...] = _norm_mod(x_ref[...], g_ref[...], scale_ref[0], shift_ref[0]).astype(BF16)

    h = h_ref[...]
    a = _dot(h, wa_ref[...]) + ba_ref[...]
    b = _dot(h, wb_ref[...]) + bb_ref[...]
    o_ref[...] = a * jax.nn.sigmoid(b)


def _conv_pw1(x2, g, mod, w_pw1, b_pw1, seq):
    t = x2.shape[0]
    tm = min(PROJ_TM, seq)
    tpb = seq // tm
    tn = 512
    nj = D_MODEL // tn
    mod2 = lambda which: pl.BlockSpec((1, 1, D_MODEL), lambda i, j: (i // tpb, 0, which))
    b2 = b_pw1.reshape(1, -1)
    return pl.pallas_call(
        _conv_pw1_kernel,
        grid=(t // tm, nj),
        in_specs=[pl.BlockSpec((tm, D_MODEL), lambda i, j: (i, 0)),
                  pl.BlockSpec((1, D_MODEL), lambda i, j: (0, 0)),
                  mod2(0), mod2(1),
                  pl.BlockSpec((D_MODEL, tn), lambda i, j: (0, j)),
                  pl.BlockSpec((D_MODEL, tn), lambda i, j: (0, j + nj)),
                  pl.BlockSpec((1, tn), lambda i, j: (0, j)),
                  pl.BlockSpec((1, tn), lambda i, j: (0, j + nj))],
        out_specs=pl.BlockSpec((tm, tn), lambda i, j: (i, j)),
        out_shape=jax.ShapeDtypeStruct((t, D_MODEL), F32),
        scratch_shapes=[pltpu.VMEM((tm, D_MODEL), BF16)],
        compiler_params=_cparams(("arbitrary", "arbitrary")),
        name="conv_pw1",
    )(x2, g.reshape(1, -1), mod, mod, w_pw1, w_pw1, b2, b2)


def _conv_dw_kernel(y_ref, halo_ref, wdw_ref, bdw_ref, lng_ref, lnb_ref, wp_ref, bp_ref, x_ref, gate_ref,
                    o_ref, buf_ref, z_ref, win_ref, *, tpb):
    i = pl.program_id(0)
    tm = y_ref.shape[0]
    first = (i % tpb) == 0
    halo = halo_ref[...]
    buf_ref[0:CONV_HALO, :] = jnp.where(first, 0.0, halo)
    buf_ref[CONV_HALO:, :] = y_ref[...]
    cw = CONV_CW
    rb = min(CONV_RB, tm)
    off = CONV_HALO - (CONV_KERNEL - 1)
    for c0 in range(0, D_MODEL, cw):
        for r0 in range(0, tm, rb):
            acc = jnp.zeros((rb, cw), F32)
            for j in range(SUBLANES):
                taps = [k for k in range(CONV_KERNEL) if (off + k) % SUBLANES == j]
                span = max(off + k for k in taps) - j + rb
                win_ref[0:span, :] = buf_ref[r0 + j:r0 + j + span, c0:c0 + cw]
                for k in taps:
                    q = off + k - j
                    acc = acc + wdw_ref[k:k + 1, c0:c0 + cw] * win_ref[q:q + rb, :]
            z_ref[r0:r0 + rb, c0:c0 + cw] = acc
    z = z_ref[...] + bdw_ref[...]
    mu = jnp.mean(z, axis=-1, keepdims=True)
    zc = z - mu
    var = jnp.mean(zc * zc, axis=-1, keepdims=True)
    zn = zc * lax.rsqrt(var + LN_EPS) * lng_ref[...] + lnb_ref[...]
    act = (zn * jax.nn.sigmoid(zn)).astype(BF16)
    o_ref[...] = x_ref[...] + gate_ref[0] * (_dot(act, wp_ref[...]) + bp_ref[...])


def _conv_dw(y1, x2, mod, w_dw, b_dw, ln_g, ln_b, w_pw2, b_pw2, seq):
    t = x2.shape[0]
    tm = min(CONV_TM, seq)
    tpb = seq // tm
    hb = tm // CONV_HALO
    wdw = jnp.zeros((CONV_HALO, D_MODEL), F32).at[:CONV_KERNEL].set(w_dw)
    row = lambda: pl.BlockSpec((tm, D_MODEL), lambda i: (i, 0))
    vec = lambda: pl.BlockSpec((1, D_MODEL), lambda i: (0, 0))
    return pl.pallas_call(
        functools.partial(_conv_dw_kernel, tpb=tpb),
        grid=(t // tm,),
        in_specs=[row(),
                  pl.BlockSpec((CONV_HALO, D_MODEL), lambda i: (jnp.maximum(i * hb - 1, 0), 0)),
                  pl.BlockSpec((CONV_HALO, D_MODEL), lambda i: (0, 0)),
                  vec(), vec(), vec(),
                  pl.BlockSpec((D_MODEL, D_MODEL), lambda i: (0, 0)),
                  vec(), row(), _mod_specs(tpb, 2)],
        out_specs=row(),
        out_shape=jax.ShapeDtypeStruct((t, D_MODEL), F32),
        scratch_shapes=[pltpu.VMEM((tm + CONV_HALO, D_MODEL), F32), pltpu.VMEM((tm, D_MODEL), F32),
                        pltpu.VMEM((CONV_RB + CONV_HALO, CONV_CW), F32)],
        compiler_params=_cparams(("arbitrary",)),
        name="conv_dw_pw2",
    )(y1, y1, wdw, b_dw.reshape(1, -1), ln_g.reshape(1, -1), ln_b.reshape(1, -1),
      w_pw2, b_pw2.reshape(1, -1), x2, mod)


def _rope_swap_cols(w, heads):
    k = w.shape[0]
    half = MLA_ROPE // 2
    w3 = w.reshape(k, heads, MLA_NOPE + MLA_ROPE)
    pe = w3[:, :, MLA_NOPE:]
    sw = jnp.concatenate([pe[:, :, half:], pe[:, :, :half]], axis=-1)
    return jnp.concatenate([w3, sw], axis=-1).reshape(k, heads * (MLA_NOPE + 2 * MLA_ROPE))


def kernel(x, c, positions, l0_mix_norm_g, l0_mix_ada_w, l0_mix_ada_b, l0_w_in, l0_s5_lambda_re, l0_s5_lambda_im, l0_s5_log_step, l0_s5_b_re, l0_s5_b_im, l0_s5_c_re, l0_s5_c_im, l0_s5_d, l0_s5_w_glu, l0_s5_b_glu, l0_mla_q_norm_g, l0_mla_w_uq, l0_mla_kv_norm_g, l0_mla_w_ukv, l0_w_out, l0_moe_norm_g, l0_moe_ada_w, l0_moe_ada_b, l0_router_w, l0_router_b, l0_exp_w1, l0_exp_b1, l0_exp_w2, l0_exp_b2, l1_mix_norm_g, l1_mix_ada_w, l1_mix_ada_b, l1_conv_w_pw1, l1_conv_b_pw1, l1_conv_w_dw, l1_conv_b_dw, l1_conv_ln_g, l1_conv_ln_b, l1_conv_w_pw2, l1_conv_b_pw2, l1_moe_norm_g, l1_moe_ada_w, l1_moe_ada_b, l1_router_w, l1_router_b, l1_exp_w1, l1_exp_b1, l1_exp_w2, l1_exp_b2, final_norm_g):
    nbatch, seq, d = x.shape
    t = nbatch * seq
    x2 = x.reshape(t, d)
    pos = positions.reshape(t, 1).astype(jnp.int32)

    mod = _ada_mod(c, l0_mix_ada_w, l0_mix_ada_b)
    half = MLA_ROPE // 2
    kcol = S5_WIDTH + MLA_Q_RANK + MLA_KV_RANK
    w_in_ext = jnp.concatenate([l0_w_in, l0_w_in[:, kcol + half:], l0_w_in[:, kcol:kcol + half]], axis=1).astype(BF16)
    u, cq, ckv, kr = _proj_in(x2, l0_mix_norm_g, mod, w_in_ext, seq)

    kt, wt, vv, a16 = _s5_prep(l0_s5_lambda_re, l0_s5_lambda_im, l0_s5_log_step, l0_s5_b_re, l0_s5_b_im,
                               l0_s5_c_re, l0_s5_c_im, l0_s5_d)
    nchunk = seq // S5_CHUNK
    u_r = (u.reshape(nbatch, nchunk, S5_CHUNK, S5_GROUPS, S5_GROUP_CH)
           .transpose(3, 0, 1, 2, 4).reshape(S5_GROUPS, nbatch * nchunk, S5_ROW))
    y_r = _s5_main(u_r, kt, wt, vv, a16, nbatch, nchunk)
    ys = (y_r.reshape(S5_GROUPS, nbatch, nchunk, S5_CHUNK, S5_GROUP_CH)
          .transpose(1, 2, 3, 0, 4).reshape(t, S5_WIDTH))

    wq_ext = _rope_swap_cols(l0_mla_w_uq, MLA_HEADS).astype(BF16)
    q, k, v = _mla_proj(cq, ckv, kr, pos, l0_mla_q_norm_g, l0_mla_kv_norm_g, wq_ext,
                        l0_mla_w_ukv.astype(BF16), nbatch, seq)
    ym = _flash_attention(q, k, v).reshape(t, MLA_HEADS * MLA_V)
    x2 = _mixer_out(ys, ym, x2, mod, l0_s5_w_glu.astype(BF16), l0_s5_b_glu, l0_w_out.astype(BF16), seq)

    mod = _ada_mod(c, l0_moe_ada_w, l0_moe_ada_b)
    x2 = _moe_sublayer(x2, mod, l0_moe_norm_g, l0_router_w, l0_router_b, l0_exp_w1, l0_exp_b1,
                       l0_exp_w2, l0_exp_b2, final_norm_g, seq, final=False)

    mod = _ada_mod(c, l1_mix_ada_w, l1_mix_ada_b)
    y1 = _conv_pw1(x2, l1_mix_norm_g, mod, l1_conv_w_pw1.astype(BF16), l1_conv_b_pw1, seq)
    x2 = _conv_dw(y1, x2, mod, l1_conv_w_dw, l1_conv_b_dw, l1_conv_ln_g, l1_conv_ln_b,
                  l1_conv_w_pw2.astype(BF16), l1_conv_b_pw2, seq)

    mod = _ada_mod(c, l1_moe_ada_w, l1_moe_ada_b)
    x2 = _moe_sublayer(x2, mod, l1_moe_norm_g, l1_router_w, l1_router_b, l1_exp_w1, l1_exp_b1,
                       l1_exp_w2, l1_exp_b2, final_norm_g, seq, final=True)
    return x2.reshape(nbatch, seq, d)
```

```python
import functools
import math

import numpy as np
import jax
import jax.numpy as jnp
from jax import lax
from jax.experimental import pallas as pl
from jax.experimental.pallas import tpu as pltpu

F32 = jnp.float32
BF16 = jnp.bfloat16

D_MODEL = 2048
S5_WIDTH = 1024
S5_GROUP_CH = 16
S5_GROUPS = 64
S5_STATE = 64
MLA_HEADS = 8
MLA_NOPE = 128
MLA_ROPE = 64
MLA_V = 128
MLA_Q_RANK = 512
MLA_KV_RANK = 512
ROPE_THETA = 10000.0
CONV_KERNEL = 31
N_EXPERTS = 32
TOP_K = 4
EXPERT_FF = 2048
SWIGLU_ALPHA = 1.702
SWIGLU_LIMIT = 7.0
EPS = 1e-6
LN_EPS = 1e-5
NEG_BIG = -1e30

LANES = 128
SUBLANES = 8
VMEM_LIMIT_BYTES = 56 * 1024 * 1024

S5_CHUNK = 16
S5_ROW = S5_CHUNK * S5_GROUP_CH
PROJ_TM = 512
MLA_TM = 256
ATT_T = 512
OUT_TM = 256
ROUTE_TM = 512
CONV_TM = 256
CONV_HALO = 32
CONV_RB = 128
CONV_CW = 256
FFN_TM = 256
FFN_R = 2304
FFN_FC = 512
GATHER_T = 512
DISPATCH_T = 512
COMBINE_T = 256


def _cparams(sem):
    return pltpu.CompilerParams(dimension_semantics=sem, vmem_limit_bytes=VMEM_LIMIT_BYTES)


def _norm_mod(x, g, scale, shift):
    ms = jnp.mean(x * x, axis=-1, keepdims=True)
    return (x * lax.rsqrt(ms + EPS)) * g * (1.0 + scale) + shift


def _rms(x, g):
    ms = jnp.mean(x * x, axis=-1, keepdims=True)
    return (x * lax.rsqrt(ms + EPS)) * g


def _dot(a, b):
    return jnp.dot(a, b, preferred_element_type=F32)


def _dot_nt(a, b):
    return lax.dot_general(a, b, (((1,), (1,)), ((), ())), preferred_element_type=F32)


U32 = jnp.uint32
PACK = D_MODEL // (2 * LANES)
HI_MASK = 0xFFFF0000


def _pack_store(ref, val, lead=()):
    n = val.shape[0]
    half = D_MODEL // 2
    for s in range(PACK):
        lo = val[:, s * LANES:(s + 1) * LANES].astype(BF16).astype(F32)
        hi = val[:, half + s * LANES:half + (s + 1) * LANES].astype(BF16).astype(F32)
        word = (lax.bitcast_convert_type(lo, U32) >> 16) | (lax.bitcast_convert_type(hi, U32) & jnp.uint32(HI_MASK))
        ref[lead + (pl.ds(s, n, stride=PACK), slice(None))] = word


def _unpack_load(ref, n, s, lead=()):
    word = ref[lead + (pl.ds(s, n, stride=PACK), slice(None))]
    lo = lax.bitcast_convert_type(word << 16, F32)
    hi = lax.bitcast_convert_type(word & jnp.uint32(HI_MASK), F32)
    return lo, hi


def _prow(r, n=1):
    return pl.ds(pl.multiple_of(r * PACK, PACK), n * PACK)


ADA_TN = 768
ADA_KC = 256


def _ada_kernel(ct_ref, w_ref, b_ref, o_ref):
    nb = ct_ref.shape[0]
    kdim = w_ref.shape[0]
    rows = []
    for b in range(nb):
        acc = jnp.zeros((1, w_ref.shape[1]), F32)
        for k0 in range(0, kdim, ADA_KC):
            c = ct_ref[b, k0:k0 + ADA_KC, :]
            cs = c * jax.nn.sigmoid(c)
            acc = acc + jnp.sum(w_ref[k0:k0 + ADA_KC, :] * cs, axis=0, keepdims=True)
        rows.append(acc)
    o_ref[...] = jnp.concatenate(rows, axis=0) + b_ref[...]


def _ada_mod(c, w, b):
    nb, d = c.shape
    n = w.shape[1]
    ct = c.reshape(nb, d, 1)
    m = pl.pallas_call(
        _ada_kernel,
        grid=(n // ADA_TN,),
        in_specs=[pl.BlockSpec((nb, d, 1), lambda j: (0, 0, 0)),
                  pl.BlockSpec((d, ADA_TN), lambda j: (0, j)),
                  pl.BlockSpec((1, ADA_TN), lambda j: (0, j))],
        out_specs=pl.BlockSpec((nb, ADA_TN), lambda j: (0, j)),
        out_shape=jax.ShapeDtypeStruct((nb, n), F32),
        compiler_params=_cparams(("arbitrary",)),
        name="ada_mod",
    )(ct, w, b.reshape(1, n))
    return m.reshape(nb, 1, n)


def _mod_specs(tiles_per_batch, which):
    return pl.BlockSpec((1, 1, D_MODEL), lambda i, *_: (i // tiles_per_batch, 0, which))


def _proj_in_kernel(x_ref, g_ref, shift_ref, scale_ref, w_ref, u_ref, cq_ref, ckv_ref, kr_ref):
    h = _norm_mod(x_ref[...], g_ref[...], scale_ref[0], shift_ref[0]).astype(BF16)
    acc = _dot(h, w_ref[...])
    c0, c1, c2 = S5_WIDTH, S5_WIDTH + MLA_Q_RANK, S5_WIDTH + MLA_Q_RANK + MLA_KV_RANK
    u_ref[...] = acc[:, :c0].astype(BF16)
    cq_ref[...] = acc[:, c0:c1]
    ckv_ref[...] = acc[:, c1:c2]
    kr_ref[...] = acc[:, c2:]


def _proj_in(x2, g, mod, w_ext, seq):
    t = x2.shape[0]
    tm = min(PROJ_TM, seq)
    tpb = seq // tm
    n = w_ext.shape[1]
    row = lambda w: pl.BlockSpec((tm, w), lambda i: (i, 0))
    return pl.pallas_call(
        _proj_in_kernel,
        grid=(t // tm,),
        in_specs=[row(D_MODEL),
                  pl.BlockSpec((1, D_MODEL), lambda i: (0, 0)),
                  _mod_specs(tpb, 0), _mod_specs(tpb, 1),
                  pl.BlockSpec((D_MODEL, n), lambda i: (0, 0))],
        out_specs=[row(S5_WIDTH), row(MLA_Q_RANK), row(MLA_KV_RANK), row(2 * MLA_ROPE)],
        out_shape=[jax.ShapeDtypeStruct((t, S5_WIDTH), BF16),
                   jax.ShapeDtypeStruct((t, MLA_Q_RANK), F32),
                   jax.ShapeDtypeStruct((t, MLA_KV_RANK), F32),
                   jax.ShapeDtypeStruct((t, 2 * MLA_ROPE), F32)],
        compiler_params=_cparams(("arbitrary",)),
        name="proj_in",
    )(x2, g.reshape(1, -1), mod, mod, w_ext)


def _s5_prep_kernel(lamc_ref, lamr_ref, step_ref, bt_ref, btt_ref, ct_ref, d_ref,
                    kt_ref, wt_ref, v_ref, a16_ref):
    P, H, C = S5_STATE, S5_GROUP_CH, S5_CHUNK
    step = step_ref[0]
    step = jnp.exp(step)
    lr_c = jnp.minimum(lamc_ref[0, 0], -1e-4)
    li_c = lamc_ref[0, 1]
    lr_r = jnp.minimum(lamr_ref[0, 0:1, :], -1e-4)
    li_r = lamr_ref[0, 1:2, :]

    def ratio(lr, li):
        mag = jnp.exp(lr * step)
        ab_re = mag * jnp.cos(li * step)
        ab_im = mag * jnp.sin(li * step)
        denom = lr * lr + li * li
        nr, ni = ab_re - 1.0, ab_im
        return (nr * lr + ni * li) / denom, (ni * lr - nr * li) / denom

    rr_c, ri_c = ratio(lr_c, li_c)
    rr_r, ri_r = ratio(lr_r, li_r)

    lane = lax.broadcasted_iota(jnp.int32, (1, S5_ROW), 1)
    kk = (lane // H).astype(F32)

    def powers(k):
        mag = jnp.exp(lr_c * step * k)
        return mag * jnp.cos(li_c * step * k), mag * jnp.sin(li_c * step * k)

    bre_t, bim_t = bt_ref[0, 0], bt_ref[0, 1]
    bbt_re = rr_c * bre_t - ri_c * bim_t
    bbt_im = rr_c * bim_t + ri_c * bre_t
    cre_t, cim_t = ct_ref[0, 0], ct_ref[0, 1]

    er, ei = powers(float(C - 1) - kk)
    wt_ref[0, 0:P, :] = (er * bbt_re - ei * bbt_im).astype(BF16)
    wt_ref[0, P:2 * P, :] = (er * bbt_im + ei * bbt_re).astype(BF16)

    er, ei = powers(kk + 1.0)
    v_ref[0, 0:P, :] = (cre_t * er - cim_t * ei).astype(BF16)
    v_ref[0, P:2 * P, :] = (-cre_t * ei - cim_t * er).astype(BF16)

    er, ei = powers(kk)
    q_re = er * cre_t - ei * cim_t
    q_im = er * cim_t + ei * cre_t
    brt, bit = btt_ref[0, 0], btt_ref[0, 1]
    bbr = rr_r * brt - ri_r * bit
    bbi = rr_r * bit + ri_r * brt
    hi = lax.Precision.HIGHEST
    mall = (jnp.dot(bbr, q_re, precision=hi, preferred_element_type=F32)
            - jnp.dot(bbi, q_im, precision=hi, preferred_element_type=F32))
    rowh = lax.broadcasted_iota(jnp.int32, (H, S5_ROW), 0)
    laneh = lax.broadcasted_iota(jnp.int32, (H, S5_ROW), 1)
    mall = mall + jnp.where(laneh == rowh, d_ref[0], 0.0)
    for s in range(C):
        piece = mall if s == 0 else pltpu.roll(mall, H * s, 1)
        piece = jnp.where(laneh >= H * s, piece, 0.0)
        kt_ref[0, s * H:(s + 1) * H, :] = piece.astype(BF16)

    mag = jnp.exp(lr_r * step * float(C))
    a16_ref[0, 0:1, :] = mag * jnp.cos(li_r * step * float(C))
    a16_ref[0, 1:2, :] = mag * jnp.sin(li_r * step * float(C))


def _s5_prep(lam_re, lam_im, log_step, b_re, b_im, c_re, c_im, d):
    G, P, H, C = S5_GROUPS, S5_STATE, S5_GROUP_CH, S5_CHUNK
    lam = jnp.stack([lam_re, lam_im], axis=1).astype(F32)
    lamc = lam.reshape(G, 2, P, 1)
    b = jnp.stack([b_re, b_im], axis=1).astype(F32)
    bt = jnp.tile(b, (1, 1, 1, C))
    btt = jnp.swapaxes(b, 2, 3)
    c = jnp.stack([c_re, c_im], axis=1).astype(F32)
    ct = jnp.tile(jnp.swapaxes(c, 2, 3), (1, 1, 1, C))
    dt = jnp.tile(d.astype(F32), (1, C)).reshape(G, 1, S5_ROW)
    step = log_step.astype(F32).reshape(G, 1, 1)
    g4 = lambda *shape: pl.BlockSpec((1,) + shape, lambda g: (g,) + (0,) * len(shape))
    return pl.pallas_call(
        _s5_prep_kernel,
        grid=(G,),
        in_specs=[g4(2, P, 1), g4(2, P), g4(1, 1), g4(2, P, S5_ROW), g4(2, H, P),
                  g4(2, P, S5_ROW), g4(1, S5_ROW)],
        out_specs=[g4(S5_ROW, S5_ROW), g4(2 * P, S5_ROW), g4(2 * P, S5_ROW), g4(2, P)],
        out_shape=[jax.ShapeDtypeStruct((G, S5_ROW, S5_ROW), BF16),
                   jax.ShapeDtypeStruct((G, 2 * P, S5_ROW), BF16),
                   jax.ShapeDtypeStruct((G, 2 * P, S5_ROW), BF16),
                   jax.ShapeDtypeStruct((G, 2, P), F32)],
        compiler_params=_cparams(("arbitrary",)),
        name="s5_prep",
    )(lamc, lam, step, bt, btt, ct, dt)


def _s5_main_kernel(u_ref, kt_ref, wt_ref, v_ref, a16_ref, y_ref, sr_ref, si_ref, xr_ref, xi_ref,
                    *, nbatch, nchunk):
    P = S5_STATE
    u = u_ref[0]
    sr_ref[...] = _dot_nt(u, wt_ref[0, 0:P, :])
    si_ref[...] = _dot_nt(u, wt_ref[0, P:2 * P, :])
    ar = a16_ref[0, 0:1, :]
    ai = a16_ref[0, 1:2, :]

    def step(c, carry):
        out = []
        for b in range(nbatch):
            xr, xi = carry[2 * b], carry[2 * b + 1]
            r = b * nchunk + c
            xr_ref[pl.ds(r, 1), :] = xr
            xi_ref[pl.ds(r, 1), :] = xi
            nxr = ar * xr - ai * xi + sr_ref[pl.ds(r, 1), :]
            nxi = ar * xi + ai * xr + si_ref[pl.ds(r, 1), :]
            out += [nxr, nxi]
        return tuple(out)

    zero = jnp.zeros((1, P), F32)
    lax.fori_loop(0, nchunk, step, (zero,) * (2 * nbatch))
    y = _dot(u, kt_ref[0])
    y = y + _dot(xr_ref[...].astype(BF16), v_ref[0, 0:P, :])
    y = y + _dot(xi_ref[...].astype(BF16), v_ref[0, P:2 * P, :])
    y_ref[0] = y.astype(y_ref.dtype)


def _s5_main(u_r, kt, wt, v, a16, nbatch, nchunk):
    G, rows, _ = u_r.shape
    P = S5_STATE
    g3 = lambda a, b: pl.BlockSpec((1, a, b), lambda g: (g, 0, 0))
    return pl.pallas_call(
        functools.partial(_s5_main_kernel, nbatch=nbatch, nchunk=nchunk),
        grid=(G,),
        in_specs=[g3(rows, S5_ROW), g3(S5_ROW, S5_ROW), g3(2 * P, S5_ROW), g3(2 * P, S5_ROW), g3(2, P)],
        out_specs=g3(rows, S5_ROW),
        out_shape=jax.ShapeDtypeStruct((G, rows, S5_ROW), BF16),
        scratch_shapes=[pltpu.VMEM((rows, P), F32) for _ in range(4)],
        compiler_params=_cparams(("arbitrary",)),
        name="s5_main",
    )(u_r, kt, wt, v, a16)


def _mla_proj_kernel(cq_ref, ckv_ref, kr_ref, pos_ref, qg_ref, kvg_ref, wq_ref, wkv_ref,
                     invf_ref, sgn_ref, q_ref, k_ref, v_ref):
    qscale = 1.0 / math.sqrt(MLA_NOPE + MLA_ROPE)
    qa = _dot(_rms(cq_ref[...], qg_ref[...]).astype(BF16), wq_ref[...])
    kva = _dot(_rms(ckv_ref[...], kvg_ref[...]).astype(BF16), wkv_ref[...])
    ang = pos_ref[...].astype(F32) * invf_ref[...]
    cc = jnp.cos(ang)
    ss = jnp.sin(ang) * sgn_ref[...]

    def rope(slab):
        return slab * cc + pltpu.roll(slab, MLA_ROPE, 1) * ss

    kpe = rope(kr_ref[...])[:, :MLA_ROPE].astype(BF16)
    hw = MLA_NOPE + 2 * MLA_ROPE
    lane = lax.broadcasted_iota(jnp.int32, (cq_ref.shape[0], MLA_V), 1)
    ones_col = jnp.where(lane == 0, 1.0, 0.0).astype(BF16)
    for h in range(MLA_HEADS):
        blk = qa[:, h * hw:(h + 1) * hw]
        q_ref[0, h, :, 0:MLA_NOPE] = (blk[:, :MLA_NOPE] * qscale).astype(BF16)
        qpe = rope(blk[:, MLA_NOPE:]) * qscale
        q_ref[0, h, :, MLA_NOPE:MLA_NOPE + MLA_ROPE] = qpe[:, :MLA_ROPE].astype(BF16)
        kvb = kva[:, h * hw:(h + 1) * hw]
        k_ref[0, h, :, 0:MLA_NOPE] = kvb[:, :MLA_NOPE].astype(BF16)
        k_ref[0, h, :, MLA_NOPE:MLA_NOPE + MLA_ROPE] = kpe
        v_ref[0, h, :, 0:MLA_V] = kvb[:, MLA_NOPE:].astype(BF16)
        v_ref[0, h, :, MLA_V:2 * MLA_V] = ones_col


def _mla_proj(cq, ckv, kr, pos, qg, kvg, wq_ext, wkv, nbatch, seq):
    tm = min(MLA_TM, seq)
    nl = seq // tm
    dqk = MLA_NOPE + MLA_ROPE
    half = MLA_ROPE // 2
    inv_freq = 1.0 / (ROPE_THETA ** (jnp.arange(0, MLA_ROPE, 2, dtype=F32) / MLA_ROPE))
    invf = jnp.tile(inv_freq, 4).reshape(1, 2 * MLA_ROPE)
    sgn = jnp.tile(jnp.concatenate([-jnp.ones((half,), F32), jnp.ones((half,), F32)]), 2).reshape(1, 2 * MLA_ROPE)
    row = lambda w: pl.BlockSpec((tm, w), lambda b, i: (b * nl + i, 0))
    full = lambda a, b_: pl.BlockSpec((a, b_), lambda b, i: (0, 0))
    head = lambda w: pl.BlockSpec((1, MLA_HEADS, tm, w), lambda b, i: (b, 0, i, 0))
    return pl.pallas_call(
        _mla_proj_kernel,
        grid=(nbatch, nl),
        in_specs=[row(MLA_Q_RANK), row(MLA_KV_RANK), row(2 * MLA_ROPE), row(1),
                  full(1, MLA_Q_RANK), full(1, MLA_KV_RANK),
                  full(MLA_Q_RANK, wq_ext.shape[1]), full(MLA_KV_RANK, wkv.shape[1]),
                  full(1, 2 * MLA_ROPE), full(1, 2 * MLA_ROPE)],
        out_specs=[head(dqk), head(dqk), head(2 * MLA_V)],
        out_shape=[jax.ShapeDtypeStruct((nbatch, MLA_HEADS, seq, dqk), BF16),
                   jax.ShapeDtypeStruct((nbatch, MLA_HEADS, seq, dqk), BF16),
                   jax.ShapeDtypeStruct((nbatch, MLA_HEADS, seq, 2 * MLA_V), BF16)],
        compiler_params=_cparams(("arbitrary", "arbitrary")),
        name="mla_proj",
    )(cq, ckv, kr, pos, qg.reshape(1, -1), kvg.reshape(1, -1), wq_ext, wkv, invf, sgn)


def _lane_groups(x):
    return [x[:, j * LANES:(j + 1) * LANES] for j in range(x.shape[1] // LANES)]


def _flash_kernel(q_ref, k_ref, v_ref, o_ref, s_ref, *, t):
    qi = pl.program_id(2)
    qh = [q_ref[0, 0, 0:t, :], q_ref[0, 0, t:2 * t, :]]

    def kblk(c):
        return k_ref[0, 0, pl.ds(pl.multiple_of(c * t, t), t), :]

    def vblk(c):
        return v_ref[0, 0, pl.ds(pl.multiple_of(c * t, t), t), :]

    def fold_max(s, mx):
        return functools.reduce(jnp.maximum, _lane_groups(s), mx)

    def pass1(c, mx):
        kc = kblk(c)
        out = []
        for h in range(2):
            s = _dot_nt(qh[h], kc)
            s_ref[h, c] = s
            out.append(fold_max(s, mx[h]))
        return tuple(out)

    neg = jnp.full((t, LANES), NEG_BIG, F32)
    mx0, mx1 = lax.fori_loop(0, 2 * qi, pass1, (neg, neg))
    c0, c1 = 2 * qi, 2 * qi + 1
    row = lax.broadcasted_iota(jnp.int32, (t, t), 0)
    col = lax.broadcasted_iota(jnp.int32, (t, t), 1)
    tri = col <= row
    k0, k1 = kblk(c0), kblk(c1)
    s00 = jnp.where(tri, _dot_nt(qh[0], k0), NEG_BIG)
    s10 = _dot_nt(qh[1], k0)
    s11 = jnp.where(tri, _dot_nt(qh[1], k1), NEG_BIG)
    s_ref[0, c0] = s00
    s_ref[1, c0] = s10
    s_ref[1, c1] = s11
    mx0 = fold_max(s00, mx0)
    mx1 = fold_max(s11, fold_max(s10, mx1))
    m = [jnp.broadcast_to(jnp.max(mx, axis=-1, keepdims=True), (t, LANES)) for mx in (mx0, mx1)]

    mt = [jnp.concatenate([mh] * (t // LANES), axis=1) for mh in m]

    def probs(h, c):
        return jnp.exp(s_ref[h, c] - mt[h]).astype(BF16)

    def pass2(j, carry):
        c = 2 * j
        vc = v_ref[0, 0, pl.ds(pl.multiple_of(c * t, 2 * t), 2 * t), :]
        out = []
        for h in range(2):
            p = jnp.concatenate([probs(h, c), probs(h, c + 1)], axis=1)
            out.append(carry[h] + _dot(p, vc))
        return tuple(out)

    za = jnp.zeros((t, 2 * MLA_V), F32)
    a0, a1 = lax.fori_loop(0, qi, pass2, (za, za))
    v0, v1 = vblk(c0), vblk(c1)
    a0 = a0 + _dot(probs(0, c0), v0)
    a1 = a1 + _dot(jnp.concatenate([probs(1, c0), probs(1, c1)], axis=1), jnp.concatenate([v0, v1], axis=0))
    o_ref[0, 0:t, :] = (a0[:, :MLA_V] / a0[:, MLA_V:MLA_V + 1]).astype(o_ref.dtype)
    o_ref[0, t:2 * t, :] = (a1[:, :MLA_V] / a1[:, MLA_V:MLA_V + 1]).astype(o_ref.dtype)


def _flash_attention(q, k, v):
    nbatch, nh, seq, dqk = q.shape
    t = min(ATT_T, seq // 2)
    nq = seq // (2 * t)
    return pl.pallas_call(
        functools.partial(_flash_kernel, t=t),
        grid=(nbatch, nh, nq),
        in_specs=[pl.BlockSpec((1, 1, 2 * t, dqk), lambda b, h, i: (b, h, i, 0)),
                  pl.BlockSpec((1, 1, seq, dqk), lambda b, h, i: (b, h, 0, 0)),
                  pl.BlockSpec((1, 1, seq, 2 * MLA_V), lambda b, h, i: (b, h, 0, 0))],
        out_specs=pl.BlockSpec((1, 2 * t, MLA_V), lambda b, h, i: (b, i, h)),
        out_shape=jax.ShapeDtypeStruct((nbatch, seq, nh * MLA_V), BF16),
        scratch_shapes=[pltpu.VMEM((2, seq // t, t, t), F32)],
        compiler_params=_cparams(("arbitrary", "arbitrary", "arbitrary")),
        name="flash_attn",
    )(q, k, v)


def _gelu_tanh(x):
    c = math.sqrt(2.0 / math.pi)
    return 0.5 * x * (1.0 + jnp.tanh(c * (x + 0.044715 * (x * x * x))))


def _mixer_out_kernel(ys_ref, ym_ref, x_ref, gate_ref, wglu_ref, bglu_ref, wo_ref, o_ref):
    y = _gelu_tanh(ys_ref[...].astype(F32))
    g = _dot(y.astype(BF16), wglu_ref[...]) + bglu_ref[...]
    s5o = (y * jax.nn.sigmoid(g)).astype(BF16)
    acc = _dot(s5o, wo_ref[0:S5_WIDTH, :]) + _dot(ym_ref[...], wo_ref[S5_WIDTH:, :])
    o_ref[...] = x_ref[...] + gate_ref[0] * acc


def _mixer_out(ys, ym, x2, mod, w_glu, b_glu, w_out, seq):
    t = x2.shape[0]
    tm = min(OUT_TM, seq)
    tpb = seq // tm
    row = lambda w: pl.BlockSpec((tm, w), lambda i: (i, 0))
    full = lambda a, b: pl.BlockSpec((a, b), lambda i: (0, 0))
    return pl.pallas_call(
        _mixer_out_kernel,
        grid=(t // tm,),
        in_specs=[row(S5_WIDTH), row(MLA_HEADS * MLA_V), row(D_MODEL), _mod_specs(tpb, 2),
                  full(S5_WIDTH, S5_WIDTH), full(1, S5_WIDTH), full(2 * S5_WIDTH, D_MODEL)],
        out_specs=row(D_MODEL),
        out_shape=jax.ShapeDtypeStruct((t, D_MODEL), F32),
        compiler_params=_cparams(("arbitrary",)),
        name="mixer_out",
    )(ys, ym, x2, mod, w_glu, b_glu.reshape(1, -1), w_out)


def _route_kernel(x_ref, g_ref, shift_ref, scale_ref, rw_ref, rb_ref, tri_ref,
                  h_ref, route_ref, gates_ref, cnt_ref, carry_ref):
    i = pl.program_id(0)

    @pl.when(i == 0)
    def _():
        carry_ref[...] = jnp.zeros(carry_ref.shape, F32)

    h = _norm_mod(x_ref[...], g_ref[...], scale_ref[0], shift_ref[0])
    _pack_store(h_ref, h)
    logits = jnp.dot(h, rw_ref[...], precision=lax.Precision.HIGHEST,
                     preferred_element_type=F32) + rb_ref[...]
    lane = lax.broadcasted_iota(jnp.int32, logits.shape, 1)
    lanef = lane.astype(F32)
    lg = jnp.where(lane < N_EXPERTS, logits, -jnp.inf)
    vals, hots, idxs = [], [], []
    sel = jnp.zeros(logits.shape, F32)
    for _ in range(TOP_K):
        m = jnp.max(lg, axis=-1, keepdims=True)
        idx = jnp.min(jnp.where(lg == m, lanef, float(LANES)), axis=-1, keepdims=True)
        hot = lanef == idx
        vals.append(m)
        idxs.append(idx)
        hots.append(hot)
        sel = jnp.where(hot, 1.0, sel)
        lg = jnp.where(hot, -jnp.inf, lg)
    es = [jnp.exp(v - vals[0]) for v in vals]
    denom = es[0] + es[1] + es[2] + es[3]
    before = _dot(tri_ref[...], sel.astype(BF16)) + carry_ref[...]
    route = jnp.zeros(logits.shape, F32)
    gates = jnp.zeros(logits.shape, F32)
    for k in range(TOP_K):
        rank = jnp.sum(jnp.where(hots[k], before, 0.0), axis=-1, keepdims=True)
        route = jnp.where(lane == k, idxs[k], route)
        route = jnp.where(lane == TOP_K + k, rank, route)
        gates = jnp.where(lane == k, es[k] / denom, gates)
    route_ref[...] = route.astype(jnp.int32)
    gates_ref[...] = gates
    carry_ref[...] = carry_ref[...] + jnp.sum(sel, axis=0, keepdims=True)
    cnt_ref[...] = carry_ref[...]


def _route(x2, g, mod, router_w, router_b, seq):
    t = x2.shape[0]
    tm = min(ROUTE_TM, seq)
    tpb = seq // tm
    rw = jnp.zeros((D_MODEL, LANES), F32).at[:, :N_EXPERTS].set(router_w)
    rb = jnp.zeros((1, LANES), F32).at[0, :N_EXPERTS].set(router_b)
    tri = jnp.asarray(np.tril(np.ones((tm, tm), np.float32), -1), BF16)
    row = lambda w: pl.BlockSpec((tm, w), lambda i: (i, 0))
    full = lambda a, b: pl.BlockSpec((a, b), lambda i: (0, 0))
    return pl.pallas_call(
        _route_kernel,
        grid=(t // tm,),
        in_specs=[row(D_MODEL), full(1, D_MODEL), _mod_specs(tpb, 0), _mod_specs(tpb, 1),
                  full(D_MODEL, LANES), full(1, LANES), full(tm, tm)],
        out_specs=[pl.BlockSpec((tm * PACK, LANES), lambda i: (i, 0)), row(LANES), row(LANES), full(1, LANES)],
        out_shape=[jax.ShapeDtypeStruct((t * PACK, LANES), U32),
                   jax.ShapeDtypeStruct((t, LANES), jnp.int32),
                   jax.ShapeDtypeStruct((t, LANES), F32),
                   jax.ShapeDtypeStruct((1, LANES), F32)],
        scratch_shapes=[pltpu.VMEM((1, LANES), F32)],
        compiler_params=_cparams(("arbitrary",)),
        name="moe_route",
    )(x2, g.reshape(1, -1), mod, mod, rw, rb, tri)


def _dispatch_kernel(pstart_ref, npad_ref, used_ref, dest_ref, h_ref, xs_hbm, zero_ref, sem, zsem):
    i = pl.program_id(0)
    n = DISPATCH_T
    nslot = xs_hbm.shape[0] // PACK

    def pad_copy(slot):
        return pltpu.make_async_copy(zero_ref.at[_prow(0), :], xs_hbm.at[_prow(slot), :], zsem)

    def tail_copy(t):
        return pltpu.make_async_copy(zero_ref, xs_hbm.at[_prow(used_ref[0] + t * FFN_TM, FFN_TM), :], zsem)

    @pl.when(i == 0)
    def _():
        zero_ref[...] = jnp.zeros(zero_ref.shape, U32)
        ntail = (nslot - used_ref[0]) // FFN_TM
        for wait in (False, True):
            for e in range(N_EXPERTS):
                def pad(j, c):
                    cp = pad_copy(pstart_ref[e] + j)
                    cp.wait() if wait else cp.start()
                    return c
                lax.fori_loop(0, npad_ref[e], pad, 0)

            def tail(t, c):
                cp = tail_copy(t)
                cp.wait() if wait else cp.start()
                return c
            lax.fori_loop(0, ntail, tail, 0)

    def row_copy(r, slot):
        return pltpu.make_async_copy(h_ref.at[_prow(r), :], xs_hbm.at[_prow(slot), :], sem)

    for k in range(TOP_K):
        def body(j, c):
            for par in range(2):
                r = 2 * j + par
                row_copy(r, dest_ref[0, 0, k * n + r]).start(priority=par)
            return c
        lax.fori_loop(0, n // 2, body, 0, unroll=4)

    def wait_rows(r, c):
        row_copy(0, 0).wait()
        return c

    lax.fori_loop(0, n * TOP_K, wait_rows, 0, unroll=8)


def _unpack_kernel(x_ref, o_ref):
    n = o_ref.shape[0]
    half = D_MODEL // 2
    for s in range(PACK):
        lo, hi = _unpack_load(x_ref, n, s)
        o_ref[:, s * LANES:(s + 1) * LANES] = lo.astype(BF16)
        o_ref[:, half + s * LANES:half + (s + 1) * LANES] = hi.astype(BF16)


def _dispatch_rows(h_slabs, dest, pad_start, npad, used, ns):
    t = dest.shape[0]
    tm = min(DISPATCH_T, t)
    nt = t // tm
    idx = dest.reshape(nt, tm, TOP_K).transpose(0, 2, 1).reshape(nt, 1, TOP_K * tm)
    grid_spec = pltpu.PrefetchScalarGridSpec(
        num_scalar_prefetch=3,
        grid=(nt,),
        in_specs=[pl.BlockSpec((1, 1, TOP_K * tm), lambda i, *_: (i, 0, 0), memory_space=pltpu.SMEM),
                  pl.BlockSpec((tm * PACK, LANES), lambda i, *_: (i, 0))],
        out_specs=pl.BlockSpec(memory_space=pl.ANY),
        scratch_shapes=[pltpu.VMEM((FFN_TM * PACK, LANES), U32),
                        pltpu.SemaphoreType.DMA(()), pltpu.SemaphoreType.DMA(())],
    )
    xs_slabs = pl.pallas_call(
        _dispatch_kernel,
        grid_spec=grid_spec,
        out_shape=jax.ShapeDtypeStruct((ns * PACK, LANES), U32),
        compiler_params=_cparams(("arbitrary",)),
        name="moe_dispatch",
    )(pad_start, npad, used, idx, h_slabs)
    return pl.pallas_call(
        _unpack_kernel,
        grid=(ns // GATHER_T,),
        in_specs=[pl.BlockSpec((GATHER_T * PACK, LANES), lambda i: (i, 0))],
        out_specs=pl.BlockSpec((GATHER_T, D_MODEL), lambda i: (i, 0)),
        out_shape=jax.ShapeDtypeStruct((ns, D_MODEL), BF16),
        compiler_params=_cparams(("arbitrary",)),
        name="moe_unpack",
    )(xs_slabs)


PERM_W = 256
W1_BLK = 1024


def _w1_prep_kernel(w_ref, p_ref, o_ref):
    for b in range(W1_BLK // PERM_W):
        cols = slice(b * PERM_W, (b + 1) * PERM_W)
        o_ref[0, :, cols] = _dot(w_ref[0, :, cols].astype(BF16), p_ref[...]).astype(BF16)


def _w1_prep(w1):
    e, d, n = w1.shape
    pm = np.zeros((PERM_W, PERM_W), np.float32)
    jj = np.arange(PERM_W // 2)
    pm[2 * jj, jj] = 1.0
    pm[2 * jj + 1, PERM_W // 2 + jj] = 1.0
    return pl.pallas_call(
        _w1_prep_kernel,
        grid=(e, n // W1_BLK),
        in_specs=[pl.BlockSpec((1, d, W1_BLK), lambda i, j: (i, 0, j)),
                  pl.BlockSpec((PERM_W, PERM_W), lambda i, j: (0, 0))],
        out_specs=pl.BlockSpec((1, d, W1_BLK), lambda i, j: (i, 0, j)),
        out_shape=jax.ShapeDtypeStruct((e, d, n), BF16),
        compiler_params=_cparams(("arbitrary", "arbitrary")),
        name="moe_w1_prep",
    )(w1, jnp.asarray(pm, BF16))


def _ffn_kernel(we_ref, row0_ref, nt_ref, nv_ref, xs_hbm, w1_ref, b1_ref, w2_ref, b2_ref,
                ys_hbm, x_ref, acc_ref, w2b_ref, stage_ref, sem_in, sem_out):
    w = pl.program_id(0)
    f = pl.program_id(1)
    nf = pl.num_programs(1)
    nt = nt_ref[w]
    row0 = row0_ref[w]
    tm = FFN_TM
    hw = PERM_W // 2

    @pl.when(jnp.logical_and(f == 0, nt > 0))
    def _():
        def tile_copy(t):
            return pltpu.make_async_copy(xs_hbm.at[pl.ds(pl.multiple_of(row0 + t * tm, tm), tm), :],
                                         x_ref.at[pl.ds(pl.multiple_of(t * tm, tm), tm), :], sem_in)

        def start(t, c):
            tile_copy(t).start()
            return c

        def wait(t, c):
            tile_copy(t).wait()
            return c

        lax.fori_loop(0, nt, start, 0)
        lax.fori_loop(0, nt, wait, 0)

    def out_copy(t, slot):
        return pltpu.make_async_copy(stage_ref.at[slot], ys_hbm.at[_prow(row0 + t * tm, tm), :], sem_out.at[slot])

    @pl.when(nt > 0)
    def _():
        w2b_ref[...] = w2_ref[0].astype(BF16)

        @pl.when(f == 0)
        def _():
            def init(t, c):
                acc_ref[pl.ds(pl.multiple_of(t * tm, tm), tm), :] = jnp.broadcast_to(b2_ref[0], (tm, D_MODEL))
                return c
            lax.fori_loop(0, nt, init, 0)

        def tile(xv, av):
            a = _dot(xv[...], w1_ref[0]) + b1_ref[0]
            nblk = a.shape[1] // PERM_W
            glu = jnp.concatenate([a[:, b * PERM_W:b * PERM_W + hw] for b in range(nblk)], axis=1)
            lin = jnp.concatenate([a[:, b * PERM_W + hw:(b + 1) * PERM_W] for b in range(nblk)], axis=1)
            glu = jnp.minimum(glu, SWIGLU_LIMIT)
            lin = jnp.clip(lin, -SWIGLU_LIMIT, SWIGLU_LIMIT)
            act = glu * jax.nn.sigmoid(SWIGLU_ALPHA * glu) * (lin + 1.0)
            av[...] = av[...] + _dot(act.astype(BF16), w2b_ref[...])

        def pair(p, c):
            win = pl.ds(pl.multiple_of(p * 2 * tm, 2 * tm), 2 * tm)
            xw, aw = x_ref.at[win, :], acc_ref.at[win, :]
            for h in range(2):
                tile(xw.at[h * tm:(h + 1) * tm, :], aw.at[h * tm:(h + 1) * tm, :])
            return c

        lax.fori_loop(0, nt // 2, pair, 0)

        @pl.when(nt % 2 == 1)
        def _():
            last = pl.ds(pl.multiple_of((nt - 1) * tm, tm), tm)
            tile(x_ref.at[last, :], acc_ref.at[last, :])

        @pl.when(f == nf - 1)
        def _():
            def emit(t, c):
                slot = t % 2

                @pl.when(t >= 2)
                def _():
                    out_copy(t - 2, slot).wait()

                _pack_store(stage_ref, acc_ref[pl.ds(pl.multiple_of(t * tm, tm), tm), :], lead=(slot,))
                out_copy(t, slot).start()
                return c

            lax.fori_loop(0, nt, emit, 0)
            for back in (2, 1):
                @pl.when(nt >= back)
                def _():
                    out_copy(nt - back, (nt - back) % 2).wait()

    @pl.when(jnp.logical_and(w == pl.num_programs(0) - 1, f == nf - 1))
    def _():
        used = nv_ref[1]
        ntail = (ys_hbm.shape[0] // PACK - used) // tm
        stage_ref[0] = jnp.zeros(stage_ref.shape[1:], U32)

        def fill(t, c):
            cp = pltpu.make_async_copy(stage_ref.at[0], ys_hbm.at[_prow(used + t * tm, tm), :], sem_out.at[0])
            cp.start()
            cp.wait()
            return c

        lax.fori_loop(0, ntail, fill, 0)


def _ffn(xs, w1p, b1p, w2b, b2, we, row0, ntile, nvalid):
    ns, d = xs.shape
    nf = EXPERT_FF // FFN_FC
    wmax = we.shape[0]

    def wmap(axis):
        def index(w, f, we_r, row0_r, nt_r, nv_r):
            fe = jnp.where(w < nv_r[0], f, nf - 1)
            return (we_r[w], 0, fe) if axis == 2 else (we_r[w], fe, 0)
        return index

    grid_spec = pltpu.PrefetchScalarGridSpec(
        num_scalar_prefetch=4,
        grid=(wmax, nf),
        in_specs=[pl.BlockSpec(memory_space=pl.ANY),
                  pl.BlockSpec((1, d, 2 * FFN_FC), wmap(2)),
                  pl.BlockSpec((1, 1, 2 * FFN_FC), wmap(2)),
                  pl.BlockSpec((1, FFN_FC, d), wmap(1)),
                  pl.BlockSpec((1, 1, d), lambda w, f, we_r, *_: (we_r[w], 0, 0))],
        out_specs=pl.BlockSpec(memory_space=pl.ANY),
        scratch_shapes=[pltpu.VMEM((FFN_R, d), BF16),
                        pltpu.VMEM((FFN_R, d), F32),
                        pltpu.VMEM((FFN_FC, d), BF16),
                        pltpu.VMEM((2, FFN_TM * PACK, LANES), U32),
                        pltpu.SemaphoreType.DMA(()),
                        pltpu.SemaphoreType.DMA((2,))],
    )
    return pl.pallas_call(
        _ffn_kernel,
        grid_spec=grid_spec,
        out_shape=jax.ShapeDtypeStruct((ns * PACK, LANES), U32),
        compiler_params=_cparams(("arbitrary", "arbitrary")),
        name="moe_ffn",
    )(we, row0, ntile, nvalid, xs, w1p, b1p, w2b, b2)


def _combine_kernel(cur_ref, nxt_ref, ys_hbm, gates_ref, x_ref, gmod_ref, fg_ref, o_ref, buf_ref, sem, *, final):
    i = pl.program_id(0)
    n = x_ref.shape[0]
    slot = i % 2

    def row_copy(idx_ref, b, k, r):
        src = ys_hbm.at[_prow(idx_ref[0, 0, k * n + r]), :]
        return pltpu.make_async_copy(src, buf_ref.at[b, k, _prow(r), :], sem.at[b])

    def issue(idx_ref, b):
        for k in range(TOP_K):
            def body(j, c):
                for par in range(2):
                    row_copy(idx_ref, b, k, 2 * j + par).start(priority=par)
                return c
            lax.fori_loop(0, n // 2, body, 0, unroll=4)

    @pl.when(i == 0)
    def _():
        issue(cur_ref, 0)

    @pl.when(i + 1 < pl.num_programs(0))
    def _():
        issue(nxt_ref, 1 - slot)

    for k in range(TOP_K):
        def wait(r, c):
            row_copy(cur_ref, slot, k, r).wait()
            return c
        lax.fori_loop(0, n, wait, 0, unroll=8)

    gates = gates_ref[...]
    gk = [gates[:, k:k + 1] for k in range(TOP_K)]
    half = D_MODEL // 2
    for s in range(PACK):
        lo = hi = None
        for k in range(TOP_K):
            lo_k, hi_k = _unpack_load(buf_ref, n, s, lead=(slot, k))
            lo = gk[k] * lo_k if lo is None else lo + gk[k] * lo_k
            hi = gk[k] * hi_k if hi is None else hi + gk[k] * hi_k
        for base, acc in ((s * LANES, lo), (half + s * LANES, hi)):
            cols = slice(base, base + LANES)
            o_ref[:, cols] = x_ref[:, cols] + gmod_ref[0, :, cols] * acc
    if final:
        o_ref[...] = _rms(o_ref[...], fg_ref[...])


def _combine(ys, dest, gates, x2, mod, final_g, seq, final):
    t, d = x2.shape
    tm = min(COMBINE_T, seq)
    tpb = seq // tm
    nt = t // tm
    idx = dest.reshape(nt, tm, TOP_K).transpose(0, 2, 1).reshape(nt, 1, TOP_K * tm)
    row = lambda w: pl.BlockSpec((tm, w), lambda i: (i, 0))
    return pl.pallas_call(
        functools.partial(_combine_kernel, final=final),
        grid=(nt,),
        in_specs=[pl.BlockSpec((1, 1, tm * TOP_K), lambda i: (i, 0, 0), memory_space=pltpu.SMEM),
                  pl.BlockSpec((1, 1, tm * TOP_K), lambda i: (jnp.minimum(i + 1, nt - 1), 0, 0),
                               memory_space=pltpu.SMEM),
                  pl.BlockSpec(memory_space=pl.ANY),
                  row(LANES), row(d), _mod_specs(tpb, 2),
                  pl.BlockSpec((1, d), lambda i: (0, 0))],
        out_specs=row(d),
        out_shape=jax.ShapeDtypeStruct((t, d), F32),
        scratch_shapes=[pltpu.VMEM((2, TOP_K, tm * PACK, LANES), U32), pltpu.SemaphoreType.DMA((2,))],
        compiler_params=_cparams(("arbitrary",)),
        name="moe_combine",
    )(idx, idx, ys, gates, x2, mod, final_g.reshape(1, -1))


def _moe_sublayer(x2, c_mod, norm_g, router_w, router_b, w1, b1, w2, b2, final_g, seq, final):
    t = x2.shape[0]
    h, route, gates, cnt = _route(x2, norm_g, c_mod, router_w, router_b, seq)
    top_e = route[:, :TOP_K]
    rank = route[:, TOP_K:2 * TOP_K]
    counts = cnt[0, :N_EXPERTS].astype(jnp.int32)
    padded = ((counts + FFN_TM - 1) // FFN_TM) * FFN_TM
    starts = jnp.cumsum(padded) - padded
    dest = starts[top_e] + rank
    ns = t * TOP_K + N_EXPERTS * FFN_TM
    ns = ((ns + GATHER_T - 1) // GATHER_T) * GATHER_T
    used = jnp.sum(padded).astype(jnp.int32)
    n_items = (padded + FFN_R - 1) // FFN_R
    item_end = jnp.cumsum(n_items)
    wmax = (t * TOP_K) // FFN_R + N_EXPERTS
    wid = jnp.arange(wmax, dtype=jnp.int32)
    nvalid = item_end[-1].astype(jnp.int32)
    we = jnp.minimum(jnp.searchsorted(item_end, wid, side='right'), N_EXPERTS - 1).astype(jnp.int32)
    last_e = we[jnp.maximum(nvalid - 1, 0)]
    valid = wid < nvalid
    we = jnp.where(valid, we, last_e)
    local = wid - (item_end - n_items)[we]
    row0 = jnp.where(valid, starts[we] + local * FFN_R, 0).astype(jnp.int32)
    ntile = jnp.where(valid, jnp.minimum(FFN_R, padded[we] - local * FFN_R) // FFN_TM, 0).astype(jnp.int32)

    xs = _dispatch_rows(h, dest.astype(jnp.int32), (starts + counts).astype(jnp.int32),
                        (padded - counts).astype(jnp.int32), used.reshape(1), ns)
    hw = PERM_W // 2
    b1p = b1.reshape(N_EXPERTS, 2 * EXPERT_FF // PERM_W, hw, 2).transpose(0, 1, 3, 2).reshape(N_EXPERTS, 1, 2 * EXPERT_FF)
    ys = _ffn(xs, _w1_prep(w1), b1p, w2, b2.reshape(N_EXPERTS, 1, D_MODEL),
              we, row0, ntile, jnp.stack([nvalid, used]))
    return _combine(ys, dest.astype(jnp.int32), gates, x2, c_mod, final_g, seq, final)


def _conv_pw1_kernel(x_ref, g_ref, shift_ref, scale_ref, wa_ref, wb_ref, ba_ref, bb_ref, o_ref, h_ref):
    j = pl.program_id(1)

    @pl.when(j == 0)
    def _():
        h_ref[...] = _norm_mod(x_ref[...], g_ref[...], scale_ref[0], shift_ref[0]).astype(BF16)

    h = h_ref[...]
    a = _dot(h, wa_ref[...]) + ba_ref[...]
    b = _dot(h, wb_ref[...]) + bb_ref[...]
    o_ref[...] = a * jax.nn.sigmoid(b)


def _conv_pw1(x2, g, mod, w_pw1, b_pw1, seq):
    t = x2.shape[0]
    tm = min(PROJ_TM, seq)
    tpb = seq // tm
    tn = 512
    nj = D_MODEL // tn
    mod2 = lambda which: pl.BlockSpec((1, 1, D_MODEL), lambda i, j: (i // tpb, 0, which))
    b2 = b_pw1.reshape(1, -1)
    return pl.pallas_call(
        _conv_pw1_kernel,
        grid=(t // tm, nj),
        in_specs=[pl.BlockSpec((tm, D_MODEL), lambda i, j: (i, 0)),
                  pl.BlockSpec((1, D_MODEL), lambda i, j: (0, 0)),
                  mod2(0), mod2(1),
                  pl.BlockSpec((D_MODEL, tn), lambda i, j: (0, j)),
                  pl.BlockSpec((D_MODEL, tn), lambda i, j: (0, j + nj)),
                  pl.BlockSpec((1, tn), lambda i, j: (0, j)),
                  pl.BlockSpec((1, tn), lambda i, j: (0, j + nj))],
        out_specs=pl.BlockSpec((tm, tn), lambda i, j: (i, j)),
        out_shape=jax.ShapeDtypeStruct((t, D_MODEL), F32),
        scratch_shapes=[pltpu.VMEM((tm, D_MODEL), BF16)],
        compiler_params=_cparams(("arbitrary", "arbitrary")),
        name="conv_pw1",
    )(x2, g.reshape(1, -1), mod, mod, w_pw1, w_pw1, b2, b2)


def _conv_dw_kernel(y_ref, halo_ref, wdw_ref, bdw_ref, lng_ref, lnb_ref, wp_ref, bp_ref, x_ref, gate_ref,
                    o_ref, buf_ref, z_ref, win_ref, *, tpb):
    i = pl.program_id(0)
    tm = y_ref.shape[0]
    first = (i % tpb) == 0
    halo = halo_ref[...]
    buf_ref[0:CONV_HALO, :] = jnp.where(first, 0.0, halo)
    buf_ref[CONV_HALO:, :] = y_ref[...]
    cw = CONV_CW
    rb = min(CONV_RB, tm)
    off = CONV_HALO - (CONV_KERNEL - 1)
    for c0 in range(0, D_MODEL, cw):
        for r0 in range(0, tm, rb):
            acc = jnp.zeros((rb, cw), F32)
            for j in range(SUBLANES):
                taps = [k for k in range(CONV_KERNEL) if (off + k) % SUBLANES == j]
                span = max(off + k for k in taps) - j + rb
                win_ref[0:span, :] = buf_ref[r0 + j:r0 + j + span, c0:c0 + cw]
                for k in taps:
                    q = off + k - j
                    acc = acc + wdw_ref[k:k + 1, c0:c0 + cw] * win_ref[q:q + rb, :]
            z_ref[r0:r0 + rb, c0:c0 + cw] = acc
    z = z_ref[...] + bdw_ref[...]
    mu = jnp.mean(z, axis=-1, keepdims=True)
    zc = z - mu
    var = jnp.mean(zc * zc, axis=-1, keepdims=True)
    zn = zc * lax.rsqrt(var + LN_EPS) * lng_ref[...] + lnb_ref[...]
    act = (zn * jax.nn.sigmoid(zn)).astype(BF16)
    o_ref[...] = x_ref[...] + gate_ref[0] * (_dot(act, wp_ref[...]) + bp_ref[...])


def _conv_dw(y1, x2, mod, w_dw, b_dw, ln_g, ln_b, w_pw2, b_pw2, seq):
    t = x2.shape[0]
    tm = min(CONV_TM, seq)
    tpb = seq // tm
    hb = tm // CONV_HALO
    wdw = jnp.zeros((CONV_HALO, D_MODEL), F32).at[:CONV_KERNEL].set(w_dw)
    row = lambda: pl.BlockSpec((tm, D_MODEL), lambda i: (i, 0))
    vec = lambda: pl.BlockSpec((1, D_MODEL), lambda i: (0, 0))
    return pl.pallas_call(
        functools.partial(_conv_dw_kernel, tpb=tpb),
        grid=(t // tm,),
        in_specs=[row(),
                  pl.BlockSpec((CONV_HALO, D_MODEL), lambda i: (jnp.maximum(i * hb - 1, 0), 0)),
                  pl.BlockSpec((CONV_HALO, D_MODEL), lambda i: (0, 0)),
                  vec(), vec(), vec(),
                  pl.BlockSpec((D_MODEL, D_MODEL), lambda i: (0, 0)),
                  vec(), row(), _mod_specs(tpb, 2)],
        out_specs=row(),
        out_shape=jax.ShapeDtypeStruct((t, D_MODEL), F32),
        scratch_shapes=[pltpu.VMEM((tm + CONV_HALO, D_MODEL), F32), pltpu.VMEM((tm, D_MODEL), F32),
                        pltpu.VMEM((CONV_RB + CONV_HALO, CONV_CW), F32)],
        compiler_params=_cparams(("arbitrary",)),
        name="conv_dw_pw2",
    )(y1, y1, wdw, b_dw.reshape(1, -1), ln_g.reshape(1, -1), ln_b.reshape(1, -1),
      w_pw2, b_pw2.reshape(1, -1), x2, mod)


def _rope_swap_cols(w, heads):
    k = w.shape[0]
    half = MLA_ROPE // 2
    w3 = w.reshape(k, heads, MLA_NOPE + MLA_ROPE)
    pe = w3[:, :, MLA_NOPE:]
    sw = jnp.concatenate([pe[:, :, half:], pe[:, :, :half]], axis=-1)
    return jnp.concatenate([w3, sw], axis=-1).reshape(k, heads * (MLA_NOPE + 2 * MLA_ROPE))


def kernel(x, c, positions, l0_mix_norm_g, l0_mix_ada_w, l0_mix_ada_b, l0_w_in, l0_s5_lambda_re, l0_s5_lambda_im, l0_s5_log_step, l0_s5_b_re, l0_s5_b_im, l0_s5_c_re, l0_s5_c_im, l0_s5_d, l0_s5_w_glu, l0_s5_b_glu, l0_mla_q_norm_g, l0_mla_w_uq, l0_mla_kv_norm_g, l0_mla_w_ukv, l0_w_out, l0_moe_norm_g, l0_moe_ada_w, l0_moe_ada_b, l0_router_w, l0_router_b, l0_exp_w1, l0_exp_b1, l0_exp_w2, l0_exp_b2, l1_mix_norm_g, l1_mix_ada_w, l1_mix_ada_b, l1_conv_w_pw1, l1_conv_b_pw1, l1_conv_w_dw, l1_conv_b_dw, l1_conv_ln_g, l1_conv_ln_b, l1_conv_w_pw2, l1_conv_b_pw2, l1_moe_norm_g, l1_moe_ada_w, l1_moe_ada_b, l1_router_w, l1_router_b, l1_exp_w1, l1_exp_b1, l1_exp_w2, l1_exp_b2, final_norm_g):
    nbatch, seq, d = x.shape
    t = nbatch * seq
    x2 = x.reshape(t, d)
    pos = positions.reshape(t, 1).astype(jnp.int32)

    mod = _ada_mod(c, l0_mix_ada_w, l0_mix_ada_b)
    half = MLA_ROPE // 2
    kcol = S5_WIDTH + MLA_Q_RANK + MLA_KV_RANK
    w_in_ext = jnp.concatenate([l0_w_in, l0_w_in[:, kcol + half:], l0_w_in[:, kcol:kcol + half]], axis=1).astype(BF16)
    u, cq, ckv, kr = _proj_in(x2, l0_mix_norm_g, mod, w_in_ext, seq)

    kt, wt, vv, a16 = _s5_prep(l0_s5_lambda_re, l0_s5_lambda_im, l0_s5_log_step, l0_s5_b_re, l0_s5_b_im,
                               l0_s5_c_re, l0_s5_c_im, l0_s5_d)
    nchunk = seq // S5_CHUNK
    u_r = (u.reshape(nbatch, nchunk, S5_CHUNK, S5_GROUPS, S5_GROUP_CH)
           .transpose(3, 0, 1, 2, 4).reshape(S5_GROUPS, nbatch * nchunk, S5_ROW))
    y_r = _s5_main(u_r, kt, wt, vv, a16, nbatch, nchunk)
    ys = (y_r.reshape(S5_GROUPS, nbatch, nchunk, S5_CHUNK, S5_GROUP_CH)
          .transpose(1, 2, 3, 0, 4).reshape(t, S5_WIDTH))

    wq_ext = _rope_swap_cols(l0_mla_w_uq, MLA_HEADS).astype(BF16)
    q, k, v = _mla_proj(cq, ckv, kr, pos, l0_mla_q_norm_g, l0_mla_kv_norm_g, wq_ext,
                        l0_mla_w_ukv.astype(BF16), nbatch, seq)
    ym = _flash_attention(q, k, v).reshape(t, MLA_HEADS * MLA_V)
    x2 = _mixer_out(ys, ym, x2, mod, l0_s5_w_glu.astype(BF16), l0_s5_b_glu, l0_w_out.astype(BF16), seq)

    mod = _ada_mod(c, l0_moe_ada_w, l0_moe_ada_b)
    x2 = _moe_sublayer(x2, mod, l0_moe_norm_g, l0_router_w, l0_router_b, l0_exp_w1, l0_exp_b1,
                       l0_exp_w2, l0_exp_b2, final_norm_g, seq, final=False)

    mod = _ada_mod(c, l1_mix_ada_w, l1_mix_ada_b)
    y1 = _conv_pw1(x2, l1_mix_norm_g, mod, l1_conv_w_pw1.astype(BF16), l1_conv_b_pw1, seq)
    x2 = _conv_dw(y1, x2, mod, l1_conv_w_dw, l1_conv_b_dw, l1_conv_ln_g, l1_conv_ln_b,
                  l1_conv_w_pw2.astype(BF16), l1_conv_b_pw2, seq)

    mod = _ada_mod(c, l1_moe_ada_w, l1_moe_ada_b)
    x2 = _moe_sublayer(x2, mod, l1_moe_norm_g, l1_router_w, l1_router_b, l1_exp_w1, l1_exp_b1,
                       l1_exp_w2, l1_exp_b2, final_norm_g, seq, final=True)
    return x2.reshape(nbatch, seq, d)
```

```python
import functools
import math

import numpy as np
import jax
import jax.numpy as jnp
from jax import lax
from jax.experimental import pallas as pl
from jax.experimental.pallas import tpu as pltpu

F32 = jnp.float32
BF16 = jnp.bfloat16

D_MODEL = 2048
S5_WIDTH = 1024
S5_GROUP_CH = 16
S5_GROUPS = 64
S5_STATE = 64
MLA_HEADS = 8
MLA_NOPE = 128
MLA_ROPE = 64
MLA_V = 128
MLA_Q_RANK = 512
MLA_KV_RANK = 512
ROPE_THETA = 10000.0
CONV_KERNEL = 31
N_EXPERTS = 32
TOP_K = 4
EXPERT_FF = 2048
SWIGLU_ALPHA = 1.702
SWIGLU_LIMIT = 7.0
EPS = 1e-6
LN_EPS = 1e-5
NEG_BIG = -1e30

LANES = 128
SUBLANES = 8
VMEM_LIMIT_BYTES = 56 * 1024 * 1024

S5_CHUNK = 16
S5_ROW = S5_CHUNK * S5_GROUP_CH
PROJ_TM = 512
MLA_TM = 256
ATT_T = 512
OUT_TM = 256
ROUTE_TM = 512
CONV_TM = 256
CONV_HALO = 32
CONV_RB = 128
CONV_CW = 256
FFN_TM = 256
FFN_R = 2304
FFN_FC = 512
GATHER_T = 512
DISPATCH_T = 512
COMBINE_T = 256


def _cparams(sem):
    return pltpu.CompilerParams(dimension_semantics=sem, vmem_limit_bytes=VMEM_LIMIT_BYTES)


def _norm_mod(x, g, scale, shift):
    ms = jnp.mean(x * x, axis=-1, keepdims=True)
    return (x * lax.rsqrt(ms + EPS)) * g * (1.0 + scale) + shift


def _rms(x, g):
    ms = jnp.mean(x * x, axis=-1, keepdims=True)
    return (x * lax.rsqrt(ms + EPS)) * g


def _dot(a, b):
    return jnp.dot(a, b, preferred_element_type=F32)


def _dot_nt(a, b):
    return lax.dot_general(a, b, (((1,), (1,)), ((), ())), preferred_element_type=F32)


U32 = jnp.uint32
PACK = D_MODEL // (2 * LANES)
HI_MASK = 0xFFFF0000


def _pack_store(ref, val, lead=()):
    n = val.shape[0]
    half = D_MODEL // 2
    for s in range(PACK):
        lo = val[:, s * LANES:(s + 1) * LANES].astype(BF16).astype(F32)
        hi = val[:, half + s * LANES:half + (s + 1) * LANES].astype(BF16).astype(F32)
        word = (lax.bitcast_convert_type(lo, U32) >> 16) | (lax.bitcast_convert_type(hi, U32) & jnp.uint32(HI_MASK))
        ref[lead + (pl.ds(s, n, stride=PACK), slice(None))] = word


def _unpack_load(ref, n, s, lead=()):
    word = ref[lead + (pl.ds(s, n, stride=PACK), slice(None))]
    lo = lax.bitcast_convert_type(word << 16, F32)
    hi = lax.bitcast_convert_type(word & jnp.uint32(HI_MASK), F32)
    return lo, hi


def _prow(r, n=1):
    return pl.ds(pl.multiple_of(r * PACK, PACK), n * PACK)


ADA_TN = 768
ADA_KC = 256


def _ada_kernel(ct_ref, w_ref, b_ref, o_ref):
    nb = ct_ref.shape[0]
    kdim = w_ref.shape[0]
    rows = []
    for b in range(nb):
        acc = jnp.zeros((1, w_ref.shape[1]), F32)
        for k0 in range(0, kdim, ADA_KC):
            c = ct_ref[b, k0:k0 + ADA_KC, :]
            cs = c * jax.nn.sigmoid(c)
            acc = acc + jnp.sum(w_ref[k0:k0 + ADA_KC, :] * cs, axis=0, keepdims=True)
        rows.append(acc)
    o_ref[...] = jnp.concatenate(rows, axis=0) + b_ref[...]


def _ada_mod(c, w, b):
    nb, d = c.shape
    n = w.shape[1]
    ct = c.reshape(nb, d, 1)
    m = pl.pallas_call(
        _ada_kernel,
        grid=(n // ADA_TN,),
        in_specs=[pl.BlockSpec((nb, d, 1), lambda j: (0, 0, 0)),
                  pl.BlockSpec((d, ADA_TN), lambda j: (0, j)),
                  pl.BlockSpec((1, ADA_TN), lambda j: (0, j))],
        out_specs=pl.BlockSpec((nb, ADA_TN), lambda j: (0, j)),
        out_shape=jax.ShapeDtypeStruct((nb, n), F32),
        compiler_params=_cparams(("arbitrary",)),
        name="ada_mod",
    )(ct, w, b.reshape(1, n))
    return m.reshape(nb, 1, n)


def _mod_specs(tiles_per_batch, which):
    return pl.BlockSpec((1, 1, D_MODEL), lambda i, *_: (i // tiles_per_batch, 0, which))


def _proj_in_kernel(x_ref, g_ref, shift_ref, scale_ref, w_ref, u_ref, cq_ref, ckv_ref, kr_ref):
    h = _norm_mod(x_ref[...], g_ref[...], scale_ref[0], shift_ref[0]).astype(BF16)
    acc = _dot(h, w_ref[...])
    c0, c1, c2 = S5_WIDTH, S5_WIDTH + MLA_Q_RANK, S5_WIDTH + MLA_Q_RANK + MLA_KV_RANK
    u_ref[...] = acc[:, :c0].astype(BF16)
    cq_ref[...] = acc[:, c0:c1]
    ckv_ref[...] = acc[:, c1:c2]
    kr_ref[...] = acc[:, c2:]


def _proj_in(x2, g, mod, w_ext, seq):
    t = x2.shape[0]
    tm = min(PROJ_TM, seq)
    tpb = seq // tm
    n = w_ext.shape[1]
    row = lambda w: pl.BlockSpec((tm, w), lambda i: (i, 0))
    return pl.pallas_call(
        _proj_in_kernel,
        grid=(t // tm,),
        in_specs=[row(D_MODEL),
                  pl.BlockSpec((1, D_MODEL), lambda i: (0, 0)),
                  _mod_specs(tpb, 0), _mod_specs(tpb, 1),
                  pl.BlockSpec((D_MODEL, n), lambda i: (0, 0))],
        out_specs=[row(S5_WIDTH), row(MLA_Q_RANK), row(MLA_KV_RANK), row(2 * MLA_ROPE)],
        out_shape=[jax.ShapeDtypeStruct((t, S5_WIDTH), BF16),
                   jax.ShapeDtypeStruct((t, MLA_Q_RANK), F32),
                   jax.ShapeDtypeStruct((t, MLA_KV_RANK), F32),
                   jax.ShapeDtypeStruct((t, 2 * MLA_ROPE), F32)],
        compiler_params=_cparams(("arbitrary",)),
        name="proj_in",
    )(x2, g.reshape(1, -1), mod, mod, w_ext)


def _s5_prep_kernel(lamc_ref, lamr_ref, step_ref, bt_ref, btt_ref, ct_ref, d_ref,
                    kt_ref, wt_ref, v_ref, a16_ref):
    P, H, C = S5_STATE, S5_GROUP_CH, S5_CHUNK
    step = step_ref[0]
    step = jnp.exp(step)
    lr_c = jnp.minimum(lamc_ref[0, 0], -1e-4)
    li_c = lamc_ref[0, 1]
    lr_r = jnp.minimum(lamr_ref[0, 0:1, :], -1e-4)
    li_r = lamr_ref[0, 1:2, :]

    def ratio(lr, li):
        mag = jnp.exp(lr * step)
        ab_re = mag * jnp.cos(li * step)
        ab_im = mag * jnp.sin(li * step)
        denom = lr * lr + li * li
        nr, ni = ab_re - 1.0, ab_im
        return (nr * lr + ni * li) / denom, (ni * lr - nr * li) / denom

    rr_c, ri_c = ratio(lr_c, li_c)
    rr_r, ri_r = ratio(lr_r, li_r)

    lane = lax.broadcasted_iota(jnp.int32, (1, S5_ROW), 1)
    kk = (lane // H).astype(F32)

    def powers(k):
        mag = jnp.exp(lr_c * step * k)
        return mag * jnp.cos(li_c * step * k), mag * jnp.sin(li_c * step * k)

    bre_t, bim_t = bt_ref[0, 0], bt_ref[0, 1]
    bbt_re = rr_c * bre_t - ri_c * bim_t
    bbt_im = rr_c * bim_t + ri_c * bre_t
    cre_t, cim_t = ct_ref[0, 0], ct_ref[0, 1]

    er, ei = powers(float(C - 1) - kk)
    wt_ref[0, 0:P, :] = (er * bbt_re - ei * bbt_im).astype(BF16)
    wt_ref[0, P:2 * P, :] = (er * bbt_im + ei * bbt_re).astype(BF16)

    er, ei = powers(kk + 1.0)
    v_ref[0, 0:P, :] = (cre_t * er - cim_t * ei).astype(BF16)
    v_ref[0, P:2 * P, :] = (-cre_t * ei - cim_t * er).astype(BF16)

    er, ei = powers(kk)
    q_re = er * cre_t - ei * cim_t
    q_im = er * cim_t + ei * cre_t
    brt, bit = btt_ref[0, 0], btt_ref[0, 1]
    bbr = rr_r * brt - ri_r * bit
    bbi = rr_r * bit + ri_r * brt
    hi = lax.Precision.HIGHEST
    mall = (jnp.dot(bbr, q_re, precision=hi, preferred_element_type=F32)
            - jnp.dot(bbi, q_im, precision=hi, preferred_element_type=F32))
    rowh = lax.broadcasted_iota(jnp.int32, (H, S5_ROW), 0)
    laneh = lax.broadcasted_iota(jnp.int32, (H, S5_ROW), 1)
    mall = mall + jnp.where(laneh == rowh, d_ref[0], 0.0)
    for s in range(C):
        piece = mall if s == 0 else pltpu.roll(mall, H * s, 1)
        piece = jnp.where(laneh >= H * s, piece, 0.0)
        kt_ref[0, s * H:(s + 1) * H, :] = piece.astype(BF16)

    mag = jnp.exp(lr_r * step * float(C))
    a16_ref[0, 0:1, :] = mag * jnp.cos(li_r * step * float(C))
    a16_ref[0, 1:2, :] = mag * jnp.sin(li_r * step * float(C))


def _s5_prep(lam_re, lam_im, log_step, b_re, b_im, c_re, c_im, d):
    G, P, H, C = S5_GROUPS, S5_STATE, S5_GROUP_CH, S5_CHUNK
    lam = jnp.stack([lam_re, lam_im], axis=1).astype(F32)
    lamc = lam.reshape(G, 2, P, 1)
    b = jnp.stack([b_re, b_im], axis=1).astype(F32)
    bt = jnp.tile(b, (1, 1, 1, C))
    btt = jnp.swapaxes(b, 2, 3)
    c = jnp.stack([c_re, c_im], axis=1).astype(F32)
    ct = jnp.tile(jnp.swapaxes(c, 2, 3), (1, 1, 1, C))
    dt = jnp.tile(d.astype(F32), (1, C)).reshape(G, 1, S5_ROW)
    step = log_step.astype(F32).reshape(G, 1, 1)
    g4 = lambda *shape: pl.BlockSpec((1,) + shape, lambda g: (g,) + (0,) * len(shape))
    return pl.pallas_call(
        _s5_prep_kernel,
        grid=(G,),
        in_specs=[g4(2, P, 1), g4(2, P), g4(1, 1), g4(2, P, S5_ROW), g4(2, H, P),
                  g4(2, P, S5_ROW), g4(1, S5_ROW)],
        out_specs=[g4(S5_ROW, S5_ROW), g4(2 * P, S5_ROW), g4(2 * P, S5_ROW), g4(2, P)],
        out_shape=[jax.ShapeDtypeStruct((G, S5_ROW, S5_ROW), BF16),
                   jax.ShapeDtypeStruct((G, 2 * P, S5_ROW), BF16),
                   jax.ShapeDtypeStruct((G, 2 * P, S5_ROW), BF16),
                   jax.ShapeDtypeStruct((G, 2, P), F32)],
        compiler_params=_cparams(("arbitrary",)),
        name="s5_prep",
    )(lamc, lam, step, bt, btt, ct, dt)


def _s5_main_kernel(u_ref, kt_ref, wt_ref, v_ref, a16_ref, y_ref, sr_ref, si_ref, xr_ref, xi_ref,
                    *, nbatch, nchunk):
    P = S5_STATE
    u = u_ref[0]
    sr_ref[...] = _dot_nt(u, wt_ref[0, 0:P, :])
    si_ref[...] = _dot_nt(u, wt_ref[0, P:2 * P, :])
    ar = a16_ref[0, 0:1, :]
    ai = a16_ref[0, 1:2, :]

    def step(c, carry):
        out = []
        for b in range(nbatch):
            xr, xi = carry[2 * b], carry[2 * b + 1]
            r = b * nchunk + c
            xr_ref[pl.ds(r, 1), :] = xr
            xi_ref[pl.ds(r, 1), :] = xi
            nxr = ar * xr - ai * xi + sr_ref[pl.ds(r, 1), :]
            nxi = ar * xi + ai * xr + si_ref[pl.ds(r, 1), :]
            out += [nxr, nxi]
        return tuple(out)

    zero = jnp.zeros((1, P), F32)
    lax.fori_loop(0, nchunk, step, (zero,) * (2 * nbatch))
    y = _dot(u, kt_ref[0])
    y = y + _dot(xr_ref[...].astype(BF16), v_ref[0, 0:P, :])
    y = y + _dot(xi_ref[...].astype(BF16), v_ref[0, P:2 * P, :])
    y_ref[0] = y.astype(y_ref.dtype)


def _s5_main(u_r, kt, wt, v, a16, nbatch, nchunk):
    G, rows, _ = u_r.shape
    P = S5_STATE
    g3 = lambda a, b: pl.BlockSpec((1, a, b), lambda g: (g, 0, 0))
    return pl.pallas_call(
        functools.partial(_s5_main_kernel, nbatch=nbatch, nchunk=nchunk),
        grid=(G,),
        in_specs=[g3(rows, S5_ROW), g3(S5_ROW, S5_ROW), g3(2 * P, S5_ROW), g3(2 * P, S5_ROW), g3(2, P)],
        out_specs=g3(rows, S5_ROW),
        out_shape=jax.ShapeDtypeStruct((G, rows, S5_ROW), BF16),
        scratch_shapes=[pltpu.VMEM((rows, P), F32) for _ in range(4)],
        compiler_params=_cparams(("arbitrary",)),
        name="s5_main",
    )(u_r, kt, wt, v, a16)


def _mla_proj_kernel(cq_ref, ckv_ref, kr_ref, pos_ref, qg_ref, kvg_ref, wq_ref, wkv_ref,
                     invf_ref, sgn_ref, q_ref, k_ref, v_ref):
    qscale = 1.0 / math.sqrt(MLA_NOPE + MLA_ROPE)
    qa = _dot(_rms(cq_ref[...], qg_ref[...]).astype(BF16), wq_ref[...])
    kva = _dot(_rms(ckv_ref[...], kvg_ref[...]).astype(BF16), wkv_ref[...])
    ang = pos_ref[...].astype(F32) * invf_ref[...]
    cc = jnp.cos(ang)
    ss = jnp.sin(ang) * sgn_ref[...]

    def rope(slab):
        return slab * cc + pltpu.roll(slab, MLA_ROPE, 1) * ss

    kpe = rope(kr_ref[...])[:, :MLA_ROPE].astype(BF16)
    hw = MLA_NOPE + 2 * MLA_ROPE
    lane = lax.broadcasted_iota(jnp.int32, (cq_ref.shape[0], MLA_V), 1)
    ones_col = jnp.where(lane == 0, 1.0, 0.0).astype(BF16)
    for h in range(MLA_HEADS):
        blk = qa[:, h * hw:(h + 1) * hw]
        q_ref[0, h, :, 0:MLA_NOPE] = (blk[:, :MLA_NOPE] * qscale).astype(BF16)
        qpe = rope(blk[:, MLA_NOPE:]) * qscale
        q_ref[0, h, :, MLA_NOPE:MLA_NOPE + MLA_ROPE] = qpe[:, :MLA_ROPE].astype(BF16)
        kvb = kva[:, h * hw:(h + 1) * hw]
        k_ref[0, h, :, 0:MLA_NOPE] = kvb[:, :MLA_NOPE].astype(BF16)
        k_ref[0, h, :, MLA_NOPE:MLA_NOPE + MLA_ROPE] = kpe
        v_ref[0, h, :, 0:MLA_V] = kvb[:, MLA_NOPE:].astype(BF16)
        v_ref[0, h, :, MLA_V:2 * MLA_V] = ones_col


def _mla_proj(cq, ckv, kr, pos, qg, kvg, wq_ext, wkv, nbatch, seq):
    tm = min(MLA_TM, seq)
    nl = seq // tm
    dqk = MLA_NOPE + MLA_ROPE
    half = MLA_ROPE // 2
    inv_freq = 1.0 / (ROPE_THETA ** (jnp.arange(0, MLA_ROPE, 2, dtype=F32) / MLA_ROPE))
    invf = jnp.tile(inv_freq, 4).reshape(1, 2 * MLA_ROPE)
    sgn = jnp.tile(jnp.concatenate([-jnp.ones((half,), F32), jnp.ones((half,), F32)]), 2).reshape(1, 2 * MLA_ROPE)
    row = lambda w: pl.BlockSpec((tm, w), lambda b, i: (b * nl + i, 0))
    full = lambda a, b_: pl.BlockSpec((a, b_), lambda b, i: (0, 0))
    head = lambda w: pl.BlockSpec((1, MLA_HEADS, tm, w), lambda b, i: (b, 0, i, 0))
    return pl.pallas_call(
        _mla_proj_kernel,
        grid=(nbatch, nl),
        in_specs=[row(MLA_Q_RANK), row(MLA_KV_RANK), row(2 * MLA_ROPE), row(1),
                  full(1, MLA_Q_RANK), full(1, MLA_KV_RANK),
                  full(MLA_Q_RANK, wq_ext.shape[1]), full(MLA_KV_RANK, wkv.shape[1]),
                  full(1, 2 * MLA_ROPE), full(1, 2 * MLA_ROPE)],
        out_specs=[head(dqk), head(dqk), head(2 * MLA_V)],
        out_shape=[jax.ShapeDtypeStruct((nbatch, MLA_HEADS, seq, dqk), BF16),
                   jax.ShapeDtypeStruct((nbatch, MLA_HEADS, seq, dqk), BF16),
                   jax.ShapeDtypeStruct((nbatch, MLA_HEADS, seq, 2 * MLA_V), BF16)],
        compiler_params=_cparams(("arbitrary", "arbitrary")),
        name="mla_proj",
    )(cq, ckv, kr, pos, qg.reshape(1, -1), kvg.reshape(1, -1), wq_ext, wkv, invf, sgn)


def _lane_groups(x):
    return [x[:, j * LANES:(j + 1) * LANES] for j in range(x.shape[1] // LANES)]


def _flash_kernel(q_ref, k_ref, v_ref, o_ref, s_ref, *, t):
    qi = pl.program_id(2)
    qh = [q_ref[0, 0, 0:t, :], q_ref[0, 0, t:2 * t, :]]

    def kblk(c):
        return k_ref[0, 0, pl.ds(pl.multiple_of(c * t, t), t), :]

    def vblk(c):
        return v_ref[0, 0, pl.ds(pl.multiple_of(c * t, t), t), :]

    def fold_max(s, mx):
        return functools.reduce(jnp.maximum, _lane_groups(s), mx)

    def pass1(c, mx):
        kc = kblk(c)
        out = []
        for h in range(2):
            s = _dot_nt(qh[h], kc)
            s_ref[h, c] = s
            out.append(fold_max(s, mx[h]))
        return tuple(out)

    neg = jnp.full((t, LANES), NEG_BIG, F32)
    mx0, mx1 = lax.fori_loop(0, 2 * qi, pass1, (neg, neg))
    c0, c1 = 2 * qi, 2 * qi + 1
    row = lax.broadcasted_iota(jnp.int32, (t, t), 0)
    col = lax.broadcasted_iota(jnp.int32, (t, t), 1)
    tri = col <= row
    k0, k1 = kblk(c0), kblk(c1)
    s00 = jnp.where(tri, _dot_nt(qh[0], k0), NEG_BIG)
    s10 = _dot_nt(qh[1], k0)
    s11 = jnp.where(tri, _dot_nt(qh[1], k1), NEG_BIG)
    s_ref[0, c0] = s00
    s_ref[1, c0] = s10
    s_ref[1, c1] = s11
    mx0 = fold_max(s00, mx0)
    mx1 = fold_max(s11, fold_max(s10, mx1))
    m = [jnp.broadcast_to(jnp.max(mx, axis=-1, keepdims=True), (t, LANES)) for mx in (mx0, mx1)]

    mt = [jnp.concatenate([mh] * (t // LANES), axis=1) for mh in m]

    def probs(h, c):
        return jnp.exp(s_ref[h, c] - mt[h]).astype(BF16)

    def pass2(j, carry):
        c = 2 * j
        vc = v_ref[0, 0, pl.ds(pl.multiple_of(c * t, 2 * t), 2 * t), :]
        out = []
        for h in range(2):
            p = jnp.concatenate([probs(h, c), probs(h, c + 1)], axis=1)
            out.append(carry[h] + _dot(p, vc))
        return tuple(out)

    za = jnp.zeros((t, 2 * MLA_V), F32)
    a0, a1 = lax.fori_loop(0, qi, pass2, (za, za))
    v0, v1 = vblk(c0), vblk(c1)
    a0 = a0 + _dot(probs(0, c0), v0)
    a1 = a1 + _dot(jnp.concatenate([probs(1, c0), probs(1, c1)], axis=1), jnp.concatenate([v0, v1], axis=0))
    o_ref[0, 0:t, :] = (a0[:, :MLA_V] / a0[:, MLA_V:MLA_V + 1]).astype(o_ref.dtype)
    o_ref[0, t:2 * t, :] = (a1[:, :MLA_V] / a1[:, MLA_V:MLA_V + 1]).astype(o_ref.dtype)


def _flash_attention(q, k, v):
    nbatch, nh, seq, dqk = q.shape
    t = min(ATT_T, seq // 2)
    nq = seq // (2 * t)
    return pl.pallas_call(
        functools.partial(_flash_kernel, t=t),
        grid=(nbatch, nh, nq),
        in_specs=[pl.BlockSpec((1, 1, 2 * t, dqk), lambda b, h, i: (b, h, i, 0)),
                  pl.BlockSpec((1, 1, seq, dqk), lambda b, h, i: (b, h, 0, 0)),
                  pl.BlockSpec((1, 1, seq, 2 * MLA_V), lambda b, h, i: (b, h, 0, 0))],
        out_specs=pl.BlockSpec((1, 2 * t, MLA_V), lambda b, h, i: (b, i, h)),
        out_shape=jax.ShapeDtypeStruct((nbatch, seq, nh * MLA_V), BF16),
        scratch_shapes=[pltpu.VMEM((2, seq // t, t, t), F32)],
        compiler_params=_cparams(("arbitrary", "arbitrary", "arbitrary")),
        name="flash_attn",
    )(q, k, v)


def _gelu_tanh(x):
    c = math.sqrt(2.0 / math.pi)
    return 0.5 * x * (1.0 + jnp.tanh(c * (x + 0.044715 * (x * x * x))))


def _mixer_out_kernel(ys_ref, ym_ref, x_ref, gate_ref, wglu_ref, bglu_ref, wo_ref, o_ref):
    y = _gelu_tanh(ys_ref[...].astype(F32))
    g = _dot(y.astype(BF16), wglu_ref[...]) + bglu_ref[...]
    s5o = (y * jax.nn.sigmoid(g)).astype(BF16)
    acc = _dot(s5o, wo_ref[0:S5_WIDTH, :]) + _dot(ym_ref[...], wo_ref[S5_WIDTH:, :])
    o_ref[...] = x_ref[...] + gate_ref[0] * acc


def _mixer_out(ys, ym, x2, mod, w_glu, b_glu, w_out, seq):
    t = x2.shape[0]
    tm = min(OUT_TM, seq)
    tpb = seq // tm
    row = lambda w: pl.BlockSpec((tm, w), lambda i: (i, 0))
    full = lambda a, b: pl.BlockSpec((a, b), lambda i: (0, 0))
    return pl.pallas_call(
        _mixer_out_kernel,
        grid=(t // tm,),
        in_specs=[row(S5_WIDTH), row(MLA_HEADS * MLA_V), row(D_MODEL), _mod_specs(tpb, 2),
                  full(S5_WIDTH, S5_WIDTH), full(1, S5_WIDTH), full(2 * S5_WIDTH, D_MODEL)],
        out_specs=row(D_MODEL),
        out_shape=jax.ShapeDtypeStruct((t, D_MODEL), F32),
        compiler_params=_cparams(("arbitrary",)),
        name="mixer_out",
    )(ys, ym, x2, mod, w_glu, b_glu.reshape(1, -1), w_out)


def _route_kernel(x_ref, g_ref, shift_ref, scale_ref, rw_ref, rb_ref, tri_ref,
                  h_ref, route_ref, gates_ref, cnt_ref, carry_ref):
    i = pl.program_id(0)

    @pl.when(i == 0)
    def _():
        carry_ref[...] = jnp.zeros(carry_ref.shape, F32)

    h = _norm_mod(x_ref[...], g_ref[...], scale_ref[0], shift_ref[0])
    _pack_store(h_ref, h)
    logits = jnp.dot(h, rw_ref[...], precision=lax.Precision.HIGHEST,
                     preferred_element_type=F32) + rb_ref[...]
    lane = lax.broadcasted_iota(jnp.int32, logits.shape, 1)
    lanef = lane.astype(F32)
    lg = jnp.where(lane < N_EXPERTS, logits, -jnp.inf)
    vals, hots, idxs = [], [], []
    sel = jnp.zeros(logits.shape, F32)
    for _ in range(TOP_K):
        m = jnp.max(lg, axis=-1, keepdims=True)
        idx = jnp.min(jnp.where(lg == m, lanef, float(LANES)), axis=-1, keepdims=True)
        hot = lanef == idx
        vals.append(m)
        idxs.append(idx)
        hots.append(hot)
        sel = jnp.where(hot, 1.0, sel)
        lg = jnp.where(hot, -jnp.inf, lg)
    es = [jnp.exp(v - vals[0]) for v in vals]
    denom = es[0] + es[1] + es[2] + es[3]
    before = _dot(tri_ref[...], sel.astype(BF16)) + carry_ref[...]
    route = jnp.zeros(logits.shape, F32)
    gates = jnp.zeros(logits.shape, F32)
    for k in range(TOP_K):
        rank = jnp.sum(jnp.where(hots[k], before, 0.0), axis=-1, keepdims=True)
        route = jnp.where(lane == k, idxs[k], route)
        route = jnp.where(lane == TOP_K + k, rank, route)
        gates = jnp.where(lane == k, es[k] / denom, gates)
    route_ref[...] = route.astype(jnp.int32)
    gates_ref[...] = gates
    carry_ref[...] = carry_ref[...] + jnp.sum(sel, axis=0, keepdims=True)
    cnt_ref[...] = carry_ref[...]


def _route(x2, g, mod, router_w, router_b, seq):
    t = x2.shape[0]
    tm = min(ROUTE_TM, seq)
    tpb = seq // tm
    rw = jnp.zeros((D_MODEL, LANES), F32).at[:, :N_EXPERTS].set(router_w)
    rb = jnp.zeros((1, LANES), F32).at[0, :N_EXPERTS].set(router_b)
    tri = jnp.asarray(np.tril(np.ones((tm, tm), np.float32), -1), BF16)
    row = lambda w: pl.BlockSpec((tm, w), lambda i: (i, 0))
    full = lambda a, b: pl.BlockSpec((a, b), lambda i: (0, 0))
    return pl.pallas_call(
        _route_kernel,
        grid=(t // tm,),
        in_specs=[row(D_MODEL), full(1, D_MODEL), _mod_specs(tpb, 0), _mod_specs(tpb, 1),
                  full(D_MODEL, LANES), full(1, LANES), full(tm, tm)],
        out_specs=[pl.BlockSpec((tm * PACK, LANES), lambda i: (i, 0)), row(LANES), row(LANES), full(1, LANES)],
        out_shape=[jax.ShapeDtypeStruct((t * PACK, LANES), U32),
                   jax.ShapeDtypeStruct((t, LANES), jnp.int32),
                   jax.ShapeDtypeStruct((t, LANES), F32),
                   jax.ShapeDtypeStruct((1, LANES), F32)],
        scratch_shapes=[pltpu.VMEM((1, LANES), F32)],
        compiler_params=_cparams(("arbitrary",)),
        name="moe_route",
    )(x2, g.reshape(1, -1), mod, mod, rw, rb, tri)


def _dispatch_kernel(pstart_ref, npad_ref, used_ref, dest_ref, h_ref, xs_hbm, zero_ref, sem, zsem):
    i = pl.program_id(0)
    n = DISPATCH_T
    nslot = xs_hbm.shape[0] // PACK

    def pad_copy(slot):
        return pltpu.make_async_copy(zero_ref.at[_prow(0), :], xs_hbm.at[_prow(slot), :], zsem)

    def tail_copy(t):
        return pltpu.make_async_copy(zero_ref, xs_hbm.at[_prow(used_ref[0] + t * FFN_TM, FFN_TM), :], zsem)

    @pl.when(i == 0)
    def _():
        zero_ref[...] = jnp.zeros(zero_ref.shape, U32)
        ntail = (nslot - used_ref[0]) // FFN_TM
        for wait in (False, True):
            for e in range(N_EXPERTS):
                def pad(j, c):
                    cp = pad_copy(pstart_ref[e] + j)
                    cp.wait() if wait else cp.start()
                    return c
                lax.fori_loop(0, npad_ref[e], pad, 0)

            def tail(t, c):
                cp = tail_copy(t)
                cp.wait() if wait else cp.start()
                return c
            lax.fori_loop(0, ntail, tail, 0)

    def row_copy(r, slot):
        return pltpu.make_async_copy(h_ref.at[_prow(r), :], xs_hbm.at[_prow(slot), :], sem)

    for k in range(TOP_K):
        def body(j, c):
            for par in range(2):
                r = 2 * j + par
                row_copy(r, dest_ref[0, 0, k * n + r]).start(priority=par)
            return c
        lax.fori_loop(0, n // 2, body, 0, unroll=4)

    def wait_rows(r, c):
        row_copy(0, 0).wait()
        return c

    lax.fori_loop(0, n * TOP_K, wait_rows, 0, unroll=8)


def _dispatch_rows(h_slabs, dest, pad_start, npad, used, ns):
    t = dest.shape[0]
    tm = min(DISPATCH_T, t)
    nt = t // tm
    idx = dest.reshape(nt, tm, TOP_K).transpose(0, 2, 1).reshape(nt, 1, TOP_K * tm)
    grid_spec = pltpu.PrefetchScalarGridSpec(
        num_scalar_prefetch=3,
        grid=(nt,),
        in_specs=[pl.BlockSpec((1, 1, TOP_K * tm), lambda i, *_: (i, 0, 0), memory_space=pltpu.SMEM),
                  pl.BlockSpec((tm * PACK, LANES), lambda i, *_: (i, 0))],
        out_specs=pl.BlockSpec(memory_space=pl.ANY),
        scratch_shapes=[pltpu.VMEM((FFN_TM * PACK, LANES), U32),
                        pltpu.SemaphoreType.DMA(()), pltpu.SemaphoreType.DMA(())],
    )
    return pl.pallas_call(
        _dispatch_kernel,
        grid_spec=grid_spec,
        out_shape=jax.ShapeDtypeStruct((ns * PACK, LANES), U32),
        compiler_params=_cparams(("arbitrary",)),
        name="moe_dispatch",
    )(pad_start, npad, used, idx, h_slabs)


PERM_W = 256
W1_BLK = 1024


def _w1_prep_kernel(w_ref, p_ref, o_ref):
    for b in range(W1_BLK // PERM_W):
        cols = slice(b * PERM_W, (b + 1) * PERM_W)
        o_ref[0, :, cols] = _dot(w_ref[0, :, cols].astype(BF16), p_ref[...]).astype(BF16)


def _w1_prep(w1):
    e, d, n = w1.shape
    pm = np.zeros((PERM_W, PERM_W), np.float32)
    jj = np.arange(PERM_W // 2)
    pm[2 * jj, jj] = 1.0
    pm[2 * jj + 1, PERM_W // 2 + jj] = 1.0
    return pl.pallas_call(
        _w1_prep_kernel,
        grid=(e, n // W1_BLK),
        in_specs=[pl.BlockSpec((1, d, W1_BLK), lambda i, j: (i, 0, j)),
                  pl.BlockSpec((PERM_W, PERM_W), lambda i, j: (0, 0))],
        out_specs=pl.BlockSpec((1, d, W1_BLK), lambda i, j: (i, 0, j)),
        out_shape=jax.ShapeDtypeStruct((e, d, n), BF16),
        compiler_params=_cparams(("arbitrary", "arbitrary")),
        name="moe_w1_prep",
    )(w1, jnp.asarray(pm, BF16))


def _ffn_kernel(we_ref, row0_ref, nt_ref, nv_ref, xs_hbm, w1_ref, b1_ref, w2_ref, b2_ref,
                ys_hbm, x_ref, acc_ref, w2b_ref, stage_ref, sem_in, sem_out):
    w = pl.program_id(0)
    f = pl.program_id(1)
    nf = pl.num_programs(1)
    nt = nt_ref[w]
    row0 = row0_ref[w]
    tm = FFN_TM
    hw = PERM_W // 2

    @pl.when(jnp.logical_and(f == 0, nt > 0))
    def _():
        def in_copy(t, slot):
            return pltpu.make_async_copy(xs_hbm.at[_prow(row0 + t * tm, tm), :], stage_ref.at[slot], sem_in.at[slot])

        in_copy(0, 0).start()
        half = D_MODEL // 2

        def load(t, c):
            slot = t % 2

            @pl.when(t + 1 < nt)
            def _():
                in_copy(t + 1, 1 - slot).start()

            in_copy(t, slot).wait()
            rows = pl.ds(pl.multiple_of(t * tm, tm), tm)
            for s in range(PACK):
                lo, hi = _unpack_load(stage_ref, tm, s, lead=(slot,))
                x_ref[rows, s * LANES:(s + 1) * LANES] = lo.astype(BF16)
                x_ref[rows, half + s * LANES:half + (s + 1) * LANES] = hi.astype(BF16)
            return c

        lax.fori_loop(0, nt, load, 0)

    def out_copy(t, slot):
        return pltpu.make_async_copy(stage_ref.at[slot], ys_hbm.at[_prow(row0 + t * tm, tm), :], sem_out.at[slot])

    @pl.when(nt > 0)
    def _():
        w2b_ref[...] = w2_ref[0].astype(BF16)

        @pl.when(f == 0)
        def _():
            def init(t, c):
                acc_ref[pl.ds(pl.multiple_of(t * tm, tm), tm), :] = jnp.broadcast_to(b2_ref[0], (tm, D_MODEL))
                return c
            lax.fori_loop(0, nt, init, 0)

        def tile(xv, av):
            a = _dot(xv[...], w1_ref[0]) + b1_ref[0]
            nblk = a.shape[1] // PERM_W
            glu = jnp.concatenate([a[:, b * PERM_W:b * PERM_W + hw] for b in range(nblk)], axis=1)
            lin = jnp.concatenate([a[:, b * PERM_W + hw:(b + 1) * PERM_W] for b in range(nblk)], axis=1)
            glu = jnp.minimum(glu, SWIGLU_LIMIT)
            lin = jnp.clip(lin, -SWIGLU_LIMIT, SWIGLU_LIMIT)
            act = glu * jax.nn.sigmoid(SWIGLU_ALPHA * glu) * (lin + 1.0)
            av[...] = av[...] + _dot(act.astype(BF16), w2b_ref[...])

        def pair(p, c):
            win = pl.ds(pl.multiple_of(p * 2 * tm, 2 * tm), 2 * tm)
            xw, aw = x_ref.at[win, :], acc_ref.at[win, :]
            for h in range(2):
                tile(xw.at[h * tm:(h + 1) * tm, :], aw.at[h * tm:(h + 1) * tm, :])
            return c

        lax.fori_loop(0, nt // 2, pair, 0)

        @pl.when(nt % 2 == 1)
        def _():
            last = pl.ds(pl.multiple_of((nt - 1) * tm, tm), tm)
            tile(x_ref.at[last, :], acc_ref.at[last, :])

        @pl.when(f == nf - 1)
        def _():
            def emit(t, c):
                slot = t % 2

                @pl.when(t >= 2)
                def _():
                    out_copy(t - 2, slot).wait()

                _pack_store(stage_ref, acc_ref[pl.ds(pl.multiple_of(t * tm, tm), tm), :], lead=(slot,))
                out_copy(t, slot).start()
                return c

            lax.fori_loop(0, nt, emit, 0)
            for back in (2, 1):
                @pl.when(nt >= back)
                def _():
                    out_copy(nt - back, (nt - back) % 2).wait()

    @pl.when(jnp.logical_and(w == pl.num_programs(0) - 1, f == nf - 1))
    def _():
        used = nv_ref[1]
        ntail = (ys_hbm.shape[0] // PACK - used) // tm
        stage_ref[0] = jnp.zeros(stage_ref.shape[1:], U32)

        def fill(t, c):
            cp = pltpu.make_async_copy(stage_ref.at[0], ys_hbm.at[_prow(used + t * tm, tm), :], sem_out.at[0])
            cp.start()
            cp.wait()
            return c

        lax.fori_loop(0, ntail, fill, 0)


def _ffn(xs, w1p, b1p, w2b, b2, we, row0, ntile, nvalid):
    d = D_MODEL
    nf = EXPERT_FF // FFN_FC
    wmax = we.shape[0]

    def wmap(axis):
        def index(w, f, we_r, row0_r, nt_r, nv_r):
            fe = jnp.where(w < nv_r[0], f, nf - 1)
            return (we_r[w], 0, fe) if axis == 2 else (we_r[w], fe, 0)
        return index

    grid_spec = pltpu.PrefetchScalarGridSpec(
        num_scalar_prefetch=4,
        grid=(wmax, nf),
        in_specs=[pl.BlockSpec(memory_space=pl.ANY),
                  pl.BlockSpec((1, d, 2 * FFN_FC), wmap(2)),
                  pl.BlockSpec((1, 1, 2 * FFN_FC), wmap(2)),
                  pl.BlockSpec((1, FFN_FC, d), wmap(1)),
                  pl.BlockSpec((1, 1, d), lambda w, f, we_r, *_: (we_r[w], 0, 0))],
        out_specs=pl.BlockSpec(memory_space=pl.ANY),
        scratch_shapes=[pltpu.VMEM((FFN_R, d), BF16),
                        pltpu.VMEM((FFN_R, d), F32),
                        pltpu.VMEM((FFN_FC, d), BF16),
                        pltpu.VMEM((2, FFN_TM * PACK, LANES), U32),
                        pltpu.SemaphoreType.DMA((2,)),
                        pltpu.SemaphoreType.DMA((2,))],
    )
    return pl.pallas_call(
        _ffn_kernel,
        grid_spec=grid_spec,
        out_shape=jax.ShapeDtypeStruct(xs.shape, U32),
        compiler_params=_cparams(("arbitrary", "arbitrary")),
        name="moe_ffn",
    )(we, row0, ntile, nvalid, xs, w1p, b1p, w2b, b2)


def _combine_kernel(cur_ref, nxt_ref, ys_hbm, gates_ref, x_ref, gmod_ref, fg_ref, o_ref, buf_ref, sem, *, final):
    i = pl.program_id(0)
    n = x_ref.shape[0]
    slot = i % 2

    def row_copy(idx_ref, b, k, r):
        src = ys_hbm.at[_prow(idx_ref[0, 0, k * n + r]), :]
        return pltpu.make_async_copy(src, buf_ref.at[b, k, _prow(r), :], sem.at[b])

    def issue(idx_ref, b):
        for k in range(TOP_K):
            def body(j, c):
                for par in range(2):
                    row_copy(idx_ref, b, k, 2 * j + par).start(priority=par)
                return c
            lax.fori_loop(0, n // 2, body, 0, unroll=4)

    @pl.when(i == 0)
    def _():
        issue(cur_ref, 0)

    @pl.when(i + 1 < pl.num_programs(0))
    def _():
        issue(nxt_ref, 1 - slot)

    for k in range(TOP_K):
        def wait(r, c):
            row_copy(cur_ref, slot, k, r).wait()
            return c
        lax.fori_loop(0, n, wait, 0, unroll=8)

    gates = gates_ref[...]
    gk = [gates[:, k:k + 1] for k in range(TOP_K)]
    half = D_MODEL // 2
    for s in range(PACK):
        lo = hi = None
        for k in range(TOP_K):
            lo_k, hi_k = _unpack_load(buf_ref, n, s, lead=(slot, k))
            lo = gk[k] * lo_k if lo is None else lo + gk[k] * lo_k
            hi = gk[k] * hi_k if hi is None else hi + gk[k] * hi_k
        for base, acc in ((s * LANES, lo), (half + s * LANES, hi)):
            cols = slice(base, base + LANES)
            o_ref[:, cols] = x_ref[:, cols] + gmod_ref[0, :, cols] * acc
    if final:
        o_ref[...] = _rms(o_ref[...], fg_ref[...])


def _combine(ys, dest, gates, x2, mod, final_g, seq, final):
    t, d = x2.shape
    tm = min(COMBINE_T, seq)
    tpb = seq // tm
    nt = t // tm
    idx = dest.reshape(nt, tm, TOP_K).transpose(0, 2, 1).reshape(nt, 1, TOP_K * tm)
    row = lambda w: pl.BlockSpec((tm, w), lambda i: (i, 0))
    return pl.pallas_call(
        functools.partial(_combine_kernel, final=final),
        grid=(nt,),
        in_specs=[pl.BlockSpec((1, 1, tm * TOP_K), lambda i: (i, 0, 0), memory_space=pltpu.SMEM),
                  pl.BlockSpec((1, 1, tm * TOP_K), lambda i: (jnp.minimum(i + 1, nt - 1), 0, 0),
                               memory_space=pltpu.SMEM),
                  pl.BlockSpec(memory_space=pl.ANY),
                  row(LANES), row(d), _mod_specs(tpb, 2),
                  pl.BlockSpec((1, d), lambda i: (0, 0))],
        out_specs=row(d),
        out_shape=jax.ShapeDtypeStruct((t, d), F32),
        scratch_shapes=[pltpu.VMEM((2, TOP_K, tm * PACK, LANES), U32), pltpu.SemaphoreType.DMA((2,))],
        compiler_params=_cparams(("arbitrary",)),
        name="moe_combine",
    )(idx, idx, ys, gates, x2, mod, final_g.reshape(1, -1))


def _moe_sublayer(x2, c_mod, norm_g, router_w, router_b, w1, b1, w2, b2, final_g, seq, final):
    t = x2.shape[0]
    h, route, gates, cnt = _route(x2, norm_g, c_mod, router_w, router_b, seq)
    top_e = route[:, :TOP_K]
    rank = route[:, TOP_K:2 * TOP_K]
    counts = cnt[0, :N_EXPERTS].astype(jnp.int32)
    padded = ((counts + FFN_TM - 1) // FFN_TM) * FFN_TM
    starts = jnp.cumsum(padded) - padded
    dest = starts[top_e] + rank
    ns = t * TOP_K + N_EXPERTS * FFN_TM
    ns = ((ns + GATHER_T - 1) // GATHER_T) * GATHER_T
    used = jnp.sum(padded).astype(jnp.int32)
    n_items = (padded + FFN_R - 1) // FFN_R
    item_end = jnp.cumsum(n_items)
    wmax = (t * TOP_K) // FFN_R + N_EXPERTS
    wid = jnp.arange(wmax, dtype=jnp.int32)
    nvalid = item_end[-1].astype(jnp.int32)
    we = jnp.minimum(jnp.searchsorted(item_end, wid, side='right'), N_EXPERTS - 1).astype(jnp.int32)
    last_e = we[jnp.maximum(nvalid - 1, 0)]
    valid = wid < nvalid
    we = jnp.where(valid, we, last_e)
    local = wid - (item_end - n_items)[we]
    row0 = jnp.where(valid, starts[we] + local * FFN_R, 0).astype(jnp.int32)
    ntile = jnp.where(valid, jnp.minimum(FFN_R, padded[we] - local * FFN_R) // FFN_TM, 0).astype(jnp.int32)

    xs = _dispatch_rows(h, dest.astype(jnp.int32), (starts + counts).astype(jnp.int32),
                        (padded - counts).astype(jnp.int32), used.reshape(1), ns)
    hw = PERM_W // 2
    b1p = b1.reshape(N_EXPERTS, 2 * EXPERT_FF // PERM_W, hw, 2).transpose(0, 1, 3, 2).reshape(N_EXPERTS, 1, 2 * EXPERT_FF)
    ys = _ffn(xs, _w1_prep(w1), b1p, w2, b2.reshape(N_EXPERTS, 1, D_MODEL),
              we, row0, ntile, jnp.stack([nvalid, used]))
    return _combine(ys, dest.astype(jnp.int32), gates, x2, c_mod, final_g, seq, final)


CONV_TN = 512


def _conv_pw1_kernel(x_ref, g_ref, shift_ref, scale_ref, w_ref, b_ref, o_ref):
    h = _norm_mod(x_ref[...], g_ref[...], scale_ref[0], shift_ref[0]).astype(BF16)
    for c0 in range(0, D_MODEL, CONV_TN):
        ca = slice(c0, c0 + CONV_TN)
        cb = slice(D_MODEL + c0, D_MODEL + c0 + CONV_TN)
        a = _dot(h, w_ref[:, ca]) + b_ref[:, ca]
        b = _dot(h, w_ref[:, cb]) + b_ref[:, cb]
        o_ref[:, ca] = a * jax.nn.sigmoid(b)


def _conv_pw1(x2, g, mod, w_pw1, b_pw1, seq):
    t = x2.shape[0]
    tm = min(PROJ_TM, seq)
    tpb = seq // tm
    once = pl.Buffered(1)
    return pl.pallas_call(
        _conv_pw1_kernel,
        grid=(t // tm,),
        in_specs=[pl.BlockSpec((tm, D_MODEL), lambda i: (i, 0)),
                  pl.BlockSpec((1, D_MODEL), lambda i: (0, 0)),
                  _mod_specs(tpb, 0), _mod_specs(tpb, 1),
                  pl.BlockSpec((D_MODEL, 2 * D_MODEL), lambda i: (0, 0), pipeline_mode=once),
                  pl.BlockSpec((1, 2 * D_MODEL), lambda i: (0, 0))],
        out_specs=pl.BlockSpec((tm, D_MODEL), lambda i: (i, 0)),
        out_shape=jax.ShapeDtypeStruct((t, D_MODEL), F32),
        compiler_params=_cparams(("arbitrary",)),
        name="conv_pw1",
    )(x2, g.reshape(1, -1), mod, mod, w_pw1, b_pw1.reshape(1, -1))


def _conv_dw_kernel(y_ref, halo_ref, wdw_ref, bdw_ref, lng_ref, lnb_ref, wp_ref, bp_ref, x_ref, gate_ref,
                    o_ref, buf_ref, z_ref, win_ref, *, tpb):
    i = pl.program_id(0)
    tm = y_ref.shape[0]
    first = (i % tpb) == 0
    halo = halo_ref[...]
    buf_ref[0:CONV_HALO, :] = jnp.where(first, 0.0, halo)
    buf_ref[CONV_HALO:, :] = y_ref[...]
    cw = CONV_CW
    rb = min(CONV_RB, tm)
    off = CONV_HALO - (CONV_KERNEL - 1)
    for c0 in range(0, D_MODEL, cw):
        for r0 in range(0, tm, rb):
            acc = jnp.zeros((rb, cw), F32)
            for j in range(SUBLANES):
                taps = [k for k in range(CONV_KERNEL) if (off + k) % SUBLANES == j]
                span = max(off + k for k in taps) - j + rb
                win_ref[0:span, :] = buf_ref[r0 + j:r0 + j + span, c0:c0 + cw]
                for k in taps:
                    q = off + k - j
                    acc = acc + wdw_ref[k:k + 1, c0:c0 + cw] * win_ref[q:q + rb, :]
            z_ref[r0:r0 + rb, c0:c0 + cw] = acc
    z = z_ref[...] + bdw_ref[...]
    mu = jnp.mean(z, axis=-1, keepdims=True)
    zc = z - mu
    var = jnp.mean(zc * zc, axis=-1, keepdims=True)
    zn = zc * lax.rsqrt(var + LN_EPS) * lng_ref[...] + lnb_ref[...]
    act = (zn * jax.nn.sigmoid(zn)).astype(BF16)
    o_ref[...] = x_ref[...] + gate_ref[0] * (_dot(act, wp_ref[...]) + bp_ref[...])


def _conv_dw(y1, x2, mod, w_dw, b_dw, ln_g, ln_b, w_pw2, b_pw2, seq):
    t = x2.shape[0]
    tm = min(CONV_TM, seq)
    tpb = seq // tm
    hb = tm // CONV_HALO
    wdw = jnp.zeros((CONV_HALO, D_MODEL), F32).at[:CONV_KERNEL].set(w_dw)
    row = lambda: pl.BlockSpec((tm, D_MODEL), lambda i: (i, 0))
    vec = lambda: pl.BlockSpec((1, D_MODEL), lambda i: (0, 0))
    return pl.pallas_call(
        functools.partial(_conv_dw_kernel, tpb=tpb),
        grid=(t // tm,),
        in_specs=[row(),
                  pl.BlockSpec((CONV_HALO, D_MODEL), lambda i: (jnp.maximum(i * hb - 1, 0), 0)),
                  pl.BlockSpec((CONV_HALO, D_MODEL), lambda i: (0, 0)),
                  vec(), vec(), vec(),
                  pl.BlockSpec((D_MODEL, D_MODEL), lambda i: (0, 0)),
                  vec(), row(), _mod_specs(tpb, 2)],
        out_specs=row(),
        out_shape=jax.ShapeDtypeStruct((t, D_MODEL), F32),
        scratch_shapes=[pltpu.VMEM((tm + CONV_HALO, D_MODEL), F32), pltpu.VMEM((tm, D_MODEL), F32),
                        pltpu.VMEM((CONV_RB + CONV_HALO, CONV_CW), F32)],
        compiler_params=_cparams(("arbitrary",)),
        name="conv_dw_pw2",
    )(y1, y1, wdw, b_dw.reshape(1, -1), ln_g.reshape(1, -1), ln_b.reshape(1, -1),
      w_pw2, b_pw2.reshape(1, -1), x2, mod)


def _rope_swap_cols(w, heads):
    k = w.shape[0]
    half = MLA_ROPE // 2
    w3 = w.reshape(k, heads, MLA_NOPE + MLA_ROPE)
    pe = w3[:, :, MLA_NOPE:]
    sw = jnp.concatenate([pe[:, :, half:], pe[:, :, :half]], axis=-1)
    return jnp.concatenate([w3, sw], axis=-1).reshape(k, heads * (MLA_NOPE + 2 * MLA_ROPE))


def kernel(x, c, positions, l0_mix_norm_g, l0_mix_ada_w, l0_mix_ada_b, l0_w_in, l0_s5_lambda_re, l0_s5_lambda_im, l0_s5_log_step, l0_s5_b_re, l0_s5_b_im, l0_s5_c_re, l0_s5_c_im, l0_s5_d, l0_s5_w_glu, l0_s5_b_glu, l0_mla_q_norm_g, l0_mla_w_uq, l0_mla_kv_norm_g, l0_mla_w_ukv, l0_w_out, l0_moe_norm_g, l0_moe_ada_w, l0_moe_ada_b, l0_router_w, l0_router_b, l0_exp_w1, l0_exp_b1, l0_exp_w2, l0_exp_b2, l1_mix_norm_g, l1_mix_ada_w, l1_mix_ada_b, l1_conv_w_pw1, l1_conv_b_pw1, l1_conv_w_dw, l1_conv_b_dw, l1_conv_ln_g, l1_conv_ln_b, l1_conv_w_pw2, l1_conv_b_pw2, l1_moe_norm_g, l1_moe_ada_w, l1_moe_ada_b, l1_router_w, l1_router_b, l1_exp_w1, l1_exp_b1, l1_exp_w2, l1_exp_b2, final_norm_g):
    nbatch, seq, d = x.shape
    t = nbatch * seq
    x2 = x.reshape(t, d)
    pos = positions.reshape(t, 1).astype(jnp.int32)

    mod = _ada_mod(c, l0_mix_ada_w, l0_mix_ada_b)
    half = MLA_ROPE // 2
    kcol = S5_WIDTH + MLA_Q_RANK + MLA_KV_RANK
    w_in_ext = jnp.concatenate([l0_w_in, l0_w_in[:, kcol + half:], l0_w_in[:, kcol:kcol + half]], axis=1).astype(BF16)
    u, cq, ckv, kr = _proj_in(x2, l0_mix_norm_g, mod, w_in_ext, seq)

    kt, wt, vv, a16 = _s5_prep(l0_s5_lambda_re, l0_s5_lambda_im, l0_s5_log_step, l0_s5_b_re, l0_s5_b_im,
                               l0_s5_c_re, l0_s5_c_im, l0_s5_d)
    nchunk = seq // S5_CHUNK
    u_r = (u.reshape(nbatch, nchunk, S5_CHUNK, S5_GROUPS, S5_GROUP_CH)
           .transpose(3, 0, 1, 2, 4).reshape(S5_GROUPS, nbatch * nchunk, S5_ROW))
    y_r = _s5_main(u_r, kt, wt, vv, a16, nbatch, nchunk)
    ys = (y_r.reshape(S5_GROUPS, nbatch, nchunk, S5_CHUNK, S5_GROUP_CH)
          .transpose(1, 2, 3, 0, 4).reshape(t, S5_WIDTH))

    wq_ext = _rope_swap_cols(l0_mla_w_uq, MLA_HEADS).astype(BF16)
    q, k, v = _mla_proj(cq, ckv, kr, pos, l0_mla_q_norm_g, l0_mla_kv_norm_g, wq_ext,
                        l0_mla_w_ukv.astype(BF16), nbatch, seq)
    ym = _flash_attention(q, k, v).reshape(t, MLA_HEADS * MLA_V)
    x2 = _mixer_out(ys, ym, x2, mod, l0_s5_w_glu.astype(BF16), l0_s5_b_glu, l0_w_out.astype(BF16), seq)

    mod = _ada_mod(c, l0_moe_ada_w, l0_moe_ada_b)
    x2 = _moe_sublayer(x2, mod, l0_moe_norm_g, l0_router_w, l0_router_b, l0_exp_w1, l0_exp_b1,
                       l0_exp_w2, l0_exp_b2, final_norm_g, seq, final=False)

    mod = _ada_mod(c, l1_mix_ada_w, l1_mix_ada_b)
    y1 = _conv_pw1(x2, l1_mix_norm_g, mod, l1_conv_w_pw1.astype(BF16), l1_conv_b_pw1, seq)
    x2 = _conv_dw(y1, x2, mod, l1_conv_w_dw, l1_conv_b_dw, l1_conv_ln_g, l1_conv_ln_b,
                  l1_conv_w_pw2.astype(BF16), l1_conv_b_pw2, seq)

    mod = _ada_mod(c, l1_moe_ada_w, l1_moe_ada_b)
    x2 = _moe_sublayer(x2, mod, l1_moe_norm_g, l1_router_w, l1_router_b, l1_exp_w1, l1_exp_b1,
                       l1_exp_w2, l1_exp_b2, final_norm_g, seq, final=True)
    return x2.reshape(nbatch, seq, d)
```

```python
import functools
import math

import numpy as np
import jax
import jax.numpy as jnp
from jax import lax
from jax.experimental import pallas as pl
from jax.experimental.pallas import tpu as pltpu

F32 = jnp.float32
BF16 = jnp.bfloat16

D_MODEL = 2048
S5_WIDTH = 1024
S5_GROUP_CH = 16
S5_GROUPS = 64
S5_STATE = 64
MLA_HEADS = 8
MLA_NOPE = 128
MLA_ROPE = 64
MLA_V = 128
MLA_Q_RANK = 512
MLA_KV_RANK = 512
ROPE_THETA = 10000.0
CONV_KERNEL = 31
N_EXPERTS = 32
TOP_K = 4
EXPERT_FF = 2048
SWIGLU_ALPHA = 1.702
SWIGLU_LIMIT = 7.0
EPS = 1e-6
LN_EPS = 1e-5
NEG_BIG = -1e30

LANES = 128
SUBLANES = 8
VMEM_LIMIT_BYTES = 56 * 1024 * 1024

S5_CHUNK = 16
S5_ROW = S5_CHUNK * S5_GROUP_CH
PROJ_TM = 512
MLA_TM = 256
ATT_T = 512
OUT_TM = 256
ROUTE_TM = 512
CONV_TM = 256
CONV_HALO = 32
CONV_RB = 128
CONV_CW = 256
FFN_TM = 256
FFN_R = 2304
FFN_FC = 512
GATHER_T = 512
DISPATCH_T = 512
COMBINE_T = 256


def _cparams(sem):
    return pltpu.CompilerParams(dimension_semantics=sem, vmem_limit_bytes=VMEM_LIMIT_BYTES)


def _norm_mod(x, g, scale, shift):
    ms = jnp.mean(x * x, axis=-1, keepdims=True)
    return (x * lax.rsqrt(ms + EPS)) * g * (1.0 + scale) + shift


def _rms(x, g):
    ms = jnp.mean(x * x, axis=-1, keepdims=True)
    return (x * lax.rsqrt(ms + EPS)) * g


def _dot(a, b):
    return jnp.dot(a, b, preferred_element_type=F32)


def _dot_nt(a, b):
    return lax.dot_general(a, b, (((1,), (1,)), ((), ())), preferred_element_type=F32)


U32 = jnp.uint32
PACK = D_MODEL // (2 * LANES)
HI_MASK = 0xFFFF0000


def _pack_store(ref, val, lead=()):
    n = val.shape[0]
    half = D_MODEL // 2
    for s in range(PACK):
        lo = val[:, s * LANES:(s + 1) * LANES].astype(BF16).astype(F32)
        hi = val[:, half + s * LANES:half + (s + 1) * LANES].astype(BF16).astype(F32)
        word = (lax.bitcast_convert_type(lo, U32) >> 16) | (lax.bitcast_convert_type(hi, U32) & jnp.uint32(HI_MASK))
        ref[lead + (pl.ds(s, n, stride=PACK), slice(None))] = word


def _unpack_load(ref, n, s, lead=()):
    word = ref[lead + (pl.ds(s, n, stride=PACK), slice(None))]
    lo = lax.bitcast_convert_type(word << 16, F32)
    hi = lax.bitcast_convert_type(word & jnp.uint32(HI_MASK), F32)
    return lo, hi


def _prow(r, n=1):
    return pl.ds(pl.multiple_of(r * PACK, PACK), n * PACK)


ADA_TN = 768
ADA_KC = 256


def _ada_kernel(ct_ref, w_ref, b_ref, o_ref):
    nb = ct_ref.shape[0]
    kdim = w_ref.shape[0]
    rows = []
    for b in range(nb):
        acc = jnp.zeros((1, w_ref.shape[1]), F32)
        for k0 in range(0, kdim, ADA_KC):
            c = ct_ref[b, k0:k0 + ADA_KC, :]
            cs = c * jax.nn.sigmoid(c)
            acc = acc + jnp.sum(w_ref[k0:k0 + ADA_KC, :] * cs, axis=0, keepdims=True)
        rows.append(acc)
    o_ref[...] = jnp.concatenate(rows, axis=0) + b_ref[...]


def _ada_mod(c, w, b):
    nb, d = c.shape
    n = w.shape[1]
    ct = c.reshape(nb, d, 1)
    m = pl.pallas_call(
        _ada_kernel,
        grid=(n // ADA_TN,),
        in_specs=[pl.BlockSpec((nb, d, 1), lambda j: (0, 0, 0)),
                  pl.BlockSpec((d, ADA_TN), lambda j: (0, j)),
                  pl.BlockSpec((1, ADA_TN), lambda j: (0, j))],
        out_specs=pl.BlockSpec((nb, ADA_TN), lambda j: (0, j)),
        out_shape=jax.ShapeDtypeStruct((nb, n), F32),
        compiler_params=_cparams(("arbitrary",)),
        name="ada_mod",
    )(ct, w, b.reshape(1, n))
    return m.reshape(nb, 1, n)


def _mod_specs(tiles_per_batch, which):
    return pl.BlockSpec((1, 1, D_MODEL), lambda i, *_: (i // tiles_per_batch, 0, which))


def _proj_in_kernel(x_ref, g_ref, shift_ref, scale_ref, w_ref, u_ref, cq_ref, ckv_ref, kr_ref):
    h = _norm_mod(x_ref[...], g_ref[...], scale_ref[0], shift_ref[0]).astype(BF16)
    acc = _dot(h, w_ref[...])
    c0, c1, c2 = S5_WIDTH, S5_WIDTH + MLA_Q_RANK, S5_WIDTH + MLA_Q_RANK + MLA_KV_RANK
    u_ref[...] = acc[:, :c0].astype(BF16)
    cq_ref[...] = acc[:, c0:c1]
    ckv_ref[...] = acc[:, c1:c2]
    kr_ref[...] = acc[:, c2:]


def _proj_in(x2, g, mod, w_ext, seq):
    t = x2.shape[0]
    tm = min(PROJ_TM, seq)
    tpb = seq // tm
    n = w_ext.shape[1]
    row = lambda w: pl.BlockSpec((tm, w), lambda i: (i, 0))
    return pl.pallas_call(
        _proj_in_kernel,
        grid=(t // tm,),
        in_specs=[row(D_MODEL),
                  pl.BlockSpec((1, D_MODEL), lambda i: (0, 0)),
                  _mod_specs(tpb, 0), _mod_specs(tpb, 1),
                  pl.BlockSpec((D_MODEL, n), lambda i: (0, 0))],
        out_specs=[row(S5_WIDTH), row(MLA_Q_RANK), row(MLA_KV_RANK), row(2 * MLA_ROPE)],
        out_shape=[jax.ShapeDtypeStruct((t, S5_WIDTH), BF16),
                   jax.ShapeDtypeStruct((t, MLA_Q_RANK), F32),
                   jax.ShapeDtypeStruct((t, MLA_KV_RANK), F32),
                   jax.ShapeDtypeStruct((t, 2 * MLA_ROPE), F32)],
        compiler_params=_cparams(("arbitrary",)),
        name="proj_in",
    )(x2, g.reshape(1, -1), mod, mod, w_ext)


def _s5_prep_kernel(lamc_ref, lamr_ref, step_ref, bt_ref, btt_ref, ct_ref, d_ref,
                    kt_ref, wt_ref, v_ref, a16_ref):
    P, H, C = S5_STATE, S5_GROUP_CH, S5_CHUNK
    step = step_ref[0]
    step = jnp.exp(step)
    lr_c = jnp.minimum(lamc_ref[0, 0], -1e-4)
    li_c = lamc_ref[0, 1]
    lr_r = jnp.minimum(lamr_ref[0, 0:1, :], -1e-4)
    li_r = lamr_ref[0, 1:2, :]

    def ratio(lr, li):
        mag = jnp.exp(lr * step)
        ab_re = mag * jnp.cos(li * step)
        ab_im = mag * jnp.sin(li * step)
        denom = lr * lr + li * li
        nr, ni = ab_re - 1.0, ab_im
        return (nr * lr + ni * li) / denom, (ni * lr - nr * li) / denom

    rr_c, ri_c = ratio(lr_c, li_c)
    rr_r, ri_r = ratio(lr_r, li_r)

    lane = lax.broadcasted_iota(jnp.int32, (1, S5_ROW), 1)
    kk = (lane // H).astype(F32)

    def powers(k):
        mag = jnp.exp(lr_c * step * k)
        return mag * jnp.cos(li_c * step * k), mag * jnp.sin(li_c * step * k)

    bre_t, bim_t = bt_ref[0, 0], bt_ref[0, 1]
    bbt_re = rr_c * bre_t - ri_c * bim_t
    bbt_im = rr_c * bim_t + ri_c * bre_t
    cre_t, cim_t = ct_ref[0, 0], ct_ref[0, 1]

    er, ei = powers(float(C - 1) - kk)
    wt_ref[0, 0:P, :] = (er * bbt_re - ei * bbt_im).astype(BF16)
    wt_ref[0, P:2 * P, :] = (er * bbt_im + ei * bbt_re).astype(BF16)

    er, ei = powers(kk + 1.0)
    v_ref[0, 0:P, :] = (cre_t * er - cim_t * ei).astype(BF16)
    v_ref[0, P:2 * P, :] = (-cre_t * ei - cim_t * er).astype(BF16)

    er, ei = powers(kk)
    q_re = er * cre_t - ei * cim_t
    q_im = er * cim_t + ei * cre_t
    brt, bit = btt_ref[0, 0], btt_ref[0, 1]
    bbr = rr_r * brt - ri_r * bit
    bbi = rr_r * bit + ri_r * brt
    hi = lax.Precision.HIGHEST
    mall = (jnp.dot(bbr, q_re, precision=hi, preferred_element_type=F32)
            - jnp.dot(bbi, q_im, precision=hi, preferred_element_type=F32))
    rowh = lax.broadcasted_iota(jnp.int32, (H, S5_ROW), 0)
    laneh = lax.broadcasted_iota(jnp.int32, (H, S5_ROW), 1)
    mall = mall + jnp.where(laneh == rowh, d_ref[0], 0.0)
    for s in range(C):
        piece = mall if s == 0 else pltpu.roll(mall, H * s, 1)
        piece = jnp.where(laneh >= H * s, piece, 0.0)
        kt_ref[0, s * H:(s + 1) * H, :] = piece.astype(BF16)

    mag = jnp.exp(lr_r * step * float(C))
    a16_ref[0, 0:1, :] = mag * jnp.cos(li_r * step * float(C))
    a16_ref[0, 1:2, :] = mag * jnp.sin(li_r * step * float(C))


def _s5_prep(lam_re, lam_im, log_step, b_re, b_im, c_re, c_im, d):
    G, P, H, C = S5_GROUPS, S5_STATE, S5_GROUP_CH, S5_CHUNK
    lam = jnp.stack([lam_re, lam_im], axis=1).astype(F32)
    lamc = lam.reshape(G, 2, P, 1)
    b = jnp.stack([b_re, b_im], axis=1).astype(F32)
    bt = jnp.tile(b, (1, 1, 1, C))
    btt = jnp.swapaxes(b, 2, 3)
    c = jnp.stack([c_re, c_im], axis=1).astype(F32)
    ct = jnp.tile(jnp.swapaxes(c, 2, 3), (1, 1, 1, C))
    dt = jnp.tile(d.astype(F32), (1, C)).reshape(G, 1, S5_ROW)
    step = log_step.astype(F32).reshape(G, 1, 1)
    g4 = lambda *shape: pl.BlockSpec((1,) + shape, lambda g: (g,) + (0,) * len(shape))
    return pl.pallas_call(
        _s5_prep_kernel,
        grid=(G,),
        in_specs=[g4(2, P, 1), g4(2, P), g4(1, 1), g4(2, P, S5_ROW), g4(2, H, P),
                  g4(2, P, S5_ROW), g4(1, S5_ROW)],
        out_specs=[g4(S5_ROW, S5_ROW), g4(2 * P, S5_ROW), g4(2 * P, S5_ROW), g4(2, P)],
        out_shape=[jax.ShapeDtypeStruct((G, S5_ROW, S5_ROW), BF16),
                   jax.ShapeDtypeStruct((G, 2 * P, S5_ROW), BF16),
                   jax.ShapeDtypeStruct((G, 2 * P, S5_ROW), BF16),
                   jax.ShapeDtypeStruct((G, 2, P), F32)],
        compiler_params=_cparams(("arbitrary",)),
        name="s5_prep",
    )(lamc, lam, step, bt, btt, ct, dt)


def _s5_main_kernel(u_ref, kt_ref, wt_ref, v_ref, a16_ref, y_ref, sr_ref, si_ref, xr_ref, xi_ref,
                    *, nbatch, nchunk):
    P = S5_STATE
    u = u_ref[0]
    sr_ref[...] = _dot_nt(u, wt_ref[0, 0:P, :])
    si_ref[...] = _dot_nt(u, wt_ref[0, P:2 * P, :])
    ar = a16_ref[0, 0:1, :]
    ai = a16_ref[0, 1:2, :]

    def step(c, carry):
        out = []
        for b in range(nbatch):
            xr, xi = carry[2 * b], carry[2 * b + 1]
            r = b * nchunk + c
            xr_ref[pl.ds(r, 1), :] = xr
            xi_ref[pl.ds(r, 1), :] = xi
            nxr = ar * xr - ai * xi + sr_ref[pl.ds(r, 1), :]
            nxi = ar * xi + ai * xr + si_ref[pl.ds(r, 1), :]
            out += [nxr, nxi]
        return tuple(out)

    zero = jnp.zeros((1, P), F32)
    lax.fori_loop(0, nchunk, step, (zero,) * (2 * nbatch))
    y = _dot(u, kt_ref[0])
    y = y + _dot(xr_ref[...].astype(BF16), v_ref[0, 0:P, :])
    y = y + _dot(xi_ref[...].astype(BF16), v_ref[0, P:2 * P, :])
    y_ref[0] = y.astype(y_ref.dtype)


def _s5_main(u_r, kt, wt, v, a16, nbatch, nchunk):
    G, rows, _ = u_r.shape
    P = S5_STATE
    g3 = lambda a, b: pl.BlockSpec((1, a, b), lambda g: (g, 0, 0))
    return pl.pallas_call(
        functools.partial(_s5_main_kernel, nbatch=nbatch, nchunk=nchunk),
        grid=(G,),
        in_specs=[g3(rows, S5_ROW), g3(S5_ROW, S5_ROW), g3(2 * P, S5_ROW), g3(2 * P, S5_ROW), g3(2, P)],
        out_specs=g3(rows, S5_ROW),
        out_shape=jax.ShapeDtypeStruct((G, rows, S5_ROW), BF16),
        scratch_shapes=[pltpu.VMEM((rows, P), F32) for _ in range(4)],
        compiler_params=_cparams(("arbitrary",)),
        name="s5_main",
    )(u_r, kt, wt, v, a16)


def _mla_proj_kernel(cq_ref, ckv_ref, kr_ref, pos_ref, qg_ref, kvg_ref, wq_ref, wkv_ref,
                     invf_ref, sgn_ref, q_ref, k_ref, v_ref):
    qscale = 1.0 / math.sqrt(MLA_NOPE + MLA_ROPE)
    qa = _dot(_rms(cq_ref[...], qg_ref[...]).astype(BF16), wq_ref[...])
    kva = _dot(_rms(ckv_ref[...], kvg_ref[...]).astype(BF16), wkv_ref[...])
    ang = pos_ref[...].astype(F32) * invf_ref[...]
    cc = jnp.cos(ang)
    ss = jnp.sin(ang) * sgn_ref[...]

    def rope(slab):
        return slab * cc + pltpu.roll(slab, MLA_ROPE, 1) * ss

    kpe = rope(kr_ref[...])[:, :MLA_ROPE].astype(BF16)
    hw = MLA_NOPE + 2 * MLA_ROPE
    lane = lax.broadcasted_iota(jnp.int32, (cq_ref.shape[0], MLA_V), 1)
    ones_col = jnp.where(lane == 0, 1.0, 0.0).astype(BF16)
    for h in range(MLA_HEADS):
        blk = qa[:, h * hw:(h + 1) * hw]
        q_ref[0, h, :, 0:MLA_NOPE] = (blk[:, :MLA_NOPE] * qscale).astype(BF16)
        qpe = rope(blk[:, MLA_NOPE:]) * qscale
        q_ref[0, h, :, MLA_NOPE:MLA_NOPE + MLA_ROPE] = qpe[:, :MLA_ROPE].astype(BF16)
        kvb = kva[:, h * hw:(h + 1) * hw]
        k_ref[0, h, :, 0:MLA_NOPE] = kvb[:, :MLA_NOPE].astype(BF16)
        k_ref[0, h, :, MLA_NOPE:MLA_NOPE + MLA_ROPE] = kpe
        v_ref[0, h, :, 0:MLA_V] = kvb[:, MLA_NOPE:].astype(BF16)
        v_ref[0, h, :, MLA_V:2 * MLA_V] = ones_col


def _mla_proj(cq, ckv, kr, pos, qg, kvg, wq_ext, wkv, nbatch, seq):
    tm = min(MLA_TM, seq)
    nl = seq // tm
    dqk = MLA_NOPE + MLA_ROPE
    half = MLA_ROPE // 2
    inv_freq = 1.0 / (ROPE_THETA ** (jnp.arange(0, MLA_ROPE, 2, dtype=F32) / MLA_ROPE))
    invf = jnp.tile(inv_freq, 4).reshape(1, 2 * MLA_ROPE)
    sgn = jnp.tile(jnp.concatenate([-jnp.ones((half,), F32), jnp.ones((half,), F32)]), 2).reshape(1, 2 * MLA_ROPE)
    row = lambda w: pl.BlockSpec((tm, w), lambda b, i: (b * nl + i, 0))
    full = lambda a, b_: pl.BlockSpec((a, b_), lambda b, i: (0, 0))
    head = lambda w: pl.BlockSpec((1, MLA_HEADS, tm, w), lambda b, i: (b, 0, i, 0))
    return pl.pallas_call(
        _mla_proj_kernel,
        grid=(nbatch, nl),
        in_specs=[row(MLA_Q_RANK), row(MLA_KV_RANK), row(2 * MLA_ROPE), row(1),
                  full(1, MLA_Q_RANK), full(1, MLA_KV_RANK),
                  full(MLA_Q_RANK, wq_ext.shape[1]), full(MLA_KV_RANK, wkv.shape[1]),
                  full(1, 2 * MLA_ROPE), full(1, 2 * MLA_ROPE)],
        out_specs=[head(dqk), head(dqk), head(2 * MLA_V)],
        out_shape=[jax.ShapeDtypeStruct((nbatch, MLA_HEADS, seq, dqk), BF16),
                   jax.ShapeDtypeStruct((nbatch, MLA_HEADS, seq, dqk), BF16),
                   jax.ShapeDtypeStruct((nbatch, MLA_HEADS, seq, 2 * MLA_V), BF16)],
        compiler_params=_cparams(("arbitrary", "arbitrary")),
        name="mla_proj",
    )(cq, ckv, kr, pos, qg.reshape(1, -1), kvg.reshape(1, -1), wq_ext, wkv, invf, sgn)


def _lane_groups(x):
    return [x[:, j * LANES:(j + 1) * LANES] for j in range(x.shape[1] // LANES)]


def _flash_kernel(q_ref, k_ref, v_ref, o_ref, s_ref, *, t):
    qi = pl.program_id(2)
    qh = [q_ref[0, 0, 0:t, :], q_ref[0, 0, t:2 * t, :]]

    def kblk(c):
        return k_ref[0, 0, pl.ds(pl.multiple_of(c * t, t), t), :]

    def vblk(c):
        return v_ref[0, 0, pl.ds(pl.multiple_of(c * t, t), t), :]

    def fold_max(s, mx):
        return functools.reduce(jnp.maximum, _lane_groups(s), mx)

    def pass1(c, mx):
        kc = kblk(c)
        out = []
        for h in range(2):
            s = _dot_nt(qh[h], kc)
            s_ref[h, c] = s
            out.append(fold_max(s, mx[h]))
        return tuple(out)

    neg = jnp.full((t, LANES), NEG_BIG, F32)
    mx0, mx1 = lax.fori_loop(0, 2 * qi, pass1, (neg, neg))
    c0, c1 = 2 * qi, 2 * qi + 1
    row = lax.broadcasted_iota(jnp.int32, (t, t), 0)
    col = lax.broadcasted_iota(jnp.int32, (t, t), 1)
    tri = col <= row
    k0, k1 = kblk(c0), kblk(c1)
    s00 = jnp.where(tri, _dot_nt(qh[0], k0), NEG_BIG)
    s10 = _dot_nt(qh[1], k0)
    s11 = jnp.where(tri, _dot_nt(qh[1], k1), NEG_BIG)
    s_ref[0, c0] = s00
    s_ref[1, c0] = s10
    s_ref[1, c1] = s11
    mx0 = fold_max(s00, mx0)
    mx1 = fold_max(s11, fold_max(s10, mx1))
    m = [jnp.broadcast_to(jnp.max(mx, axis=-1, keepdims=True), (t, LANES)) for mx in (mx0, mx1)]

    mt = [jnp.concatenate([mh] * (t // LANES), axis=1) for mh in m]

    def probs(h, c):
        return jnp.exp(s_ref[h, c] - mt[h]).astype(BF16)

    def pass2(j, carry):
        c = 2 * j
        vc = v_ref[0, 0, pl.ds(pl.multiple_of(c * t, 2 * t), 2 * t), :]
        out = []
        for h in range(2):
            p = jnp.concatenate([probs(h, c), probs(h, c + 1)], axis=1)
            out.append(carry[h] + _dot(p, vc))
        return tuple(out)

    za = jnp.zeros((t, 2 * MLA_V), F32)
    a0, a1 = lax.fori_loop(0, qi, pass2, (za, za))
    v0, v1 = vblk(c0), vblk(c1)
    a0 = a0 + _dot(probs(0, c0), v0)
    a1 = a1 + _dot(jnp.concatenate([probs(1, c0), probs(1, c1)], axis=1), jnp.concatenate([v0, v1], axis=0))
    o_ref[0, 0:t, :] = (a0[:, :MLA_V] / a0[:, MLA_V:MLA_V + 1]).astype(o_ref.dtype)
    o_ref[0, t:2 * t, :] = (a1[:, :MLA_V] / a1[:, MLA_V:MLA_V + 1]).astype(o_ref.dtype)


def _flash_attention(q, k, v):
    nbatch, nh, seq, dqk = q.shape
    t = min(ATT_T, seq // 2)
    nq = seq // (2 * t)
    return pl.pallas_call(
        functools.partial(_flash_kernel, t=t),
        grid=(nbatch, nh, nq),
        in_specs=[pl.BlockSpec((1, 1, 2 * t, dqk), lambda b, h, i: (b, h, i, 0)),
                  pl.BlockSpec((1, 1, seq, dqk), lambda b, h, i: (b, h, 0, 0)),
                  pl.BlockSpec((1, 1, seq, 2 * MLA_V), lambda b, h, i: (b, h, 0, 0))],
        out_specs=pl.BlockSpec((1, 2 * t, MLA_V), lambda b, h, i: (b, i, h)),
        out_shape=jax.ShapeDtypeStruct((nbatch, seq, nh * MLA_V), BF16),
        scratch_shapes=[pltpu.VMEM((2, seq // t, t, t), F32)],
        compiler_params=_cparams(("arbitrary", "arbitrary", "arbitrary")),
        name="flash_attn",
    )(q, k, v)


def _gelu_tanh(x):
    c = math.sqrt(2.0 / math.pi)
    return 0.5 * x * (1.0 + jnp.tanh(c * (x + 0.044715 * (x * x * x))))


def _mixer_out_kernel(ys_ref, ym_ref, x_ref, gate_ref, wglu_ref, bglu_ref, wo_ref, o_ref):
    y = _gelu_tanh(ys_ref[...].astype(F32))
    g = _dot(y.astype(BF16), wglu_ref[...]) + bglu_ref[...]
    s5o = (y * jax.nn.sigmoid(g)).astype(BF16)
    acc = _dot(s5o, wo_ref[0:S5_WIDTH, :]) + _dot(ym_ref[...], wo_ref[S5_WIDTH:, :])
    o_ref[...] = x_ref[...] + gate_ref[0] * acc


def _mixer_out(ys, ym, x2, mod, w_glu, b_glu, w_out, seq):
    t = x2.shape[0]
    tm = min(OUT_TM, seq)
    tpb = seq // tm
    row = lambda w: pl.BlockSpec((tm, w), lambda i: (i, 0))
    full = lambda a, b: pl.BlockSpec((a, b), lambda i: (0, 0))
    return pl.pallas_call(
        _mixer_out_kernel,
        grid=(t // tm,),
        in_specs=[row(S5_WIDTH), row(MLA_HEADS * MLA_V), row(D_MODEL), _mod_specs(tpb, 2),
                  full(S5_WIDTH, S5_WIDTH), full(1, S5_WIDTH), full(2 * S5_WIDTH, D_MODEL)],
        out_specs=row(D_MODEL),
        out_shape=jax.ShapeDtypeStruct((t, D_MODEL), F32),
        compiler_params=_cparams(("arbitrary",)),
        name="mixer_out",
    )(ys, ym, x2, mod, w_glu, b_glu.reshape(1, -1), w_out)


def _route_kernel(x_ref, g_ref, shift_ref, scale_ref, rw_ref, rb_ref, tri_ref,
                  h_ref, route_ref, gates_ref, cnt_ref, carry_ref):
    i = pl.program_id(0)

    @pl.when(i == 0)
    def _():
        carry_ref[...] = jnp.zeros(carry_ref.shape, F32)

    h = _norm_mod(x_ref[...], g_ref[...], scale_ref[0], shift_ref[0])
    _pack_store(h_ref, h)
    logits = jnp.dot(h, rw_ref[...], precision=lax.Precision.HIGHEST,
                     preferred_element_type=F32) + rb_ref[...]
    lane = lax.broadcasted_iota(jnp.int32, logits.shape, 1)
    lanef = lane.astype(F32)
    lg = jnp.where(lane < N_EXPERTS, logits, -jnp.inf)
    vals, hots, idxs = [], [], []
    sel = jnp.zeros(logits.shape, F32)
    for _ in range(TOP_K):
        m = jnp.max(lg, axis=-1, keepdims=True)
        idx = jnp.min(jnp.where(lg == m, lanef, float(LANES)), axis=-1, keepdims=True)
        hot = lanef == idx
        vals.append(m)
        idxs.append(idx)
        hots.append(hot)
        sel = jnp.where(hot, 1.0, sel)
        lg = jnp.where(hot, -jnp.inf, lg)
    es = [jnp.exp(v - vals[0]) for v in vals]
    denom = es[0] + es[1] + es[2] + es[3]
    before = _dot(tri_ref[...], sel.astype(BF16)) + carry_ref[...]
    route = jnp.zeros(logits.shape, F32)
    gates = jnp.zeros(logits.shape, F32)
    for k in range(TOP_K):
        rank = jnp.sum(jnp.where(hots[k], before, 0.0), axis=-1, keepdims=True)
        route = jnp.where(lane == k, idxs[k], route)
        route = jnp.where(lane == TOP_K + k, rank, route)
        gates = jnp.where(lane == k, es[k] / denom, gates)
    route_ref[...] = route.astype(jnp.int32)
    gates_ref[...] = gates
    carry_ref[...] = carry_ref[...] + jnp.sum(sel, axis=0, keepdims=True)
    cnt_ref[...] = carry_ref[...]


def _route(x2, g, mod, router_w, router_b, seq):
    t = x2.shape[0]
    tm = min(ROUTE_TM, seq)
    tpb = seq // tm
    rw = jnp.zeros((D_MODEL, LANES), F32).at[:, :N_EXPERTS].set(router_w)
    rb = jnp.zeros((1, LANES), F32).at[0, :N_EXPERTS].set(router_b)
    tri = jnp.asarray(np.tril(np.ones((tm, tm), np.float32), -1), BF16)
    row = lambda w: pl.BlockSpec((tm, w), lambda i: (i, 0))
    full = lambda a, b: pl.BlockSpec((a, b), lambda i: (0, 0))
    return pl.pallas_call(
        _route_kernel,
        grid=(t // tm,),
        in_specs=[row(D_MODEL), full(1, D_MODEL), _mod_specs(tpb, 0), _mod_specs(tpb, 1),
                  full(D_MODEL, LANES), full(1, LANES), full(tm, tm)],
        out_specs=[pl.BlockSpec((tm * PACK, LANES), lambda i: (i, 0)), row(LANES), row(LANES), full(1, LANES)],
        out_shape=[jax.ShapeDtypeStruct((t * PACK, LANES), U32),
                   jax.ShapeDtypeStruct((t, LANES), jnp.int32),
                   jax.ShapeDtypeStruct((t, LANES), F32),
                   jax.ShapeDtypeStruct((1, LANES), F32)],
        scratch_shapes=[pltpu.VMEM((1, LANES), F32)],
        compiler_params=_cparams(("arbitrary",)),
        name="moe_route",
    )(x2, g.reshape(1, -1), mod, mod, rw, rb, tri)


def _dispatch_kernel(pstart_ref, npad_ref, used_ref, dest_ref, h_ref, xs_hbm, zero_ref, sem, zsem):
    i = pl.program_id(0)
    n = DISPATCH_T
    nslot = xs_hbm.shape[0] // PACK

    def pad_copy(slot):
        return pltpu.make_async_copy(zero_ref.at[_prow(0), :], xs_hbm.at[_prow(slot), :], zsem)

    def tail_copy(t):
        return pltpu.make_async_copy(zero_ref, xs_hbm.at[_prow(used_ref[0] + t * FFN_TM, FFN_TM), :], zsem)

    @pl.when(i == 0)
    def _():
        zero_ref[...] = jnp.zeros(zero_ref.shape, U32)
        ntail = (nslot - used_ref[0]) // FFN_TM
        for wait in (False, True):
            for e in range(N_EXPERTS):
                def pad(j, c):
                    cp = pad_copy(pstart_ref[e] + j)
                    cp.wait() if wait else cp.start()
                    return c
                lax.fori_loop(0, npad_ref[e], pad, 0)

            def tail(t, c):
                cp = tail_copy(t)
                cp.wait() if wait else cp.start()
                return c
            lax.fori_loop(0, ntail, tail, 0)

    def row_copy(r, slot):
        return pltpu.make_async_copy(h_ref.at[_prow(r), :], xs_hbm.at[_prow(slot), :], sem)

    for k in range(TOP_K):
        def body(j, c):
            for par in range(2):
                r = 2 * j + par
                row_copy(r, dest_ref[0, 0, k * n + r]).start(priority=par)
            return c
        lax.fori_loop(0, n // 2, body, 0, unroll=4)

    def wait_rows(r, c):
        row_copy(0, 0).wait()
        return c

    lax.fori_loop(0, n * TOP_K, wait_rows, 0, unroll=8)


def _dispatch_rows(h_slabs, dest, pad_start, npad, used, ns):
    t = dest.shape[0]
    tm = min(DISPATCH_T, t)
    nt = t // tm
    idx = dest.reshape(nt, tm, TOP_K).transpose(0, 2, 1).reshape(nt, 1, TOP_K * tm)
    grid_spec = pltpu.PrefetchScalarGridSpec(
        num_scalar_prefetch=3,
        grid=(nt,),
        in_specs=[pl.BlockSpec((1, 1, TOP_K * tm), lambda i, *_: (i, 0, 0), memory_space=pltpu.SMEM),
                  pl.BlockSpec((tm * PACK, LANES), lambda i, *_: (i, 0))],
        out_specs=pl.BlockSpec(memory_space=pl.ANY),
        scratch_shapes=[pltpu.VMEM((FFN_TM * PACK, LANES), U32),
                        pltpu.SemaphoreType.DMA(()), pltpu.SemaphoreType.DMA(())],
    )
    return pl.pallas_call(
        _dispatch_kernel,
        grid_spec=grid_spec,
        out_shape=jax.ShapeDtypeStruct((ns * PACK, LANES), U32),
        compiler_params=_cparams(("arbitrary",)),
        name="moe_dispatch",
    )(pad_start, npad, used, idx, h_slabs)


PERM_W = 256
W1_BLK = 1024


def _w1_prep_kernel(w_ref, p_ref, o_ref):
    for b in range(W1_BLK // PERM_W):
        cols = slice(b * PERM_W, (b + 1) * PERM_W)
        o_ref[0, :, cols] = _dot(w_ref[0, :, cols].astype(BF16), p_ref[...]).astype(BF16)


def _w1_prep(w1):
    e, d, n = w1.shape
    pm = np.zeros((PERM_W, PERM_W), np.float32)
    jj = np.arange(PERM_W // 2)
    pm[2 * jj, jj] = 1.0
    pm[2 * jj + 1, PERM_W // 2 + jj] = 1.0
    return pl.pallas_call(
        _w1_prep_kernel,
        grid=(e, n // W1_BLK),
        in_specs=[pl.BlockSpec((1, d, W1_BLK), lambda i, j: (i, 0, j)),
                  pl.BlockSpec((PERM_W, PERM_W), lambda i, j: (0, 0))],
        out_specs=pl.BlockSpec((1, d, W1_BLK), lambda i, j: (i, 0, j)),
        out_shape=jax.ShapeDtypeStruct((e, d, n), BF16),
        compiler_params=_cparams(("arbitrary", "arbitrary")),
        name="moe_w1_prep",
    )(w1, jnp.asarray(pm, BF16))


def _ffn_kernel(we_ref, row0_ref, nt_ref, nv_ref, xs_hbm, w1_ref, b1_ref, w2_ref, b2_ref,
                ys_hbm, x_ref, acc_ref, w2b_ref, stage_ref, sem_in, sem_out):
    w = pl.program_id(0)
    f = pl.program_id(1)
    nf = pl.num_programs(1)
    nt = nt_ref[w]
    row0 = row0_ref[w]
    tm = FFN_TM
    hw = PERM_W // 2

    def in_copy(t, slot):
        return pltpu.make_async_copy(xs_hbm.at[_prow(row0 + t * tm, tm), :], stage_ref.at[slot], sem_in.at[slot])

    def out_copy(t, slot):
        return pltpu.make_async_copy(stage_ref.at[slot], ys_hbm.at[_prow(row0 + t * tm, tm), :], sem_out.at[slot])

    @pl.when(nt > 0)
    def _():
        w2b_ref[...] = w2_ref[0].astype(BF16)

        def tile(xv, av):
            a = _dot(xv[...], w1_ref[0]) + b1_ref[0]
            nblk = a.shape[1] // PERM_W
            glu = jnp.concatenate([a[:, b * PERM_W:b * PERM_W + hw] for b in range(nblk)], axis=1)
            lin = jnp.concatenate([a[:, b * PERM_W + hw:(b + 1) * PERM_W] for b in range(nblk)], axis=1)
            glu = jnp.minimum(glu, SWIGLU_LIMIT)
            lin = jnp.clip(lin, -SWIGLU_LIMIT, SWIGLU_LIMIT)
            act = glu * jax.nn.sigmoid(SWIGLU_ALPHA * glu) * (lin + 1.0)
            av[...] = av[...] + _dot(act.astype(BF16), w2b_ref[...])

        @pl.when(f == 0)
        def _():
            in_copy(0, 0).start()
            half = D_MODEL // 2

            def first(t, c):
                slot = t % 2

                @pl.when(t + 1 < nt)
                def _():
                    in_copy(t + 1, 1 - slot).start()

                in_copy(t, slot).wait()
                rows = pl.ds(pl.multiple_of(t * tm, tm), tm)
                for s in range(PACK):
                    lo, hi = _unpack_load(stage_ref, tm, s, lead=(slot,))
                    x_ref[rows, s * LANES:(s + 1) * LANES] = lo.astype(BF16)
                    x_ref[rows, half + s * LANES:half + (s + 1) * LANES] = hi.astype(BF16)
                acc_ref[rows, :] = jnp.broadcast_to(b2_ref[0], (tm, D_MODEL))
                tile(x_ref.at[rows, :], acc_ref.at[rows, :])
                return c

            lax.fori_loop(0, nt, first, 0)

        @pl.when(f > 0)
        def _():
            def pair(p, c):
                win = pl.ds(pl.multiple_of(p * 2 * tm, 2 * tm), 2 * tm)
                xw, aw = x_ref.at[win, :], acc_ref.at[win, :]
                for h in range(2):
                    tile(xw.at[h * tm:(h + 1) * tm, :], aw.at[h * tm:(h + 1) * tm, :])
                return c

            lax.fori_loop(0, nt // 2, pair, 0)

            @pl.when(nt % 2 == 1)
            def _():
                last = pl.ds(pl.multiple_of((nt - 1) * tm, tm), tm)
                tile(x_ref.at[last, :], acc_ref.at[last, :])

        @pl.when(f == nf - 1)
        def _():
            def emit(t, c):
                slot = t % 2

                @pl.when(t >= 2)
                def _():
                    out_copy(t - 2, slot).wait()

                _pack_store(stage_ref, acc_ref[pl.ds(pl.multiple_of(t * tm, tm), tm), :], lead=(slot,))
                out_copy(t, slot).start()
                return c

            lax.fori_loop(0, nt, emit, 0)
            for back in (2, 1):
                @pl.when(nt >= back)
                def _():
                    out_copy(nt - back, (nt - back) % 2).wait()

    @pl.when(jnp.logical_and(w == pl.num_programs(0) - 1, f == nf - 1))
    def _():
        used = nv_ref[1]
        ntail = (ys_hbm.shape[0] // PACK - used) // tm
        stage_ref[0] = jnp.zeros(stage_ref.shape[1:], U32)

        def fill(t, c):
            cp = pltpu.make_async_copy(stage_ref.at[0], ys_hbm.at[_prow(used + t * tm, tm), :], sem_out.at[0])
            cp.start()
            cp.wait()
            return c

        lax.fori_loop(0, ntail, fill, 0)


def _ffn(xs, w1p, b1p, w2b, b2, we, row0, ntile, nvalid):
    d = D_MODEL
    nf = EXPERT_FF // FFN_FC
    wmax = we.shape[0]

    def wmap(axis):
        def index(w, f, we_r, row0_r, nt_r, nv_r):
            fe = jnp.where(w < nv_r[0], f, nf - 1)
            return (we_r[w], 0, fe) if axis == 2 else (we_r[w], fe, 0)
        return index

    grid_spec = pltpu.PrefetchScalarGridSpec(
        num_scalar_prefetch=4,
        grid=(wmax, nf),
        in_specs=[pl.BlockSpec(memory_space=pl.ANY),
                  pl.BlockSpec((1, d, 2 * FFN_FC), wmap(2)),
                  pl.BlockSpec((1, 1, 2 * FFN_FC), wmap(2)),
                  pl.BlockSpec((1, FFN_FC, d), wmap(1)),
                  pl.BlockSpec((1, 1, d), lambda w, f, we_r, *_: (we_r[w], 0, 0))],
        out_specs=pl.BlockSpec(memory_space=pl.ANY),
        scratch_shapes=[pltpu.VMEM((FFN_R, d), BF16),
                        pltpu.VMEM((FFN_R, d), F32),
                        pltpu.VMEM((FFN_FC, d), BF16),
                        pltpu.VMEM((2, FFN_TM * PACK, LANES), U32),
                        pltpu.SemaphoreType.DMA((2,)),
                        pltpu.SemaphoreType.DMA((2,))],
    )
    return pl.pallas_call(
        _ffn_kernel,
        grid_spec=grid_spec,
        out_shape=jax.ShapeDtypeStruct(xs.shape, U32),
        compiler_params=_cparams(("arbitrary", "arbitrary")),
        name="moe_ffn",
    )(we, row0, ntile, nvalid, xs, w1p, b1p, w2b, b2)


def _combine_kernel(cur_ref, nxt_ref, ys_hbm, gates_ref, x_ref, gmod_ref, fg_ref, o_ref, buf_ref, sem, *, final):
    i = pl.program_id(0)
    n = x_ref.shape[0]
    slot = i % 2

    def row_copy(idx_ref, b, k, r):
        src = ys_hbm.at[_prow(idx_ref[0, 0, k * n + r]), :]
        return pltpu.make_async_copy(src, buf_ref.at[b, k, _prow(r), :], sem.at[b])

    def issue(idx_ref, b):
        for k in range(TOP_K):
            def body(j, c):
                for par in range(2):
                    row_copy(idx_ref, b, k, 2 * j + par).start(priority=par)
                return c
            lax.fori_loop(0, n // 2, body, 0, unroll=4)

    @pl.when(i == 0)
    def _():
        issue(cur_ref, 0)

    @pl.when(i + 1 < pl.num_programs(0))
    def _():
        issue(nxt_ref, 1 - slot)

    for k in range(TOP_K):
        def wait(r, c):
            row_copy(cur_ref, slot, k, r).wait()
            return c
        lax.fori_loop(0, n, wait, 0, unroll=8)

    gates = gates_ref[...]
    gk = [gates[:, k:k + 1] for k in range(TOP_K)]
    half = D_MODEL // 2
    for s in range(PACK):
        lo = hi = None
        for k in range(TOP_K):
            lo_k, hi_k = _unpack_load(buf_ref, n, s, lead=(slot, k))
            lo = gk[k] * lo_k if lo is None else lo + gk[k] * lo_k
            hi = gk[k] * hi_k if hi is None else hi + gk[k] * hi_k
        for base, acc in ((s * LANES, lo), (half + s * LANES, hi)):
            cols = slice(base, base + LANES)
            o_ref[:, cols] = x_ref[:, cols] + gmod_ref[0, :, cols] * acc
    if final:
        o_ref[...] = _rms(o_ref[...], fg_ref[...])


def _combine(ys, dest, gates, x2, mod, final_g, seq, final):
    t, d = x2.shape
    tm = min(COMBINE_T, seq)
    tpb = seq // tm
    nt = t // tm
    idx = dest.reshape(nt, tm, TOP_K).transpose(0, 2, 1).reshape(nt, 1, TOP_K * tm)
    row = lambda w: pl.BlockSpec((tm, w), lambda i: (i, 0))
    return pl.pallas_call(
        functools.partial(_combine_kernel, final=final),
        grid=(nt,),
        in_specs=[pl.BlockSpec((1, 1, tm * TOP_K), lambda i: (i, 0, 0), memory_space=pltpu.SMEM),
                  pl.BlockSpec((1, 1, tm * TOP_K), lambda i: (jnp.minimum(i + 1, nt - 1), 0, 0),
                               memory_space=pltpu.SMEM),
                  pl.BlockSpec(memory_space=pl.ANY),
                  row(LANES), row(d), _mod_specs(tpb, 2),
                  pl.BlockSpec((1, d), lambda i: (0, 0))],
        out_specs=row(d),
        out_shape=jax.ShapeDtypeStruct((t, d), F32),
        scratch_shapes=[pltpu.VMEM((2, TOP_K, tm * PACK, LANES), U32), pltpu.SemaphoreType.DMA((2,))],
        compiler_params=_cparams(("arbitrary",)),
        name="moe_combine",
    )(idx, idx, ys, gates, x2, mod, final_g.reshape(1, -1))


def _moe_sublayer(x2, c_mod, norm_g, router_w, router_b, w1, b1, w2, b2, final_g, seq, final):
    t = x2.shape[0]
    h, route, gates, cnt = _route(x2, norm_g, c_mod, router_w, router_b, seq)
    top_e = route[:, :TOP_K]
    rank = route[:, TOP_K:2 * TOP_K]
    counts = cnt[0, :N_EXPERTS].astype(jnp.int32)
    padded = ((counts + FFN_TM - 1) // FFN_TM) * FFN_TM
    starts = jnp.cumsum(padded) - padded
    dest = starts[top_e] + rank
    ns = t * TOP_K + N_EXPERTS * FFN_TM
    ns = ((ns + GATHER_T - 1) // GATHER_T) * GATHER_T
    used = jnp.sum(padded).astype(jnp.int32)
    n_items = (padded + FFN_R - 1) // FFN_R
    item_end = jnp.cumsum(n_items)
    wmax = (t * TOP_K) // FFN_R + N_EXPERTS
    wid = jnp.arange(wmax, dtype=jnp.int32)
    nvalid = item_end[-1].astype(jnp.int32)
    we = jnp.minimum(jnp.searchsorted(item_end, wid, side='right'), N_EXPERTS - 1).astype(jnp.int32)
    last_e = we[jnp.maximum(nvalid - 1, 0)]
    valid = wid < nvalid
    we = jnp.where(valid, we, last_e)
    local = wid - (item_end - n_items)[we]
    row0 = jnp.where(valid, starts[we] + local * FFN_R, 0).astype(jnp.int32)
    ntile = jnp.where(valid, jnp.minimum(FFN_R, padded[we] - local * FFN_R) // FFN_TM, 0).astype(jnp.int32)

    xs = _dispatch_rows(h, dest.astype(jnp.int32), (starts + counts).astype(jnp.int32),
                        (padded - counts).astype(jnp.int32), used.reshape(1), ns)
    hw = PERM_W // 2
    b1p = b1.reshape(N_EXPERTS, 2 * EXPERT_FF // PERM_W, hw, 2).transpose(0, 1, 3, 2).reshape(N_EXPERTS, 1, 2 * EXPERT_FF)
    ys = _ffn(xs, _w1_prep(w1), b1p, w2, b2.reshape(N_EXPERTS, 1, D_MODEL),
              we, row0, ntile, jnp.stack([nvalid, used]))
    return _combine(ys, dest.astype(jnp.int32), gates, x2, c_mod, final_g, seq, final)


CONV_TN = 512


def _conv_pw1_kernel(x_ref, g_ref, shift_ref, scale_ref, w_ref, b_ref, o_ref):
    h = _norm_mod(x_ref[...], g_ref[...], scale_ref[0], shift_ref[0]).astype(BF16)
    for c0 in range(0, D_MODEL, CONV_TN):
        ca = slice(c0, c0 + CONV_TN)
        cb = slice(D_MODEL + c0, D_MODEL + c0 + CONV_TN)
        a = _dot(h, w_ref[:, ca]) + b_ref[:, ca]
        b = _dot(h, w_ref[:, cb]) + b_ref[:, cb]
        o_ref[:, ca] = a * jax.nn.sigmoid(b)


def _conv_pw1(x2, g, mod, w_pw1, b_pw1, seq):
    t = x2.shape[0]
    tm = min(PROJ_TM, seq)
    tpb = seq // tm
    once = pl.Buffered(1)
    return pl.pallas_call(
        _conv_pw1_kernel,
        grid=(t // tm,),
        in_specs=[pl.BlockSpec((tm, D_MODEL), lambda i: (i, 0)),
                  pl.BlockSpec((1, D_MODEL), lambda i: (0, 0)),
                  _mod_specs(tpb, 0), _mod_specs(tpb, 1),
                  pl.BlockSpec((D_MODEL, 2 * D_MODEL), lambda i: (0, 0), pipeline_mode=once),
                  pl.BlockSpec((1, 2 * D_MODEL), lambda i: (0, 0))],
        out_specs=pl.BlockSpec((tm, D_MODEL), lambda i: (i, 0)),
        out_shape=jax.ShapeDtypeStruct((t, D_MODEL), F32),
        compiler_params=_cparams(("arbitrary",)),
        name="conv_pw1",
    )(x2, g.reshape(1, -1), mod, mod, w_pw1, b_pw1.reshape(1, -1))


def _conv_dw_kernel(y_ref, halo_ref, wdw_ref, bdw_ref, lng_ref, lnb_ref, wp_ref, bp_ref, x_ref, gate_ref,
                    o_ref, buf_ref, z_ref, win_ref, *, tpb):
    i = pl.program_id(0)
    tm = y_ref.shape[0]
    first = (i % tpb) == 0
    halo = halo_ref[...]
    buf_ref[0:CONV_HALO, :] = jnp.where(first, 0.0, halo)
    buf_ref[CONV_HALO:, :] = y_ref[...]
    cw = CONV_CW
    rb = min(CONV_RB, tm)
    off = CONV_HALO - (CONV_KERNEL - 1)
    for c0 in range(0, D_MODEL, cw):
        for r0 in range(0, tm, rb):
            acc = jnp.zeros((rb, cw), F32)
            for j in range(SUBLANES):
                taps = [k for k in range(CONV_KERNEL) if (off + k) % SUBLANES == j]
                span = max(off + k for k in taps) - j + rb
                win_ref[0:span, :] = buf_ref[r0 + j:r0 + j + span, c0:c0 + cw]
                for k in taps:
                    q = off + k - j
                    acc = acc + wdw_ref[k:k + 1, c0:c0 + cw] * win_ref[q:q + rb, :]
            z_ref[r0:r0 + rb, c0:c0 + cw] = acc
    z = z_ref[...] + bdw_ref[...]
    mu = jnp.mean(z, axis=-1, keepdims=True)
    zc = z - mu
    var = jnp.mean(zc * zc, axis=-1, keepdims=True)
    zn = zc * lax.rsqrt(var + LN_EPS) * lng_ref[...] + lnb_ref[...]
    act = (zn * jax.nn.sigmoid(zn)).astype(BF16)
    o_ref[...] = x_ref[...] + gate_ref[0] * (_dot(act, wp_ref[...]) + bp_ref[...])


def _conv_dw(y1, x2, mod, w_dw, b_dw, ln_g, ln_b, w_pw2, b_pw2, seq):
    t = x2.shape[0]
    tm = min(CONV_TM, seq)
    tpb = seq // tm
    hb = tm // CONV_HALO
    wdw = jnp.zeros((CONV_HALO, D_MODEL), F32).at[:CONV_KERNEL].set(w_dw)
    row = lambda: pl.BlockSpec((tm, D_MODEL), lambda i: (i, 0))
    vec = lambda: pl.BlockSpec((1, D_MODEL), lambda i: (0, 0))
    return pl.pallas_call(
        functools.partial(_conv_dw_kernel, tpb=tpb),
        grid=(t // tm,),
        in_specs=[row(),
                  pl.BlockSpec((CONV_HALO, D_MODEL), lambda i: (jnp.maximum(i * hb - 1, 0), 0)),
                  pl.BlockSpec((CONV_HALO, D_MODEL), lambda i: (0, 0)),
                  vec(), vec(), vec(),
                  pl.BlockSpec((D_MODEL, D_MODEL), lambda i: (0, 0)),
                  vec(), row(), _mod_specs(tpb, 2)],
        out_specs=row(),
        out_shape=jax.ShapeDtypeStruct((t, D_MODEL), F32),
        scratch_shapes=[pltpu.VMEM((tm + CONV_HALO, D_MODEL), F32), pltpu.VMEM((tm, D_MODEL), F32),
                        pltpu.VMEM((CONV_RB + CONV_HALO, CONV_CW), F32)],
        compiler_params=_cparams(("arbitrary",)),
        name="conv_dw_pw2",
    )(y1, y1, wdw, b_dw.reshape(1, -1), ln_g.reshape(1, -1), ln_b.reshape(1, -1),
      w_pw2, b_pw2.reshape(1, -1), x2, mod)


def _rope_swap_cols(w, heads):
    k = w.shape[0]
    half = MLA_ROPE // 2
    w3 = w.reshape(k, heads, MLA_NOPE + MLA_ROPE)
    pe = w3[:, :, MLA_NOPE:]
    sw = jnp.concatenate([pe[:, :, half:], pe[:, :, :half]], axis=-1)
    return jnp.concatenate([w3, sw], axis=-1).reshape(k, heads * (MLA_NOPE + 2 * MLA_ROPE))


def kernel(x, c, positions, l0_mix_norm_g, l0_mix_ada_w, l0_mix_ada_b, l0_w_in, l0_s5_lambda_re, l0_s5_lambda_im, l0_s5_log_step, l0_s5_b_re, l0_s5_b_im, l0_s5_c_re, l0_s5_c_im, l0_s5_d, l0_s5_w_glu, l0_s5_b_glu, l0_mla_q_norm_g, l0_mla_w_uq, l0_mla_kv_norm_g, l0_mla_w_ukv, l0_w_out, l0_moe_norm_g, l0_moe_ada_w, l0_moe_ada_b, l0_router_w, l0_router_b, l0_exp_w1, l0_exp_b1, l0_exp_w2, l0_exp_b2, l1_mix_norm_g, l1_mix_ada_w, l1_mix_ada_b, l1_conv_w_pw1, l1_conv_b_pw1, l1_conv_w_dw, l1_conv_b_dw, l1_conv_ln_g, l1_conv_ln_b, l1_conv_w_pw2, l1_conv_b_pw2, l1_moe_norm_g, l1_moe_ada_w, l1_moe_ada_b, l1_router_w, l1_router_b, l1_exp_w1, l1_exp_b1, l1_exp_w2, l1_exp_b2, final_norm_g):
    nbatch, seq, d = x.shape
    t = nbatch * seq
    x2 = x.reshape(t, d)
    pos = positions.reshape(t, 1).astype(jnp.int32)

    mod = _ada_mod(c, l0_mix_ada_w, l0_mix_ada_b)
    half = MLA_ROPE // 2
    kcol = S5_WIDTH + MLA_Q_RANK + MLA_KV_RANK
    w_in_ext = jnp.concatenate([l0_w_in, l0_w_in[:, kcol + half:], l0_w_in[:, kcol:kcol + half]], axis=1).astype(BF16)
    u, cq, ckv, kr = _proj_in(x2, l0_mix_norm_g, mod, w_in_ext, seq)

    kt, wt, vv, a16 = _s5_prep(l0_s5_lambda_re, l0_s5_lambda_im, l0_s5_log_step, l0_s5_b_re, l0_s5_b_im,
                               l0_s5_c_re, l0_s5_c_im, l0_s5_d)
    nchunk = seq // S5_CHUNK
    u_r = (u.reshape(nbatch, nchunk, S5_CHUNK, S5_GROUPS, S5_GROUP_CH)
           .transpose(3, 0, 1, 2, 4).reshape(S5_GROUPS, nbatch * nchunk, S5_ROW))
    y_r = _s5_main(u_r, kt, wt, vv, a16, nbatch, nchunk)
    ys = (y_r.reshape(S5_GROUPS, nbatch, nchunk, S5_CHUNK, S5_GROUP_CH)
          .transpose(1, 2, 3, 0, 4).reshape(t, S5_WIDTH))

    wq_ext = _rope_swap_cols(l0_mla_w_uq, MLA_HEADS).astype(BF16)
    q, k, v = _mla_proj(cq, ckv, kr, pos, l0_mla_q_norm_g, l0_mla_kv_norm_g, wq_ext,
                        l0_mla_w_ukv.astype(BF16), nbatch, seq)
    ym = _flash_attention(q, k, v).reshape(t, MLA_HEADS * MLA_V)
    x2 = _mixer_out(ys, ym, x2, mod, l0_s5_w_glu.astype(BF16), l0_s5_b_glu, l0_w_out.astype(BF16), seq)

    mod = _ada_mod(c, l0_moe_ada_w, l0_moe_ada_b)
    x2 = _moe_sublayer(x2, mod, l0_moe_norm_g, l0_router_w, l0_router_b, l0_exp_w1, l0_exp_b1,
                       l0_exp_w2, l0_exp_b2, final_norm_g, seq, final=False)

    mod = _ada_mod(c, l1_mix_ada_w, l1_mix_ada_b)
    y1 = _conv_pw1(x2, l1_mix_norm_g, mod, l1_conv_w_pw1.astype(BF16), l1_conv_b_pw1, seq)
    x2 = _conv_dw(y1, x2, mod, l1_conv_w_dw, l1_conv_b_dw, l1_conv_ln_g, l1_conv_ln_b,
                  l1_conv_w_pw2.astype(BF16), l1_conv_b_pw2, seq)

    mod = _ada_mod(c, l1_moe_ada_w, l1_moe_ada_b)
    x2 = _moe_sublayer(x2, mod, l1_moe_norm_g, l1_router_w, l1_router_b, l1_exp_w1, l1_exp_b1,
                       l1_exp_w2, l1_exp_b2, final_norm_g, seq, final=True)
    return x2.reshape(nbatch, seq, d)
```
